```python
import math
import jax
import jax.numpy as jnp
from jax import lax
import numpy as np

D_MODEL = 1024
BATCH = 2
SEQ = 8192
DEPTH = 1
DEC_BATCH = 32
DEC_SEQ = 1
PAST_LEN = 8192
PAGE_SIZE = 128

HEAD_DIM = 64
N_ATT_HEADS = 8
N_IDX_HEADS = 8
IDX_DIM = 64
TOPK_MAX = 256
N_RET_HEADS = 4
N_MEM_HEADS = 4
N_MEM = 256
D_FF = 2816
RET_CHUNK = 128
Q_BLOCK = 128
EPS = 1e-6
ROPE_BASE = 10000.0

ATT_W = N_ATT_HEADS * HEAD_DIM
RET_W = N_RET_HEADS * HEAD_DIM
MEM_W = N_MEM_HEADS * HEAD_DIM
MIX_W = ATT_W + RET_W + MEM_W
IDXQ_W = N_IDX_HEADS * IDX_DIM
SPLITS = (ATT_W, ATT_W, ATT_W, IDXQ_W, IDX_DIM, N_IDX_HEADS, RET_W, RET_W, RET_W, RET_W, MEM_W)
IN_W = ATT_W * 3 + IDXQ_W + IDX_DIM + N_IDX_HEADS + RET_W * 4 + MEM_W

kernel_name = 'hymba_dsa_retention_macaron_step'

F32 = jnp.float32


def rms_norm(x, g):
    xf = x.astype(F32)
    y = xf * lax.rsqrt(jnp.mean(xf * xf, axis=-1, keepdims=True) + EPS)
    return (y * g.astype(F32)).astype(x.dtype)


def macaron_half(x, norm_g, w_gu, w_down):
    h = rms_norm(x, norm_g)
    gate, up = jnp.split(h @ w_gu, 2, axis=-1)
    return x + 0.5 * ((jax.nn.silu(gate) * up) @ w_down)


def rotary(x, pos):
    half = HEAD_DIM // 2
    inv = ROPE_BASE ** (-jnp.arange(half, dtype=F32) / half)
    ang = pos.astype(F32)[:, None] * inv[None, :]
    cos = jnp.cos(ang)[:, None, :]
    sin = jnp.sin(ang)[:, None, :]
    xf = x.astype(F32)
    x1, x2 = xf[..., :half], xf[..., half:]
    return jnp.concatenate([x1 * cos - x2 * sin, x1 * sin + x2 * cos], axis=-1).astype(x.dtype)


def ret_log_decay():
    return jnp.log1p(-(2.0 ** (-5.0 - jnp.arange(N_RET_HEADS, dtype=F32))))


def project_mix(h, pos, w_in, att_q_norm_g, att_k_norm_g, mem_q_norm_g):
    B, T = h.shape[0], h.shape[1]
    offs = np.cumsum(np.array(SPLITS))[:-1].tolist()
    q, k, v, qi, ki, wi, rq, rk, rv, rg, mq = jnp.split(h @ w_in, offs, axis=-1)
    q = rms_norm(q.reshape(B, T, N_ATT_HEADS, HEAD_DIM), att_q_norm_g)
    k = rms_norm(k.reshape(B, T, N_ATT_HEADS, HEAD_DIM), att_k_norm_g)
    v = v.reshape(B, T, N_ATT_HEADS, HEAD_DIM)
    qi = qi.reshape(B, T, N_IDX_HEADS, IDX_DIM)
    rq = rotary(rq.reshape(B, T, N_RET_HEADS, HEAD_DIM), pos)
    rk = rotary(rk.reshape(B, T, N_RET_HEADS, HEAD_DIM), pos) * (HEAD_DIM ** -0.5)
    rv = rv.reshape(B, T, N_RET_HEADS, HEAD_DIM)
    mq = rms_norm(mq.reshape(B, T, N_MEM_HEADS, HEAD_DIM), mem_q_norm_g)
    return q, k, v, qi, ki, wi, rq, rk, rv, rg, mq


def mix_output(att, ret, rg, mo, ret_gn_g, w_out):
    B, T = att.shape[0], att.shape[1]
    dt = att.dtype
    ret = rms_norm(ret, ret_gn_g).astype(dt).reshape(B, T, RET_W)
    cat = jnp.concatenate([att.reshape(B, T, ATT_W), jax.nn.silu(rg) * ret, mo.reshape(B, T, MEM_W)], axis=-1)
    return cat @ w_out


def indexer_scores(qi, wi, ki):
    logits = jnp.einsum('bthd,bsd->bths', qi.astype(F32), ki.astype(F32)) * (IDX_DIM ** -0.5)
    return jnp.einsum('bth,bths->bts', wi.astype(F32) * (N_IDX_HEADS ** -0.5), jax.nn.relu(logits))


def select_topk(iscore, valid, topk):
    masked = jnp.where(valid, iscore, -jnp.inf)
    top_s, top_i = lax.top_k(masked, topk)
    return top_i, jnp.isfinite(top_s)


def sparse_attend(q, k_sel, v_sel, sel_valid):
    s = jnp.einsum('bthd,btkhd->bthk', q.astype(F32), k_sel.astype(F32)) * (HEAD_DIM ** -0.5)
    s = jnp.where(sel_valid[:, :, None, :], s, -jnp.inf)
    p = jax.nn.softmax(s, axis=-1)
    return jnp.einsum('bthk,btkhd->bthd', p, v_sel.astype(F32)).astype(q.dtype)


def dsa_prompt(q, k, v, qi, ki, wi):
    B, S = q.shape[0], q.shape[1]
    topk = min(TOPK_MAX, S // 4)
    nblk = S // Q_BLOCK
    key_pos = jnp.arange(S, dtype=jnp.int32)
    bidx = jnp.arange(B, dtype=jnp.int32)[:, None, None]

    def blocks(a):
        return jnp.moveaxis(a.reshape((B, nblk, Q_BLOCK) + a.shape[2:]), 1, 0)

    def one_block(args):
        qb, qib, wib, start = args
        qpos = start + jnp.arange(Q_BLOCK, dtype=jnp.int32)
        iscore = indexer_scores(qib, wib, ki)
        valid = key_pos[None, None, :] <= qpos[None, :, None]
        top_i, sel_valid = select_topk(iscore, valid, topk)
        return sparse_attend(qb, k[bidx, top_i], v[bidx, top_i], sel_valid)

    starts = jnp.arange(nblk, dtype=jnp.int32) * Q_BLOCK
    out = lax.map(one_block, (blocks(q), blocks(qi), blocks(wi), starts))
    return jnp.moveaxis(out, 0, 1).reshape(B, S, N_ATT_HEADS, HEAD_DIM)


def dsa_sample(q, k, v, qi, ki, wi, cache_k, cache_v, cache_kidx, page_table):
    B, T = q.shape[0], q.shape[1]
    past_len = page_table.shape[1] * PAGE_SIZE
    topk = min(TOPK_MAX, (past_len + T) // 4)
    bidx = jnp.arange(B, dtype=jnp.int32)[:, None, None]
    ki_past = cache_kidx[page_table].reshape(B, past_len, IDX_DIM)
    iscore = jnp.concatenate([indexer_scores(qi, wi, ki_past), indexer_scores(qi, wi, ki)], axis=-1)
    t = jnp.arange(T, dtype=jnp.int32)
    valid = jnp.concatenate([jnp.ones((T, past_len), dtype=bool), t[None, :] <= t[:, None]], axis=-1)[None]
    top_i, sel_valid = select_topk(iscore, valid, topk)
    is_past = (top_i < past_len)[..., None, None]
    ip = jnp.minimum(top_i, past_len - 1)
    phys = page_table[bidx, ip // PAGE_SIZE]
    off = ip % PAGE_SIZE
    inew = jnp.clip(top_i - past_len, 0, T - 1)
    k_sel = jnp.where(is_past, cache_k[phys, off].astype(k.dtype), k[bidx, inew])
    v_sel = jnp.where(is_past, cache_v[phys, off].astype(v.dtype), v[bidx, inew])
    return sparse_attend(q, k_sel, v_sel, sel_valid)


def retention_chunk(S, q, k, v, log_g):
    C = q.shape[1]
    i = jnp.arange(C, dtype=F32)
    diff = i[:, None] - i[None, :]
    causal = diff >= 0
    decay = jnp.where(causal[None], jnp.exp(log_g[:, None, None] * jnp.where(causal, diff, 0.0)[None]), 0.0)
    qf, kf, vf = q.astype(F32), k.astype(F32), v.astype(F32)
    scores = jnp.einsum('bihd,bjhd->bhij', qf, kf) * decay[None]
    inner = jnp.einsum('bhij,bjhd->bihd', scores, vf)
    q_dec = jnp.exp(log_g[None, :] * (i[:, None] + 1.0))
    cross = jnp.einsum('bihk,bhkv->bihv', qf, S) * q_dec[None, :, :, None]
    k_dec = jnp.exp(log_g[None, :] * (C - 1.0 - i[:, None]))
    S_new = jnp.exp(log_g * C)[None, :, None, None] * S + jnp.einsum('bjhk,bjhv->bhkv', kf * k_dec[None, :, :, None], vf)
    return S_new, inner + cross


def retention_prompt(rq, rk, rv):
    B, S = rq.shape[0], rq.shape[1]
    nc = S // RET_CHUNK
    log_g = ret_log_decay()

    def blocks(a):
        return jnp.moveaxis(a.reshape((B, nc, RET_CHUNK) + a.shape[2:]), 1, 0)

    def step(state, args):
        qc, kc, vc = args
        return retention_chunk(state, qc, kc, vc, log_g)

    S0 = jnp.zeros((B, N_RET_HEADS, HEAD_DIM, HEAD_DIM), F32)
    S_fin, o = lax.scan(step, S0, (blocks(rq), blocks(rk), blocks(rv)))
    return jnp.moveaxis(o, 0, 1).reshape(B, S, N_RET_HEADS, HEAD_DIM), S_fin


def memory_kv(mem, norm_g, w_kv, k_norm_g):
    B, M = mem.shape[0], mem.shape[1]
    mk, mv = jnp.split(rms_norm(mem, norm_g) @ w_kv, 2, axis=-1)
    mk = rms_norm(mk.reshape(B, M, N_MEM_HEADS, HEAD_DIM), k_norm_g)
    return mk, mv.reshape(B, M, N_MEM_HEADS, HEAD_DIM)


def mem_attend(mq, mk, mv):
    s = jnp.einsum('bthd,bmhd->bhtm', mq.astype(F32), mk.astype(F32)) * (HEAD_DIM ** -0.5)
    p = jax.nn.softmax(s, axis=-1)
    return jnp.einsum('bhtm,bmhd->bthd', p, mv.astype(F32)).astype(mq.dtype)


def setup_inputs(seed: int = 0) -> dict:
    key = jax.random.key(seed)
    ks = jax.random.split(key, 40)
    n_pages = PAST_LEN // PAGE_SIZE
    n_used = DEC_BATCH * n_pages
    n_pool = n_used + max(1, n_used // 4)

    def nrm(k, shape, scale=1.0):
        return jax.random.normal(k, shape, F32) * scale

    def gain(k, shape):
        return 1.0 + 0.01 * jax.random.normal(k, shape, F32)

    page_table = jax.random.permutation(ks[0], n_pool)[:n_used].reshape(DEC_BATCH, n_pages).astype(jnp.int32)
    return {
        'x_prompt': nrm(ks[1], (BATCH, SEQ, D_MODEL)),
        'x_sample': nrm(ks[2], (DEC_BATCH, DEC_SEQ, D_MODEL)),
        'mem_prompt': nrm(ks[3], (BATCH, N_MEM, D_MODEL)),
        'cache_k': nrm(ks[4], (DEPTH, n_pool, PAGE_SIZE, N_ATT_HEADS, HEAD_DIM)),
        'cache_v': nrm(ks[5], (DEPTH, n_pool, PAGE_SIZE, N_ATT_HEADS, HEAD_DIM)),
        'cache_kidx': nrm(ks[6], (DEPTH, n_pool, PAGE_SIZE, IDX_DIM)),
        'state_ret': nrm(ks[7], (DEPTH, DEC_BATCH, N_RET_HEADS, HEAD_DIM, HEAD_DIM), 0.5),
        'cache_mem_k': nrm(ks[8], (DEPTH, DEC_BATCH, N_MEM, N_MEM_HEADS, HEAD_DIM)),
        'cache_mem_v': nrm(ks[9], (DEPTH, DEC_BATCH, N_MEM, N_MEM_HEADS, HEAD_DIM)),
        'page_table': page_table,
        'ffn1_norm_g': gain(ks[10], (DEPTH, D_MODEL)),
        'ffn1_w_gu': nrm(ks[11], (DEPTH, D_MODEL, 2 * D_FF), D_MODEL ** -0.5),
        'ffn1_w_down': nrm(ks[12], (DEPTH, D_FF, D_MODEL), D_FF ** -0.5),
        'mix_norm_g': gain(ks[13], (DEPTH, D_MODEL)),
        'w_in': nrm(ks[14], (DEPTH, D_MODEL, IN_W), D_MODEL ** -0.5),
        'att_q_norm_g': gain(ks[15], (DEPTH, HEAD_DIM)),
        'att_k_norm_g': gain(ks[16], (DEPTH, HEAD_DIM)),
        'ret_gn_g': gain(ks[17], (DEPTH, N_RET_HEADS, HEAD_DIM)),
        'mem_norm_g': gain(ks[18], (DEPTH, D_MODEL)),
        'w_mem_kv': nrm(ks[19], (DEPTH, D_MODEL, 2 * MEM_W), D_MODEL ** -0.5),
        'mem_q_norm_g': gain(ks[20], (DEPTH, HEAD_DIM)),
        'mem_k_norm_g': gain(ks[21], (DEPTH, HEAD_DIM)),
        'w_out': nrm(ks[22], (DEPTH, MIX_W, D_MODEL), MIX_W ** -0.5),
        'ffn2_norm_g': gain(ks[23], (DEPTH, D_MODEL)),
        'ffn2_w_gu': nrm(ks[24], (DEPTH, D_MODEL, 2 * D_FF), D_MODEL ** -0.5),
        'ffn2_w_down': nrm(ks[25], (DEPTH, D_FF, D_MODEL), D_FF ** -0.5),
    }


def reference(x_prompt, x_sample, mem_prompt, cache_k, cache_v, cache_kidx, state_ret, cache_mem_k, cache_mem_v,
              page_table, ffn1_norm_g, ffn1_w_gu, ffn1_w_down, mix_norm_g, w_in, att_q_norm_g, att_k_norm_g,
              ret_gn_g, mem_norm_g, w_mem_kv, mem_q_norm_g, mem_k_norm_g, w_out, ffn2_norm_g, ffn2_w_gu, ffn2_w_down):
    past_len = page_table.shape[1] * PAGE_SIZE
    pos_p = jnp.arange(x_prompt.shape[1], dtype=jnp.int32)
    pos_s = past_len + jnp.arange(x_sample.shape[1], dtype=jnp.int32)
    log_g = ret_log_decay()
    yp, ys = x_prompt, x_sample
    kp_l, vp_l, kip_l, rp_l, mkp_l, mvp_l = [], [], [], [], [], []
    ks_l, vs_l, kis_l, rs_l = [], [], [], []
    for l in range(DEPTH):
        yp = macaron_half(yp, ffn1_norm_g[l], ffn1_w_gu[l], ffn1_w_down[l])
        hp = rms_norm(yp, mix_norm_g[l])
        q, k, v, qi, ki, wi, rq, rk, rv, rg, mq = project_mix(hp, pos_p, w_in[l], att_q_norm_g[l], att_k_norm_g[l], mem_q_norm_g[l])
        mk, mv = memory_kv(mem_prompt, mem_norm_g[l], w_mem_kv[l], mem_k_norm_g[l])
        att = dsa_prompt(q, k, v, qi, ki, wi)
        ret, s_fin = retention_prompt(rq, rk, rv)
        mo = mem_attend(mq, mk, mv)
        yp = yp + mix_output(att, ret, rg, mo, ret_gn_g[l], w_out[l])
        yp = macaron_half(yp, ffn2_norm_g[l], ffn2_w_gu[l], ffn2_w_down[l])
        kp_l.append(k); vp_l.append(v); kip_l.append(ki); rp_l.append(s_fin); mkp_l.append(mk); mvp_l.append(mv)
        ys = macaron_half(ys, ffn1_norm_g[l], ffn1_w_gu[l], ffn1_w_down[l])
        hs = rms_norm(ys, mix_norm_g[l])
        q, k, v, qi, ki, wi, rq, rk, rv, rg, mq = project_mix(hs, pos_s, w_in[l], att_q_norm_g[l], att_k_norm_g[l], mem_q_norm_g[l])
        att = dsa_sample(q, k, v, qi, ki, wi, cache_k[l], cache_v[l], cache_kidx[l], page_table)
        s_new, ret = retention_chunk(state_ret[l].astype(F32), rq, rk, rv, log_g)
        mo = mem_attend(mq, cache_mem_k[l], cache_mem_v[l])
        ys = ys + mix_output(att, ret, rg, mo, ret_gn_g[l], w_out[l])
        ys = macaron_half(ys, ffn2_norm_g[l], ffn2_w_gu[l], ffn2_w_down[l])
        ks_l.append(k); vs_l.append(v); kis_l.append(ki); rs_l.append(s_new)
    return (yp, ys,
            jnp.stack(kp_l), jnp.stack(vp_l), jnp.stack(kip_l), jnp.stack(rp_l), jnp.stack(mkp_l), jnp.stack(mvp_l),
            jnp.stack(ks_l), jnp.stack(vs_l), jnp.stack(kis_l), jnp.stack(rs_l))
```

```python
import functools
import math

import numpy as np
import jax
import jax.numpy as jnp
from jax import lax
from jax.experimental import pallas as pl
from jax.experimental.pallas import tpu as pltpu

F32 = jnp.float32
BF16 = jnp.bfloat16
I32 = jnp.int32

HEAD_DIM = 64
N_ATT_HEADS = 8
N_IDX_HEADS = 8
IDX_DIM = 64
TOPK_MAX = 256
N_RET_HEADS = 4
N_MEM_HEADS = 4
PAGE_SIZE = 128
EPS = 1e-6
ROPE_BASE = 10000.0

ATT_W = N_ATT_HEADS * HEAD_DIM
RET_W = N_RET_HEADS * HEAD_DIM
MEM_W = N_MEM_HEADS * HEAD_DIM
IDXQ_W = N_IDX_HEADS * IDX_DIM
LANES = 128
KW_W = LANES
C_Q = 0
C_K = C_Q + ATT_W
C_V = C_K + ATT_W
C_QI = C_V + ATT_W
C_KW = C_QI + IDXQ_W
C_RQ = C_KW + KW_W
C_RK = C_RQ + RET_W
C_RV = C_RK + RET_W
C_RG = C_RV + RET_W
C_MQ = C_RG + RET_W
PROJ_W = C_MQ + MEM_W

QK_SCALE = HEAD_DIM ** -0.5
IDX_SCALE = IDX_DIM ** -0.5
IDX_HEAD_SCALE = N_IDX_HEADS ** -0.5
NEG_BIG = -1e30
INT_MIN = -(2 ** 31)
NEG_INF_KEY = int(np.int32(np.uint32(0x807FFFFF)))
POS_INF_KEY = 0x7F800000
LOG_G = [float(np.log1p(np.float32(-(2.0 ** (-5.0 - h))))) for h in range(N_RET_HEADS)]

VMEM_LIMIT = 56 * 1024 * 1024
ROW_TILE = 512
KEY_CHUNK = 512
Q_TILE = LANES
RET_CHUNK = 128
PAGES_PER_STEP = 8


def _cparams(sem):
    return pltpu.CompilerParams(dimension_semantics=sem, vmem_limit_bytes=VMEM_LIMIT)


def _resident(shape):
    nd = len(shape)
    return pl.BlockSpec(shape, lambda *_: (0,) * nd, pipeline_mode=pl.Buffered(1))


def _rms(x, g):
    return x * lax.rsqrt(jnp.mean(x * x, axis=-1, keepdims=True) + EPS) * g


def _head_rms(z, g, bd):
    z2 = z * z
    hi = z2.astype(BF16)
    lo = (z2 - hi.astype(F32)).astype(BF16)
    ss = jnp.dot(hi, bd, preferred_element_type=F32) + jnp.dot(lo, bd, preferred_element_type=F32)
    return z * lax.rsqrt(ss * (1.0 / HEAD_DIM) + EPS) * g


def _nt(a, b):
    return lax.dot_general(a, b, (((1,), (1,)), ((), ())), preferred_element_type=F32)


def _to_key(x):
    b = lax.bitcast_convert_type(x, I32)
    return jnp.where(b < 0, b ^ jnp.int32(0x7FFFFFFF), b)


def _ffn_body(x_ref, g_ref, wgu_ref, wd_ref, o_ref, *, d_ff, fc):
    x = x_ref[...]
    h = _rms(x, g_ref[...]).astype(BF16)
    acc = jnp.zeros(x.shape, F32)
    for c in range(d_ff // fc):
        gate = jnp.dot(h, wgu_ref[:, c * fc:(c + 1) * fc], preferred_element_type=F32)
        up = jnp.dot(h, wgu_ref[:, d_ff + c * fc:d_ff + (c + 1) * fc], preferred_element_type=F32)
        act = (gate * jax.nn.sigmoid(gate) * up).astype(BF16)
        acc = acc + jnp.dot(act, wd_ref[c * fc:(c + 1) * fc, :], preferred_element_type=F32)
    o_ref[...] = x + 0.5 * acc


def _ffn(x, g, wgu, wd):
    t, d = x.shape
    d_ff = wd.shape[0]
    tm = min(ROW_TILE, t)
    fc = d_ff // 2 if (d_ff // 2) % LANES == 0 else d_ff
    return pl.pallas_call(
        functools.partial(_ffn_body, d_ff=d_ff, fc=fc),
        grid=(t // tm,),
        in_specs=[pl.BlockSpec((tm, d), lambda i: (i, 0)), _resident((1, d)),
                  _resident(wgu.shape), _resident(wd.shape)],
        out_specs=pl.BlockSpec((tm, d), lambda i: (i, 0)),
        out_shape=jax.ShapeDtypeStruct((t, d), F32),
        compiler_params=_cparams(("parallel",)),
        name="ffn_half",
    )(x, g, wgu, wd)


def _proj_body(x_ref, g_ref, w_ref, gq_ref, gk_ref, gm_ref, cos_ref, sin_ref, bd_ref,
               q_ref, k_ref, kb_ref, v_ref, vt_ref, qi_ref, kw_ref, ki_ref, ki2_ref,
               rq_ref, rk_ref, rv_ref, rg_ref, mq_ref):
    h = _rms(x_ref[...], g_ref[...]).astype(BF16)

    def proj(a, width):
        return jnp.dot(h, w_ref[:, a:a + width], preferred_element_type=F32)

    bd = bd_ref[...]
    q = _head_rms(proj(C_Q, ATT_W), gq_ref[...], bd)
    q_ref[...] = (q * QK_SCALE).astype(BF16)
    k = _head_rms(proj(C_K, ATT_W), gk_ref[...], bd)
    k_ref[...] = k
    kb_ref[...] = k.astype(BF16)
    v = proj(C_V, ATT_W)
    v_ref[...] = v
    vt_ref[...] = jnp.transpose(v).astype(BF16)
    qi_ref[...] = proj(C_QI, IDXQ_W).astype(BF16)
    kw = proj(C_KW, KW_W)
    kw_ref[...] = kw
    ki = kw[:, :IDX_DIM]
    ki_ref[...] = ki
    kib = ki.astype(BF16)
    ki2_ref[...] = jnp.concatenate([kib, kib], axis=1)

    lane = lax.broadcasted_iota(I32, (1, RET_W), 1)
    first_half = (lane % HEAD_DIM) < (HEAD_DIM // 2)
    cos = cos_ref[...]
    sin = sin_ref[...]

    def rot(x):
        swapped = jnp.where(first_half, pltpu.roll(x, RET_W - HEAD_DIM // 2, axis=1),
                            pltpu.roll(x, HEAD_DIM // 2, axis=1))
        return x * cos + swapped * sin

    rq_ref[...] = rot(proj(C_RQ, RET_W))
    rk_ref[...] = rot(proj(C_RK, RET_W)) * QK_SCALE
    rv_ref[...] = proj(C_RV, RET_W)
    rg_ref[...] = proj(C_RG, RET_W)
    mq = _head_rms(proj(C_MQ, MEM_W), gm_ref[...], bd[:MEM_W, :MEM_W])
    mq_ref[...] = (mq * QK_SCALE).astype(BF16)


def _proj(y, g, w, gq, gk, gm, cos, sin, bd, n_batch):
    t, d = y.shape
    s_len = t // n_batch
    tm = min(ROW_TILE, s_len)
    nsb = s_len // tm
    row = lambda w_: pl.BlockSpec((tm, w_), lambda i: (i, 0))
    tab = pl.BlockSpec((tm, RET_W), lambda i: (i % nsb, 0))
    out_shapes = [
        (jax.ShapeDtypeStruct((t, ATT_W), BF16), row(ATT_W)),
        (jax.ShapeDtypeStruct((t, ATT_W), F32), row(ATT_W)),
        (jax.ShapeDtypeStruct((t, ATT_W), BF16), row(ATT_W)),
        (jax.ShapeDtypeStruct((t, ATT_W), F32), row(ATT_W)),
        (jax.ShapeDtypeStruct((n_batch, nsb, ATT_W, tm), BF16),
         pl.BlockSpec((None, None, ATT_W, tm), lambda i: (i // nsb, i % nsb, 0, 0))),
        (jax.ShapeDtypeStruct((t, IDXQ_W), BF16), row(IDXQ_W)),
        (jax.ShapeDtypeStruct((t, KW_W), F32), row(KW_W)),
        (jax.ShapeDtypeStruct((t, IDX_DIM), F32), row(IDX_DIM)),
        (jax.ShapeDtypeStruct((t, 2 * IDX_DIM), BF16), row(2 * IDX_DIM)),
        (jax.ShapeDtypeStruct((t, RET_W), F32), row(RET_W)),
        (jax.ShapeDtypeStruct((t, RET_W), F32), row(RET_W)),
        (jax.ShapeDtypeStruct((t, RET_W), F32), row(RET_W)),
        (jax.ShapeDtypeStruct((t, RET_W), F32), row(RET_W)),
        (jax.ShapeDtypeStruct((t, MEM_W), BF16), row(MEM_W)),
    ]
    return pl.pallas_call(
        _proj_body,
        grid=(t // tm,),
        in_specs=[row(d), _resident((1, d)), _resident(w.shape), _resident((1, ATT_W)),
                  _resident((1, ATT_W)), _resident((1, MEM_W)), tab, tab, _resident(bd.shape)],
        out_specs=[s for _, s in out_shapes],
        out_shape=[s for s, _ in out_shapes],
        compiler_params=_cparams(("parallel",)),
        name="mix_proj",
    )(y, g, w, gq, gk, gm, cos, sin, bd)


def _memkv_body(x_ref, g_ref, w_ref, gk_ref, bd_ref, mk_ref, mv_ref):
    h = _rms(x_ref[...], g_ref[...]).astype(BF16)
    z = jnp.dot(h, w_ref[...], preferred_element_type=F32)
    mk_ref[...] = _head_rms(z[:, :MEM_W], gk_ref[...], bd_ref[...])
    mv_ref[...] = z[:, MEM_W:]


def _memkv(mem, g, w, gk, bd):
    t, d = mem.shape
    tm = min(ROW_TILE, t)
    row = lambda w_: pl.BlockSpec((tm, w_), lambda i: (i, 0))
    return pl.pallas_call(
        _memkv_body,
        grid=(t // tm,),
        in_specs=[row(d), _resident((1, d)), _resident(w.shape), _resident((1, MEM_W)), _resident(bd.shape)],
        out_specs=[row(MEM_W), row(MEM_W)],
        out_shape=[jax.ShapeDtypeStruct((t, MEM_W), F32)] * 2,
        compiler_params=_cparams(("parallel",)),
        name="mem_kv",
    )(mem, g, w, gk, bd)


def _dsa_body(q_ref, qi_ref, kw_ref, k_ref, vt_ref, ki2_ref, o_ref,
              key_scr, qi2_scr, q2_scr, cut_scr, m_scr, l_scr, acc_scr, *, topk, sc, s_len):
    j = pl.program_id(1)
    n_ch = (j * Q_TILE + Q_TILE + sc - 1) // sc
    n_pair = N_ATT_HEADS // 2
    lane = lax.broadcasted_iota(I32, (1, LANES), 1)
    qpos = j * Q_TILE + lane
    kiota = lax.broadcasted_iota(I32, (sc, LANES), 0)

    r2 = lax.broadcasted_iota(I32, (2 * Q_TILE, LANES), 0)
    c2 = lax.broadcasted_iota(I32, (2 * Q_TILE, LANES), 1)
    keep = (r2 < Q_TILE) == (c2 < HEAD_DIM)
    for p in range(n_pair):
        a = qi_ref[:, p * LANES:(p + 1) * LANES].astype(F32)
        qi2_scr[p] = jnp.where(keep, jnp.concatenate([a, a], axis=0), 0.0).astype(BF16)
        a = q_ref[:, p * LANES:(p + 1) * LANES].astype(F32)
        q2_scr[p] = jnp.where(keep, jnp.concatenate([a, a], axis=0), 0.0).astype(BF16)

    w_t = jnp.transpose(kw_ref[...])[IDX_DIM:IDX_DIM + N_IDX_HEADS, :]
    w_s = (w_t * IDX_HEAD_SCALE) * IDX_SCALE

    def score_chunk(c, carry):
        off = pl.multiple_of(c * sc, sc)
        kc = ki2_ref[pl.ds(off, sc), :]
        acc = jnp.zeros((sc, LANES), F32)
        for p in range(n_pair):
            lg = _nt(kc, qi2_scr[p])
            acc = acc + jnp.maximum(lg[:, :LANES], 0.0) * w_s[2 * p:2 * p + 1, :]
            acc = acc + jnp.maximum(lg[:, LANES:], 0.0) * w_s[2 * p + 1:2 * p + 2, :]
        key = jnp.where(off + kiota <= qpos, _to_key(acc), INT_MIN)
        key_scr[pl.ds(off, sc), :] = key
        return carry

    lax.fori_loop(0, n_ch, score_chunk, 0)

    def count(pred):
        def body(c, acc):
            off = pl.multiple_of(c * sc, sc)
            hit = jnp.where(pred(key_scr[pl.ds(off, sc), :], off + kiota), 1.0, 0.0)
            return acc + hit.reshape(sc // 8, 8, LANES).sum(axis=0)
        acc = lax.fori_loop(0, n_ch, body, jnp.zeros((8, LANES), F32))
        return acc.sum(axis=0, keepdims=True)

    def bisect(i, thr):
        cand = thr + lax.shift_left(jnp.int32(1), 31 - i)
        cnt = count(lambda key, _: key >= cand)
        return jnp.where(cnt >= topk, cand, thr)

    thr = lax.fori_loop(0, 32, bisect, jnp.full((1, LANES), INT_MIN, I32))
    c_gt = count(lambda key, _: key > thr)
    c_ge = count(lambda key, _: key >= thr)
    need = topk - c_gt
    tie = (c_ge - c_gt) > need
    cut_scr[...] = jnp.full((1, LANES), s_len, I32)

    @pl.when(jnp.max(jnp.where(tie, 1.0, 0.0)) > 0.0)
    def _():
        nbits = max(1, (s_len - 1).bit_length())

        def bisect_pos(i, lo):
            cand = lo + lax.shift_left(jnp.int32(1), nbits - 1 - i)
            cnt = count(lambda key, kpos: jnp.where(key == thr, kpos, s_len) < cand)
            return jnp.where(cnt < need, cand, lo)

        lo = lax.fori_loop(0, nbits, bisect_pos, jnp.zeros((1, LANES), I32))
        cut_scr[...] = jnp.where(tie, lo, s_len)

    cut = cut_scr[...]
    thr_lo = jnp.maximum(thr, NEG_INF_KEY + 1)

    m_scr[...] = jnp.full(m_scr.shape, NEG_BIG, F32)
    l_scr[...] = jnp.zeros(l_scr.shape, F32)
    acc_scr[...] = jnp.zeros(acc_scr.shape, F32)

    def att_chunk(c, carry):
        off = pl.multiple_of(c * sc, sc)
        key = key_scr[pl.ds(off, sc), :]
        bias = jnp.where(key >= thr_lo, 0.0, NEG_BIG)
        bias = jnp.where(key >= POS_INF_KEY, NEG_BIG, bias)
        bias = jnp.where(key == thr, jnp.where(off + kiota > cut, NEG_BIG, bias), bias)
        bias2 = jnp.concatenate([bias, bias], axis=1)
        vt_c = vt_ref[c]
        for p in range(n_pair):
            s = _nt(k_ref[pl.ds(off, sc), p * LANES:(p + 1) * LANES], q2_scr[p]) + bias2
            m_old = m_scr[p]
            m_new = jnp.maximum(m_old, jnp.max(s, axis=0, keepdims=True))
            alpha = jnp.exp(m_old - m_new)
            e = jnp.exp(s - m_new)
            l_scr[p] = alpha * l_scr[p] + jnp.sum(e, axis=0, keepdims=True)
            pv = jnp.dot(vt_c[p * LANES:(p + 1) * LANES, :], e.astype(BF16), preferred_element_type=F32)
            acc_scr[p] = acc_scr[p] * alpha + pv
            m_scr[p] = m_new
        return carry

    lax.fori_loop(0, n_ch, att_chunk, 0)

    outs = []
    for p in range(n_pair):
        a = acc_scr[p]
        l = l_scr[p]
        outs.append(a[:HEAD_DIM, :LANES] / l[:, :LANES])
        outs.append(a[HEAD_DIM:, LANES:] / l[:, LANES:])
    o_ref[...] = jnp.transpose(jnp.concatenate(outs, axis=0))


def _dsa_prompt(q, qi, kw, kb, vt, ki2, n_batch, s_len):
    t = q.shape[0]
    sc = min(KEY_CHUNK, s_len)
    nq = s_len // Q_TILE
    topk = min(TOPK_MAX, s_len // 4)
    qrow = lambda w_: pl.BlockSpec((Q_TILE, w_), lambda b, j: (b * nq + j, 0))
    per_batch = lambda w_: pl.BlockSpec((s_len, w_), lambda b, j: (b, 0), pipeline_mode=pl.Buffered(1))
    return pl.pallas_call(
        functools.partial(_dsa_body, topk=topk, sc=sc, s_len=s_len),
        grid=(n_batch, nq),
        in_specs=[qrow(ATT_W), qrow(IDXQ_W), qrow(KW_W), per_batch(ATT_W),
                  pl.BlockSpec((None, s_len // sc, ATT_W, sc), lambda b, j: (b, 0, 0, 0),
                               pipeline_mode=pl.Buffered(1)),
                  per_batch(2 * IDX_DIM)],
        out_specs=qrow(ATT_W),
        out_shape=jax.ShapeDtypeStruct((t, ATT_W), F32),
        scratch_shapes=[pltpu.VMEM((s_len, LANES), I32),
                        pltpu.VMEM((N_ATT_HEADS // 2, 2 * Q_TILE, LANES), BF16),
                        pltpu.VMEM((N_ATT_HEADS // 2, 2 * Q_TILE, LANES), BF16),
                        pltpu.VMEM((1, LANES), I32),
                        pltpu.VMEM((N_ATT_HEADS // 2, 1, 2 * Q_TILE), F32),
                        pltpu.VMEM((N_ATT_HEADS // 2, 1, 2 * Q_TILE), F32),
                        pltpu.VMEM((N_ATT_HEADS // 2, LANES, 2 * Q_TILE), F32)],
        compiler_params=_cparams(("parallel", "arbitrary")),
        name="dsa_prompt",
    )(q, qi, kw, kb, vt, ki2)


def _ret_lane_const(vals):
    lane = lax.broadcasted_iota(I32, (1, RET_W), 1)
    out = jnp.zeros((1, RET_W), F32)
    for h, v in enumerate(vals):
        out = jnp.where(lane // HEAD_DIM == h, v, out)
    return out


def _ret_body(rq_ref, rk_ref, rv_ref, o_ref, st_ref, sbd_scr, *, ch):
    c = pl.program_id(1)

    @pl.when(c == 0)
    def _():
        sbd_scr[...] = jnp.zeros(sbd_scr.shape, F32)

    q = rq_ref[...]
    k = rk_ref[...]
    vb = rv_ref[...].astype(BF16)
    qb = q.astype(BF16)
    kb = k.astype(BF16)
    head = lax.broadcasted_iota(I32, (1, RET_W), 1) // HEAD_DIM
    log_g = _ret_lane_const(LOG_G)
    i_col = lax.broadcasted_iota(I32, (ch, 1), 0).astype(F32)
    state = sbd_scr[...]

    cross = jnp.dot(qb, state.astype(BF16), preferred_element_type=F32) * jnp.exp(log_g * (i_col + 1.0))
    ii = lax.broadcasted_iota(I32, (ch, ch), 0)
    jj = lax.broadcasted_iota(I32, (ch, ch), 1)
    causal = ii >= jj
    diff = jnp.where(causal, ii - jj, 0).astype(F32)
    inner = jnp.zeros((ch, RET_W), F32)
    for h in range(N_RET_HEADS):
        qm = jnp.where(head == h, q, 0.0).astype(BF16)
        decay = jnp.where(causal, jnp.exp(LOG_G[h] * diff), 0.0)
        sc = (_nt(qm, kb) * decay).astype(BF16)
        inner = inner + jnp.where(head == h, jnp.dot(sc, vb, preferred_element_type=F32), 0.0)
    o_ref[...] = inner + cross

    kd = (k * jnp.exp(log_g * (ch - 1.0 - i_col))).astype(BF16)
    kv = lax.dot_general(kd, vb, (((0,), (0,)), ((), ())), preferred_element_type=F32)
    rh = lax.broadcasted_iota(I32, (RET_W, RET_W), 0) // HEAD_DIM
    ch_ = lax.broadcasted_iota(I32, (RET_W, RET_W), 1) // HEAD_DIM
    new_state = state * jnp.exp(log_g * float(ch)) + jnp.where(rh == ch_, kv, 0.0)
    sbd_scr[...] = new_state

    @pl.when(c == pl.num_programs(1) - 1)
    def _():
        for h in range(N_RET_HEADS):
            st_ref[h] = new_state[h * HEAD_DIM:(h + 1) * HEAD_DIM, h * HEAD_DIM:(h + 1) * HEAD_DIM]


def _ret_prompt(rq, rk, rv, n_batch, s_len):
    t = rq.shape[0]
    ch = min(RET_CHUNK, s_len)
    nc = s_len // ch
    blk = pl.BlockSpec((ch, RET_W), lambda b, c: (b * nc + c, 0))
    return pl.pallas_call(
        functools.partial(_ret_body, ch=ch),
        grid=(n_batch, nc),
        in_specs=[blk, blk, blk],
        out_specs=[blk, pl.BlockSpec((None, N_RET_HEADS, HEAD_DIM, HEAD_DIM), lambda b, c: (b, 0, 0, 0))],
        out_shape=[jax.ShapeDtypeStruct((t, RET_W), F32),
                   jax.ShapeDtypeStruct((n_batch, N_RET_HEADS, HEAD_DIM, HEAD_DIM), F32)],
        scratch_shapes=[pltpu.VMEM((RET_W, RET_W), F32)],
        compiler_params=_cparams(("parallel", "arbitrary")),
        name="ret_prompt",
    )(rq, rk, rv)


def _softmax_lanes(s):
    m = jnp.max(s, axis=-1, keepdims=True)
    e = jnp.exp(s - m)
    return e / jnp.sum(e, axis=-1, keepdims=True)


def _mematt_body(mq_ref, mk_ref, mv_ref, o_ref):
    mq = mq_ref[...].astype(F32)
    mk = mk_ref[...].astype(BF16)
    mv = mv_ref[...].astype(BF16)
    head = lax.broadcasted_iota(I32, (1, MEM_W), 1) // HEAD_DIM
    out = jnp.zeros(mq.shape, F32)
    for h in range(N_MEM_HEADS):
        qm = jnp.where(head == h, mq, 0.0).astype(BF16)
        p = _softmax_lanes(_nt(qm, mk)).astype(BF16)
        out = out + jnp.where(head == h, jnp.dot(p, mv, preferred_element_type=F32), 0.0)
    o_ref[...] = out


def _mematt_prompt(mq, mk, mv, n_batch):
    t = mq.shape[0]
    n_mem = mk.shape[0] // n_batch
    s_len = t // n_batch
    tm = min(ROW_TILE, s_len)
    nsb = s_len // tm
    row = pl.BlockSpec((tm, MEM_W), lambda i: (i, 0))
    mem = pl.BlockSpec((n_mem, MEM_W), lambda i: (i // nsb, 0))
    return pl.pallas_call(
        _mematt_body,
        grid=(t // tm,),
        in_specs=[row, mem, mem],
        out_specs=row,
        out_shape=jax.ShapeDtypeStruct((t, MEM_W), F32),
        compiler_params=_cparams(("parallel",)),
        name="mem_attend",
    )(mq, mk, mv)


def _mematt_sample_body(mq_ref, mk_ref, mv_ref, o_ref):
    rows = 8
    mq = jnp.broadcast_to(mq_ref[...].astype(F32), (rows, MEM_W))
    sel = lax.broadcasted_iota(I32, (rows, MEM_W), 1) // HEAD_DIM == lax.broadcasted_iota(I32, (rows, MEM_W), 0)
    qbd = jnp.where(sel, mq, 0.0).astype(BF16)
    p = _softmax_lanes(_nt(qbd, mk_ref[...].astype(BF16))).astype(BF16)
    o = jnp.dot(p, mv_ref[...].astype(BF16), preferred_element_type=F32)
    o_ref[...] = jnp.sum(jnp.where(sel, o, 0.0), axis=0, keepdims=True)


def _mematt_sample(mq, mk, mv):
    n, n_mem = mk.shape[0], mk.shape[1]
    one = pl.BlockSpec((None, 1, MEM_W), lambda b: (b, 0, 0))
    mem = pl.BlockSpec((None, n_mem, MEM_W), lambda b: (b, 0, 0))
    out = pl.pallas_call(
        _mematt_sample_body,
        grid=(n,),
        in_specs=[one, mem, mem],
        out_specs=one,
        out_shape=jax.ShapeDtypeStruct((n, 1, MEM_W), F32),
        compiler_params=_cparams(("parallel",)),
        name="mem_attend_sample",
    )(mq.reshape(n, 1, MEM_W), mk, mv)
    return out.reshape(n, MEM_W)


def _mixout_body(y_ref, att_ref, ret_ref, rg_ref, mo_ref, gn_ref, bd_ref, w_ref, o_ref):
    retn = _head_rms(ret_ref[...], gn_ref[...], bd_ref[...])
    rg = rg_ref[...]
    gated = (rg * jax.nn.sigmoid(rg)) * retn
    acc = jnp.dot(att_ref[...].astype(BF16), w_ref[:ATT_W, :], preferred_element_type=F32)
    acc = acc + jnp.dot(gated.astype(BF16), w_ref[ATT_W:ATT_W + RET_W, :], preferred_element_type=F32)
    acc = acc + jnp.dot(mo_ref[...].astype(BF16), w_ref[ATT_W + RET_W:, :], preferred_element_type=F32)
    o_ref[...] = y_ref[...] + acc


def _mixout(y, att, ret, rg, mo, gn, bd, w):
    t, d = y.shape
    tm = min(ROW_TILE, t)
    row = lambda w_: pl.BlockSpec((tm, w_), lambda i: (i, 0))
    return pl.pallas_call(
        _mixout_body,
        grid=(t // tm,),
        in_specs=[row(d), row(ATT_W), row(RET_W), row(RET_W), row(MEM_W), _resident((1, RET_W)),
                  _resident(bd.shape), _resident(w.shape)],
        out_specs=row(d),
        out_shape=jax.ShapeDtypeStruct((t, d), F32),
        compiler_params=_cparams(("parallel",)),
        name="mix_out",
    )(y, att, ret, rg, mo, gn, bd, w)


def _sample_score_body(pt_ref, qi_ref, w_ref, kin_ref, *rest, pps):
    page_refs, (past_ref, self_ref) = rest[:pps], rest[pps:]
    q8 = qi_ref[...]
    w8 = (w_ref[...] * IDX_HEAD_SCALE) * IDX_SCALE

    def head_sum(lg):
        s = jnp.sum(jnp.maximum(lg, 0.0) * w8, axis=0, keepdims=True)
        return jnp.where(s == 0.0, 0.0, s)

    for i in range(pps):
        lg = _nt(q8, page_refs[i][...].astype(BF16))
        past_ref[i:i + 1, :] = head_sum(lg)

    @pl.when(pl.program_id(1) == 0)
    def _():
        kn = kin_ref[...].astype(BF16).astype(F32)
        lg = jnp.sum(q8.astype(F32) * kn, axis=1, keepdims=True)
        self_ref[...] = jnp.broadcast_to(head_sum(lg), self_ref.shape)


def _sample_scores(page_table, qi, wi, ki_new, cache_kidx):
    n, n_pages = page_table.shape
    pps = min(PAGES_PER_STEP, n_pages)
    page_specs = [pl.BlockSpec((None, PAGE_SIZE, IDX_DIM),
                               functools.partial(lambda b, g, pt, i: (pt[b, g * pps + i], 0, 0), i=i))
                  for i in range(pps)]
    grid_spec = pltpu.PrefetchScalarGridSpec(
        num_scalar_prefetch=1,
        grid=(n, n_pages // pps),
        in_specs=[pl.BlockSpec((None, N_IDX_HEADS, IDX_DIM), lambda b, g, pt: (b, 0, 0)),
                  pl.BlockSpec((None, N_IDX_HEADS, 1), lambda b, g, pt: (b, 0, 0)),
                  pl.BlockSpec((None, 1, IDX_DIM), lambda b, g, pt: (b, 0, 0))] + page_specs,
        out_specs=[pl.BlockSpec((None, pps, PAGE_SIZE), lambda b, g, pt: (b, g, 0)),
                   pl.BlockSpec((None, 1, LANES), lambda b, g, pt: (b, 0, 0))],
    )
    return pl.pallas_call(
        functools.partial(_sample_score_body, pps=pps),
        grid_spec=grid_spec,
        out_shape=[jax.ShapeDtypeStruct((n, n_pages, PAGE_SIZE), F32),
                   jax.ShapeDtypeStruct((n, 1, LANES), F32)],
        compiler_params=_cparams(("parallel", "arbitrary")),
        name="sample_scores",
    )(page_table, qi.reshape(n, N_IDX_HEADS, IDX_DIM), wi.reshape(n, N_IDX_HEADS, 1),
      ki_new.reshape(n, 1, IDX_DIM), *([cache_kidx] * pps))


def _sample_select_body(past_ref, self_ref, thr_ref, cut_ref, *, topk, past_len):
    keys = _to_key(past_ref[...])
    kself = _to_key(self_ref[...][:, 0:1])
    n = keys.shape[0]
    kpos = lax.broadcasted_iota(I32, keys.shape, 1)

    def count(pred):
        c = jnp.sum(jnp.where(pred(keys, kpos), 1.0, 0.0), axis=1, keepdims=True)
        return c + jnp.where(pred(kself, past_len), 1.0, 0.0)

    def bisect(i, thr):
        cand = thr + lax.shift_left(jnp.int32(1), 31 - i)
        return jnp.where(count(lambda key, _: key >= cand) >= topk, cand, thr)

    thr = lax.fori_loop(0, 32, bisect, jnp.full((n, 1), INT_MIN, I32))
    c_gt = count(lambda key, _: key > thr)
    c_ge = count(lambda key, _: key >= thr)
    need = topk - c_gt
    tie = (c_ge - c_gt) > need
    thr_ref[...] = jnp.broadcast_to(thr, thr_ref.shape)
    cut_ref[...] = jnp.full(cut_ref.shape, past_len + 1, I32)

    @pl.when(jnp.max(jnp.where(tie, 1.0, 0.0)) > 0.0)
    def _():
        nbits = max(1, past_len.bit_length())

        def bisect_pos(i, lo):
            cand = lo + lax.shift_left(jnp.int32(1), nbits - 1 - i)
            cnt = count(lambda key, pos: jnp.where(key == thr, pos, past_len + 1) < cand)
            return jnp.where(cnt < need, cand, lo)

        lo = lax.fori_loop(0, nbits, bisect_pos, jnp.zeros((n, 1), I32))
        cut_ref[...] = jnp.broadcast_to(jnp.where(tie, lo, past_len + 1), cut_ref.shape)


def _sample_select(sc_past, sc_self, topk):
    n, past_len = sc_past.shape
    full = lambda shape: pl.BlockSpec(shape, lambda i: (0,) * len(shape))
    return pl.pallas_call(
        functools.partial(_sample_select_body, topk=topk, past_len=past_len),
        grid=(1,),
        in_specs=[full((n, past_len)), full((n, LANES))],
        out_specs=[full((n, LANES)), full((n, LANES))],
        out_shape=[jax.ShapeDtypeStruct((n, LANES), I32)] * 2,
        compiler_params=_cparams(("arbitrary",)),
        name="sample_select",
    )(sc_past, sc_self)


def _select_bias(key, kpos, thr, cut):
    bias = jnp.where(key >= jnp.maximum(thr, NEG_INF_KEY + 1), 0.0, NEG_BIG)
    bias = jnp.where(key >= POS_INF_KEY, NEG_BIG, bias)
    return jnp.where(key == thr, jnp.where(kpos > cut, NEG_BIG, bias), bias)


def _sample_att_body(pt_ref, thr_ref, cut_ref, q_ref, kn_ref, vn_ref, past_ref, self_ref, *rest, pps, past_len):
    k_refs, v_refs = rest[:pps], rest[pps:2 * pps]
    o_ref, m_scr, l_scr, acc_scr = rest[2 * pps:]
    b = pl.program_id(0)
    g = pl.program_id(1)
    rows = N_ATT_HEADS
    thr = thr_ref[b]
    cut = cut_ref[b]

    @pl.when(g == 0)
    def _():
        m_scr[...] = jnp.full(m_scr.shape, NEG_BIG, F32)
        l_scr[...] = jnp.zeros(l_scr.shape, F32)
        acc_scr[...] = jnp.zeros(acc_scr.shape, F32)

    q = jnp.broadcast_to(q_ref[...].astype(F32), (rows, ATT_W))
    diag = lax.broadcasted_iota(I32, (rows, ATT_W), 1) // HEAD_DIM == lax.broadcasted_iota(I32, (rows, ATT_W), 0)
    qbd_f = jnp.where(diag, q, 0.0)
    qbd = qbd_f.astype(BF16)
    lane = lax.broadcasted_iota(I32, (1, PAGE_SIZE), 1)

    parts = []
    for i in range(pps):
        kpos = (g * pps + i) * PAGE_SIZE + lane
        bias = _select_bias(_to_key(past_ref[i:i + 1, :]), kpos, thr, cut)
        parts.append(_nt(qbd, k_refs[i][...].astype(BF16)) + bias)
    s = jnp.concatenate(parts, axis=1)
    m_old = m_scr[...]
    m_new = jnp.maximum(m_old, jnp.max(s, axis=1, keepdims=True))
    alpha = jnp.exp(m_old - m_new)
    e = jnp.exp(s - m_new)
    l_new = alpha * l_scr[...] + jnp.sum(e, axis=1, keepdims=True)
    eb = e.astype(BF16)
    acc = acc_scr[...] * alpha
    for i in range(pps):
        acc = acc + jnp.dot(eb[:, i * PAGE_SIZE:(i + 1) * PAGE_SIZE], v_refs[i][...].astype(BF16),
                            preferred_element_type=F32)
    m_scr[...] = m_new
    l_scr[...] = l_new
    acc_scr[...] = acc

    @pl.when(g == pl.num_programs(1) - 1)
    def _():
        kn = kn_ref[...].astype(BF16).astype(F32)
        vn = vn_ref[...].astype(BF16).astype(F32)
        s_self = jnp.sum(qbd_f * kn, axis=1, keepdims=True)
        bias = _select_bias(_to_key(self_ref[...][:, 0:1]), past_len, thr, cut)
        s_self = s_self + bias
        m_fin = jnp.maximum(m_new, s_self)
        a2 = jnp.exp(m_new - m_fin)
        e_self = jnp.exp(s_self - m_fin)
        l_fin = a2 * l_new + e_self
        acc_fin = acc * a2 + e_self.astype(BF16).astype(F32) * vn
        o_ref[...] = jnp.sum(jnp.where(diag, acc_fin / l_fin, 0.0), axis=0, keepdims=True)


def _sample_attend(page_table, thr, cut, q, k_new, v_new, sc_past, sc_self, cache_k, cache_v):
    n, n_pages = page_table.shape
    pps = min(PAGES_PER_STEP, n_pages)
    past_len = n_pages * PAGE_SIZE
    page_specs = [pl.BlockSpec((None, PAGE_SIZE, ATT_W),
                               functools.partial(lambda b, g, pt, th, cu, i: (pt[b, g * pps + i], 0, 0), i=i))
                  for i in range(pps)]
    one = lambda w_: pl.BlockSpec((None, 1, w_), lambda b, g, pt, th, cu: (b, 0, 0))
    grid_spec = pltpu.PrefetchScalarGridSpec(
        num_scalar_prefetch=3,
        grid=(n, n_pages // pps),
        in_specs=[one(ATT_W), one(ATT_W), one(ATT_W),
                  pl.BlockSpec((None, pps, PAGE_SIZE), lambda b, g, pt, th, cu: (b, g, 0)),
                  one(LANES)] + page_specs + page_specs,
        out_specs=one(ATT_W),
        scratch_shapes=[pltpu.VMEM((N_ATT_HEADS, 1), F32), pltpu.VMEM((N_ATT_HEADS, 1), F32),
                        pltpu.VMEM((N_ATT_HEADS, ATT_W), F32)],
    )
    out = pl.pallas_call(
        functools.partial(_sample_att_body, pps=pps, past_len=past_len),
        grid_spec=grid_spec,
        out_shape=jax.ShapeDtypeStruct((n, 1, ATT_W), F32),
        compiler_params=_cparams(("parallel", "arbitrary")),
        name="sample_attend",
    )(page_table, thr, cut, q.reshape(n, 1, ATT_W), k_new.reshape(n, 1, ATT_W), v_new.reshape(n, 1, ATT_W),
      sc_past, sc_self, *([cache_k] * pps), *([cache_v] * pps))
    return out.reshape(n, ATT_W)


def _ret_sample_body(st_ref, q_ref, k_ref, v_ref, o_ref, ns_ref):
    state = st_ref[...]
    q = q_ref[...]
    k = k_ref[...]
    v = v_ref[...]
    hh = lax.broadcasted_iota(I32, (N_RET_HEADS, 1, 1), 0)
    g = jnp.zeros((N_RET_HEADS, 1, 1), F32)
    for h in range(N_RET_HEADS):
        g = jnp.where(hh == h, math.exp(LOG_G[h]), g)
    inner = jnp.sum(q * k, axis=1, keepdims=True) * v
    cross = jnp.sum(q * state, axis=1, keepdims=True) * g
    o_ref[...] = inner + cross
    ns_ref[...] = g * state + k * v


def _ret_sample(state, rq, rk, rv):
    n = state.shape[0]
    col = pl.BlockSpec((None, N_RET_HEADS, HEAD_DIM, 1), lambda b: (b, 0, 0, 0))
    rowv = pl.BlockSpec((None, N_RET_HEADS, 1, HEAD_DIM), lambda b: (b, 0, 0, 0))
    st = pl.BlockSpec((None, N_RET_HEADS, HEAD_DIM, HEAD_DIM), lambda b: (b, 0, 0, 0))
    out, new_state = pl.pallas_call(
        _ret_sample_body,
        grid=(n,),
        in_specs=[st, col, col, rowv],
        out_specs=[rowv, st],
        out_shape=[jax.ShapeDtypeStruct((n, N_RET_HEADS, 1, HEAD_DIM), F32),
                   jax.ShapeDtypeStruct(state.shape, F32)],
        compiler_params=_cparams(("parallel",)),
        name="ret_sample",
    )(state, rq.reshape(n, N_RET_HEADS, HEAD_DIM, 1), rk.reshape(n, N_RET_HEADS, HEAD_DIM, 1),
      rv.reshape(n, N_RET_HEADS, 1, HEAD_DIM))
    return out.reshape(n, RET_W), new_state


def _rope_tables(pos):
    half = HEAD_DIM // 2
    inv = ROPE_BASE ** (-jnp.arange(half, dtype=F32) / half)
    ang = pos.astype(F32)[:, None] * inv[None, :]
    cos, sin = jnp.cos(ang), jnp.sin(ang)
    cos = jnp.tile(jnp.concatenate([cos, cos], axis=1), (1, N_RET_HEADS))
    sin = jnp.tile(jnp.concatenate([-sin, sin], axis=1), (1, N_RET_HEADS))
    return cos, sin


def _block_diag_ones(width):
    r = np.arange(width)[:, None] // HEAD_DIM
    c = np.arange(width)[None, :] // HEAD_DIM
    return jnp.asarray((r == c).astype(np.float32), dtype=BF16)


def _repack_w_in(w):
    o = np.cumsum([0, ATT_W, ATT_W, ATT_W, IDXQ_W, IDX_DIM, N_IDX_HEADS, RET_W, RET_W, RET_W, RET_W, MEM_W])
    pad = jnp.zeros((w.shape[0], KW_W - IDX_DIM - N_IDX_HEADS), w.dtype)
    return jnp.concatenate([w[:, o[0]:o[4]], w[:, o[4]:o[6]], pad, w[:, o[6]:o[11]]], axis=1).astype(BF16)


def kernel(x_prompt, x_sample, mem_prompt, cache_k, cache_v, cache_kidx, state_ret, cache_mem_k, cache_mem_v,
           page_table, ffn1_norm_g, ffn1_w_gu, ffn1_w_down, mix_norm_g, w_in, att_q_norm_g, att_k_norm_g,
           ret_gn_g, mem_norm_g, w_mem_kv, mem_q_norm_g, mem_k_norm_g, w_out, ffn2_norm_g, ffn2_w_gu, ffn2_w_down):
    n_b, s_len, d = x_prompt.shape
    n_s, t_s, _ = x_sample.shape
    assert t_s == 1, "the sample group decodes one token per sequence"
    depth = w_in.shape[0]
    n_mem = mem_prompt.shape[1]
    n_pool = cache_k.shape[1]
    past_len = page_table.shape[1] * PAGE_SIZE
    page_table = page_table.astype(I32)

    cos_p, sin_p = _rope_tables(jnp.arange(s_len, dtype=I32))
    cos_s, sin_s = _rope_tables(jnp.full((n_s,), past_len, I32))
    bd = _block_diag_ones(ATT_W)
    bd_ret = _block_diag_ones(RET_W)
    tile8 = lambda g_: jnp.tile(g_, N_ATT_HEADS).reshape(1, ATT_W)
    tile4 = lambda g_: jnp.tile(g_, N_MEM_HEADS).reshape(1, MEM_W)

    yp = x_prompt.reshape(n_b * s_len, d)
    ys = x_sample.reshape(n_s, d)
    mem = mem_prompt.reshape(n_b * n_mem, d)
    outs = [[] for _ in range(10)]
    for l in range(depth):
        g1 = ffn1_norm_g[l].reshape(1, d)
        g2 = ffn2_norm_g[l].reshape(1, d)
        gmix = mix_norm_g[l].reshape(1, d)
        w1gu, w1d = ffn1_w_gu[l].astype(BF16), ffn1_w_down[l].astype(BF16)
        w2gu, w2d = ffn2_w_gu[l].astype(BF16), ffn2_w_down[l].astype(BF16)
        w_in_l = _repack_w_in(w_in[l])
        w_out_l = w_out[l].astype(BF16)
        gq, gk, gm = tile8(att_q_norm_g[l]), tile8(att_k_norm_g[l]), tile4(mem_q_norm_g[l])
        gn = ret_gn_g[l].reshape(1, RET_W)

        yp = _ffn(yp, g1, w1gu, w1d)
        (q, k, kb, v, vt, qi, kw, ki, ki2, rq, rk, rv, rg, mq) = _proj(
            yp, gmix, w_in_l, gq, gk, gm, cos_p, sin_p, bd, n_b)
        mk, mv = _memkv(mem, mem_norm_g[l].reshape(1, d), w_mem_kv[l].astype(BF16), tile4(mem_k_norm_g[l]), bd_ret)
        att = _dsa_prompt(q, qi, kw, kb, vt, ki2, n_b, s_len)
        ret, s_fin = _ret_prompt(rq, rk, rv, n_b, s_len)
        mo = _mematt_prompt(mq, mk, mv, n_b)
        yp = _mixout(yp, att, ret, rg, mo, gn, bd_ret, w_out_l)
        yp = _ffn(yp, g2, w2gu, w2d)
        outs[0].append(k.reshape(n_b, s_len, N_ATT_HEADS, HEAD_DIM))
        outs[1].append(v.reshape(n_b, s_len, N_ATT_HEADS, HEAD_DIM))
        outs[2].append(ki.reshape(n_b, s_len, IDX_DIM))
        outs[3].append(s_fin)
        outs[4].append(mk.reshape(n_b, n_mem, N_MEM_HEADS, HEAD_DIM))
        outs[5].append(mv.reshape(n_b, n_mem, N_MEM_HEADS, HEAD_DIM))

        ys = _ffn(ys, g1, w1gu, w1d)
        (q, k, kb, v, vt, qi, kw, ki, ki2, rq, rk, rv, rg, mq) = _proj(
            ys, gmix, w_in_l, gq, gk, gm, cos_s, sin_s, bd, 1)
        wi = kw[:, IDX_DIM:IDX_DIM + N_IDX_HEADS]
        sc_past, sc_self = _sample_scores(page_table, qi, wi, ki, cache_kidx[l])
        topk = min(TOPK_MAX, (past_len + t_s) // 4)
        thr, cut = _sample_select(sc_past.reshape(n_s, past_len), sc_self.reshape(n_s, LANES), topk)
        att = _sample_attend(page_table, thr[:, 0], cut[:, 0], q, k, v, sc_past, sc_self,
                             cache_k[l].reshape(n_pool, PAGE_SIZE, ATT_W), cache_v[l].reshape(n_pool, PAGE_SIZE, ATT_W))
        ret, s_new = _ret_sample(state_ret[l].astype(F32), rq, rk, rv)
        mo = _mematt_sample(mq, cache_mem_k[l].reshape(n_s, n_mem, MEM_W), cache_mem_v[l].reshape(n_s, n_mem, MEM_W))
        ys = _mixout(ys, att, ret, rg, mo, gn, bd_ret, w_out_l)
        ys = _ffn(ys, g2, w2gu, w2d)
        outs[6].append(k.reshape(n_s, t_s, N_ATT_HEADS, HEAD_DIM))
        outs[7].append(v.reshape(n_s, t_s, N_ATT_HEADS, HEAD_DIM))
        outs[8].append(ki.reshape(n_s, t_s, IDX_DIM))
        outs[9].append(s_new)

    return (yp.reshape(n_b, s_len, d), ys.reshape(n_s, t_s, d)) + tuple(jnp.stack(o) for o in outs)
```

```python
import functools
import math

import numpy as np
import jax
import jax.numpy as jnp
from jax import lax
from jax.experimental import pallas as pl
from jax.experimental.pallas import tpu as pltpu

F32 = jnp.float32
BF16 = jnp.bfloat16
I32 = jnp.int32

HEAD_DIM = 64
N_ATT_HEADS = 8
N_IDX_HEADS = 8
IDX_DIM = 64
TOPK_MAX = 256
N_RET_HEADS = 4
N_MEM_HEADS = 4
PAGE_SIZE = 128
EPS = 1e-6
ROPE_BASE = 10000.0

ATT_W = N_ATT_HEADS * HEAD_DIM
RET_W = N_RET_HEADS * HEAD_DIM
MEM_W = N_MEM_HEADS * HEAD_DIM
IDXQ_W = N_IDX_HEADS * IDX_DIM
LANES = 128
KW_W = LANES
C_Q = 0
C_K = C_Q + ATT_W
C_V = C_K + ATT_W
C_QI = C_V + ATT_W
C_KW = C_QI + IDXQ_W
C_RQ = C_KW + KW_W
C_RK = C_RQ + RET_W
C_RV = C_RK + RET_W
C_RG = C_RV + RET_W
C_MQ = C_RG + RET_W
PROJ_W = C_MQ + MEM_W

QK_SCALE = HEAD_DIM ** -0.5
IDX_SCALE = IDX_DIM ** -0.5
IDX_HEAD_SCALE = N_IDX_HEADS ** -0.5
NEG_BIG = -1e30
M_INIT = -(2.0 ** 100)
INT_MIN = -(2 ** 31)
NEG_INF_KEY = int(np.int32(np.uint32(0x807FFFFF)))
POS_INF_KEY = 0x7F800000
LOG_G = [float(np.log1p(np.float32(-(2.0 ** (-5.0 - h))))) for h in range(N_RET_HEADS)]

VMEM_LIMIT = 56 * 1024 * 1024
ROW_TILE = 512
KEY_CHUNK = 512
Q_TILE = LANES
VT_ROWS = LANES + 16
BISECT_GROUP = 4
RET_CHUNK = 128
PAGES_PER_STEP = 16


def _cparams(sem):
    return pltpu.CompilerParams(dimension_semantics=sem, vmem_limit_bytes=VMEM_LIMIT)


def _resident(shape):
    nd = len(shape)
    return pl.BlockSpec(shape, lambda *_: (0,) * nd, pipeline_mode=pl.Buffered(1))


def _rms(x, g):
    return x * lax.rsqrt(jnp.mean(x * x, axis=-1, keepdims=True) + EPS) * g


def _head_rms(z, g, bd):
    z2 = z * z
    hi = z2.astype(BF16)
    lo = (z2 - hi.astype(F32)).astype(BF16)
    ss = jnp.dot(hi, bd, preferred_element_type=F32) + jnp.dot(lo, bd, preferred_element_type=F32)
    return z * lax.rsqrt(ss * (1.0 / HEAD_DIM) + EPS) * g


def _nt(a, b):
    return lax.dot_general(a, b, (((1,), (1,)), ((), ())), preferred_element_type=F32)


def _to_key(x):
    b = lax.bitcast_convert_type(x, I32)
    return jnp.where(b < 0, b ^ jnp.int32(0x7FFFFFFF), b)


def _ffn_body(x_ref, g_ref, wgu_ref, wd_ref, o_ref, *, d_ff, fc):
    x = x_ref[...]
    h = _rms(x, g_ref[...]).astype(BF16)
    acc = jnp.zeros(x.shape, F32)
    for c in range(d_ff // fc):
        gate = jnp.dot(h, wgu_ref[:, c * fc:(c + 1) * fc], preferred_element_type=F32)
        up = jnp.dot(h, wgu_ref[:, d_ff + c * fc:d_ff + (c + 1) * fc], preferred_element_type=F32)
        act = (gate * jax.nn.sigmoid(gate) * up).astype(BF16)
        acc = acc + jnp.dot(act, wd_ref[c * fc:(c + 1) * fc, :], preferred_element_type=F32)
    o_ref[...] = x + 0.5 * acc


def _ffn(x, g, wgu, wd):
    t, d = x.shape
    d_ff = wd.shape[0]
    tm = min(ROW_TILE, t)
    fc = d_ff // 2 if (d_ff // 2) % LANES == 0 else d_ff
    return pl.pallas_call(
        functools.partial(_ffn_body, d_ff=d_ff, fc=fc),
        grid=(t // tm,),
        in_specs=[pl.BlockSpec((tm, d), lambda i: (i, 0)), _resident((1, d)),
                  _resident(wgu.shape), _resident(wd.shape)],
        out_specs=pl.BlockSpec((tm, d), lambda i: (i, 0)),
        out_shape=jax.ShapeDtypeStruct((t, d), F32),
        compiler_params=_cparams(("parallel",)),
        name="ffn_half",
    )(x, g, wgu, wd)


def _proj_body(x_ref, g_ref, w_ref, gq_ref, gk_ref, gm_ref, cos_ref, sin_ref, bd_ref,
               q_ref, k_ref, kb_ref, v_ref, vt_ref, qi_ref, kw_ref, ki_ref, ki2_ref,
               rq_ref, rk_ref, rv_ref, rg_ref, mq_ref):
    h = _rms(x_ref[...], g_ref[...]).astype(BF16)

    def proj(a, width):
        return jnp.dot(h, w_ref[:, a:a + width], preferred_element_type=F32)

    bd = bd_ref[...]
    q = _head_rms(proj(C_Q, ATT_W), gq_ref[...], bd)
    q_ref[...] = (q * QK_SCALE).astype(BF16)
    k = _head_rms(proj(C_K, ATT_W), gk_ref[...], bd)
    k_ref[...] = k
    kb_ref[...] = k.astype(BF16)
    v = proj(C_V, ATT_W)
    v_ref[...] = v
    vt = jnp.transpose(v)
    tail_row = lax.broadcasted_iota(I32, (VT_ROWS - LANES, vt.shape[1]), 0)
    tail = jnp.where(tail_row == 0, 1.0, 0.0)
    for p in range(N_ATT_HEADS // 2):
        vt_ref[p] = jnp.concatenate([vt[p * LANES:(p + 1) * LANES, :], tail], axis=0).astype(BF16)
    qi_ref[...] = proj(C_QI, IDXQ_W).astype(BF16)
    kw = proj(C_KW, KW_W)
    kw_ref[...] = kw
    ki = kw[:, :IDX_DIM]
    ki_ref[...] = ki
    kib = ki.astype(BF16)
    ki2_ref[...] = jnp.concatenate([kib, kib], axis=1)

    lane = lax.broadcasted_iota(I32, (1, RET_W), 1)
    first_half = (lane % HEAD_DIM) < (HEAD_DIM // 2)
    cos = cos_ref[...]
    sin = sin_ref[...]

    def rot(x):
        swapped = jnp.where(first_half, pltpu.roll(x, RET_W - HEAD_DIM // 2, axis=1),
                            pltpu.roll(x, HEAD_DIM // 2, axis=1))
        return x * cos + swapped * sin

    rq_ref[...] = rot(proj(C_RQ, RET_W))
    rk_ref[...] = rot(proj(C_RK, RET_W)) * QK_SCALE
    rv_ref[...] = proj(C_RV, RET_W)
    rg_ref[...] = proj(C_RG, RET_W)
    mq = _head_rms(proj(C_MQ, MEM_W), gm_ref[...], bd[:MEM_W, :MEM_W])
    mq_ref[...] = (mq * QK_SCALE).astype(BF16)


def _proj(y, g, w, gq, gk, gm, cos, sin, bd, n_batch):
    t, d = y.shape
    s_len = t // n_batch
    tm = min(ROW_TILE, s_len)
    nsb = s_len // tm
    row = lambda w_: pl.BlockSpec((tm, w_), lambda i: (i, 0))
    tab = pl.BlockSpec((tm, RET_W), lambda i: (i % nsb, 0))
    out_shapes = [
        (jax.ShapeDtypeStruct((t, ATT_W), BF16), row(ATT_W)),
        (jax.ShapeDtypeStruct((t, ATT_W), F32), row(ATT_W)),
        (jax.ShapeDtypeStruct((t, ATT_W), BF16), row(ATT_W)),
        (jax.ShapeDtypeStruct((t, ATT_W), F32), row(ATT_W)),
        (jax.ShapeDtypeStruct((n_batch, nsb, N_ATT_HEADS // 2, VT_ROWS, tm), BF16),
         pl.BlockSpec((None, None, N_ATT_HEADS // 2, VT_ROWS, tm),
                      lambda i: (i // nsb, i % nsb, 0, 0, 0))),
        (jax.ShapeDtypeStruct((t, IDXQ_W), BF16), row(IDXQ_W)),
        (jax.ShapeDtypeStruct((t, KW_W), F32), row(KW_W)),
        (jax.ShapeDtypeStruct((t, IDX_DIM), F32), row(IDX_DIM)),
        (jax.ShapeDtypeStruct((t, 2 * IDX_DIM), BF16), row(2 * IDX_DIM)),
        (jax.ShapeDtypeStruct((t, RET_W), F32), row(RET_W)),
        (jax.ShapeDtypeStruct((t, RET_W), F32), row(RET_W)),
        (jax.ShapeDtypeStruct((t, RET_W), F32), row(RET_W)),
        (jax.ShapeDtypeStruct((t, RET_W), F32), row(RET_W)),
        (jax.ShapeDtypeStruct((t, MEM_W), BF16), row(MEM_W)),
    ]
    return pl.pallas_call(
        _proj_body,
        grid=(t // tm,),
        in_specs=[row(d), _resident((1, d)), _resident(w.shape), _resident((1, ATT_W)),
                  _resident((1, ATT_W)), _resident((1, MEM_W)), tab, tab, _resident(bd.shape)],
        out_specs=[s for _, s in out_shapes],
        out_shape=[s for s, _ in out_shapes],
        compiler_params=_cparams(("parallel",)),
        name="mix_proj",
    )(y, g, w, gq, gk, gm, cos, sin, bd)


def _memkv_body(x_ref, g_ref, w_ref, gk_ref, bd_ref, mk_ref, mv_ref):
    h = _rms(x_ref[...], g_ref[...]).astype(BF16)
    z = jnp.dot(h, w_ref[...], preferred_element_type=F32)
    mk_ref[...] = _head_rms(z[:, :MEM_W], gk_ref[...], bd_ref[...])
    mv_ref[...] = z[:, MEM_W:]


def _memkv(mem, g, w, gk, bd):
    t, d = mem.shape
    tm = min(ROW_TILE, t)
    row = lambda w_: pl.BlockSpec((tm, w_), lambda i: (i, 0))
    return pl.pallas_call(
        _memkv_body,
        grid=(t // tm,),
        in_specs=[row(d), _resident((1, d)), _resident(w.shape), _resident((1, MEM_W)), _resident(bd.shape)],
        out_specs=[row(MEM_W), row(MEM_W)],
        out_shape=[jax.ShapeDtypeStruct((t, MEM_W), F32)] * 2,
        compiler_params=_cparams(("parallel",)),
        name="mem_kv",
    )(mem, g, w, gk, bd)


def _dsa_body(q_ref, qi_ref, kw_ref, k_ref, vt_ref, ki2_ref, o_ref,
              key_scr, qi2_scr, q2_scr, cut_scr, m_scr, acc_scr, sa_scr, sb_scr, *, topk, sc, s_len):
    j = pl.program_id(1)
    n_ch = (j * Q_TILE + Q_TILE + sc - 1) // sc
    n_pair = N_ATT_HEADS // 2
    lane = lax.broadcasted_iota(I32, (1, LANES), 1)
    qpos = j * Q_TILE + lane
    kiota = lax.broadcasted_iota(I32, (sc, LANES), 0)

    r2 = lax.broadcasted_iota(I32, (2 * Q_TILE, LANES), 0)
    c2 = lax.broadcasted_iota(I32, (2 * Q_TILE, LANES), 1)
    keep = (r2 < Q_TILE) == (c2 < HEAD_DIM)
    eye2 = jnp.where(r2 % Q_TILE == c2, 1.0, 0.0).astype(BF16)
    for p in range(n_pair):
        a = qi_ref[:, p * LANES:(p + 1) * LANES].astype(F32)
        qi2_scr[p] = jnp.where(keep, jnp.concatenate([a, a], axis=0), 0.0).astype(BF16)
        a = q_ref[:, p * LANES:(p + 1) * LANES].astype(F32)
        q2 = jnp.where(keep, jnp.concatenate([a, a], axis=0), 0.0).astype(BF16)
        q2_scr[p] = jnp.concatenate([q2, eye2], axis=1)

    w_t = jnp.transpose(kw_ref[...])[IDX_DIM:IDX_DIM + N_IDX_HEADS, :]
    w_s = (w_t * IDX_HEAD_SCALE) * IDX_SCALE

    def score_chunk(c, carry):
        off = pl.multiple_of(c * sc, sc)
        kc = ki2_ref[pl.ds(off, sc), :]
        acc = jnp.zeros((sc, LANES), F32)
        for p in range(n_pair):
            lg = _nt(kc, qi2_scr[p])
            acc = acc + jnp.maximum(lg[:, :LANES], 0.0) * w_s[2 * p:2 * p + 1, :]
            acc = acc + jnp.maximum(lg[:, LANES:], 0.0) * w_s[2 * p + 1:2 * p + 2, :]
        key = jnp.where(off + kiota <= qpos, _to_key(acc), INT_MIN)
        key_scr[pl.ds(off, sc), :] = key
        return carry

    lax.fori_loop(0, n_ch, score_chunk, 0)

    def count(pred):
        def body(c, acc):
            off = pl.multiple_of(c * sc, sc)
            for g in range(sc // 64):
                rows = pl.ds(off + g * 64, 64)
                acc = jnp.where(pred(key_scr[rows, :], off + kiota[g * 64:(g + 1) * 64]), acc + 1.0, acc)
            return acc
        acc = lax.fori_loop(0, n_ch, body, jnp.zeros((64, LANES), F32))
        return acc.reshape(8, 8, LANES).sum(axis=0).sum(axis=0, keepdims=True)

    def bisect_group(state):
        i, thr, active, _ = state
        for _ in range(BISECT_GROUP):
            cand = thr + lax.shift_left(jnp.int32(1), 31 - i)
            cnt = count(lambda key, _: key >= cand)
            thr = jnp.where((cnt >= topk) & (active > 0.0), cand, thr)
            active = jnp.where(cnt == topk, 0.0, active)
            i = i + 1
        return i, thr, active, jnp.max(active)

    active0 = jnp.where(qpos + 1 <= topk, 0.0, 1.0)
    _, thr, active, n_active = lax.while_loop(
        lambda st: (st[0] < 32) & (st[3] > 0.0), bisect_group,
        (jnp.int32(0), jnp.full((1, LANES), INT_MIN, I32), active0, jnp.max(active0)))
    cut_scr[...] = jnp.full((1, LANES), s_len, I32)

    @pl.when(n_active > 0.0)
    def _():
        c_gt = count(lambda key, _: key > thr)
        c_ge = count(lambda key, _: key >= thr)
        need = topk - c_gt
        tie = ((c_ge - c_gt) > need) & (active > 0.0)
        nbits = max(1, (s_len - 1).bit_length())

        def bisect_pos(i, lo):
            cand = lo + lax.shift_left(jnp.int32(1), nbits - 1 - i)
            cnt = count(lambda key, kpos: jnp.where(key == thr, kpos, s_len) < cand)
            return jnp.where(cnt < need, cand, lo)

        lo = lax.fori_loop(0, nbits, bisect_pos, jnp.zeros((1, LANES), I32))
        cut_scr[...] = jnp.where(tie, lo, s_len)

    cut = cut_scr[...]
    thr_lo = jnp.maximum(thr, NEG_INF_KEY + 1)

    m_scr[...] = jnp.full(m_scr.shape, M_INIT, F32)
    acc_scr[...] = jnp.zeros(acc_scr.shape, F32)

    def qk_chunk(c, s_scr):
        off = pl.multiple_of(c * sc, sc)
        key = key_scr[pl.ds(off, sc), :]
        bias = jnp.where(key >= thr_lo, 0.0, NEG_BIG)
        bias = jnp.where(key >= POS_INF_KEY, NEG_BIG, bias)
        bias = jnp.where(key == thr, jnp.where(off + kiota > cut, NEG_BIG, bias), bias).astype(BF16)
        for p in range(n_pair):
            kc = jnp.concatenate([k_ref[pl.ds(off, sc), p * LANES:(p + 1) * LANES], bias], axis=1)
            s_scr[p] = _nt(kc, q2_scr[p])

    def softmax_pv_chunk(c, s_scr):
        es, alphas = [], []
        for p in range(n_pair):
            s = s_scr[p].astype(BF16)
            m_c = s.reshape(sc // 64, 64, 2 * Q_TILE).max(axis=0)
            m_c = m_c.reshape(4, 16, 2 * Q_TILE).max(axis=0).astype(F32).max(axis=0, keepdims=True)
            m_old = m_scr[p]
            m_new = jnp.maximum(m_old, m_c)
            alphas.append(jnp.exp(m_old - m_new))
            es.append(jnp.exp(s - m_new.astype(BF16)))
            m_scr[p] = m_new
        for p in range(n_pair):
            pv = jnp.dot(vt_ref[c, p], es[p], preferred_element_type=F32)
            acc_scr[p] = acc_scr[p] * alphas[p] + pv

    def att_two_chunks(i, carry):
        c = 2 * i
        qk_chunk(c + 1, sb_scr)
        softmax_pv_chunk(c, sa_scr)
        qk_chunk(c + 2, sa_scr)
        softmax_pv_chunk(c + 1, sb_scr)
        return carry

    qk_chunk(0, sa_scr)
    n_two = (n_ch - 1) // 2
    lax.fori_loop(0, n_two, att_two_chunks, 0)

    @pl.when(n_ch % 2 == 0)
    def _():
        qk_chunk(n_ch - 1, sb_scr)
        softmax_pv_chunk(n_ch - 2, sa_scr)
        softmax_pv_chunk(n_ch - 1, sb_scr)

    @pl.when(n_ch % 2 == 1)
    def _():
        softmax_pv_chunk(n_ch - 1, sa_scr)

    outs = []
    for p in range(n_pair):
        a = acc_scr[p]
        l = a[LANES:LANES + 1, :]
        outs.append(a[:HEAD_DIM, :LANES] / l[:, :LANES])
        outs.append(a[HEAD_DIM:LANES, LANES:] / l[:, LANES:])
    o_ref[...] = jnp.transpose(jnp.concatenate(outs, axis=0))


def _dsa_prompt(q, qi, kw, kb, vt, ki2, n_batch, s_len):
    t = q.shape[0]
    sc = min(KEY_CHUNK, s_len)
    nq = s_len // Q_TILE
    topk = min(TOPK_MAX, s_len // 4)
    qrow = lambda w_: pl.BlockSpec((Q_TILE, w_), lambda b, j: (b * nq + j, 0))
    per_batch = lambda w_: pl.BlockSpec((s_len, w_), lambda b, j: (b, 0), pipeline_mode=pl.Buffered(1))
    return pl.pallas_call(
        functools.partial(_dsa_body, topk=topk, sc=sc, s_len=s_len),
        grid=(n_batch, nq),
        in_specs=[qrow(ATT_W), qrow(IDXQ_W), qrow(KW_W), per_batch(ATT_W),
                  pl.BlockSpec((None, s_len // sc, N_ATT_HEADS // 2, VT_ROWS, sc), lambda b, j: (b, 0, 0, 0, 0),
                               pipeline_mode=pl.Buffered(1)),
                  per_batch(2 * IDX_DIM)],
        out_specs=qrow(ATT_W),
        out_shape=jax.ShapeDtypeStruct((t, ATT_W), F32),
        scratch_shapes=[pltpu.VMEM((s_len, LANES), I32),
                        pltpu.VMEM((N_ATT_HEADS // 2, 2 * Q_TILE, LANES), BF16),
                        pltpu.VMEM((N_ATT_HEADS // 2, 2 * Q_TILE, 2 * LANES), BF16),
                        pltpu.VMEM((1, LANES), I32),
                        pltpu.VMEM((N_ATT_HEADS // 2, 1, 2 * Q_TILE), F32),
                        pltpu.VMEM((N_ATT_HEADS // 2, VT_ROWS, 2 * Q_TILE), F32),
                        pltpu.VMEM((N_ATT_HEADS // 2, sc, 2 * Q_TILE), F32),
                        pltpu.VMEM((N_ATT_HEADS // 2, sc, 2 * Q_TILE), F32)],
        compiler_params=_cparams(("parallel", "arbitrary")),
        name="dsa_prompt",
    )(q, qi, kw, kb, vt, ki2)


def _ret_lane_const(vals):
    lane = lax.broadcasted_iota(I32, (1, RET_W), 1)
    out = jnp.zeros((1, RET_W), F32)
    for h, v in enumerate(vals):
        out = jnp.where(lane // HEAD_DIM == h, v, out)
    return out


def _ret_body(rq_ref, rk_ref, rv_ref, o_ref, st_ref, sbd_scr, *, ch):
    c = pl.program_id(1)

    @pl.when(c == 0)
    def _():
        sbd_scr[...] = jnp.zeros(sbd_scr.shape, F32)

    q = rq_ref[...]
    k = rk_ref[...]
    vb = rv_ref[...].astype(BF16)
    qb = q.astype(BF16)
    kb = k.astype(BF16)
    head = lax.broadcasted_iota(I32, (1, RET_W), 1) // HEAD_DIM
    log_g = _ret_lane_const(LOG_G)
    i_col = lax.broadcasted_iota(I32, (ch, 1), 0).astype(F32)
    state = sbd_scr[...]

    cross = jnp.dot(qb, state.astype(BF16), preferred_element_type=F32) * jnp.exp(log_g * (i_col + 1.0))
    ii = lax.broadcasted_iota(I32, (ch, ch), 0)
    jj = lax.broadcasted_iota(I32, (ch, ch), 1)
    causal = ii >= jj
    diff = jnp.where(causal, ii - jj, 0).astype(F32)
    inner = jnp.zeros((ch, RET_W), F32)
    for h in range(N_RET_HEADS):
        qm = jnp.where(head == h, q, 0.0).astype(BF16)
        decay = jnp.where(causal, jnp.exp(LOG_G[h] * diff), 0.0)
        sc = (_nt(qm, kb) * decay).astype(BF16)
        inner = inner + jnp.where(head == h, jnp.dot(sc, vb, preferred_element_type=F32), 0.0)
    o_ref[...] = inner + cross

    kd = (k * jnp.exp(log_g * (ch - 1.0 - i_col))).astype(BF16)
    kv = lax.dot_general(kd, vb, (((0,), (0,)), ((), ())), preferred_element_type=F32)
    rh = lax.broadcasted_iota(I32, (RET_W, RET_W), 0) // HEAD_DIM
    ch_ = lax.broadcasted_iota(I32, (RET_W, RET_W), 1) // HEAD_DIM
    new_state = state * jnp.exp(log_g * float(ch)) + jnp.where(rh == ch_, kv, 0.0)
    sbd_scr[...] = new_state

    @pl.when(c == pl.num_programs(1) - 1)
    def _():
        for h in range(N_RET_HEADS):
            st_ref[h] = new_state[h * HEAD_DIM:(h + 1) * HEAD_DIM, h * HEAD_DIM:(h + 1) * HEAD_DIM]


def _ret_prompt(rq, rk, rv, n_batch, s_len):
    t = rq.shape[0]
    ch = min(RET_CHUNK, s_len)
    nc = s_len // ch
    blk = pl.BlockSpec((ch, RET_W), lambda b, c: (b * nc + c, 0))
    return pl.pallas_call(
        functools.partial(_ret_body, ch=ch),
        grid=(n_batch, nc),
        in_specs=[blk, blk, blk],
        out_specs=[blk, pl.BlockSpec((None, N_RET_HEADS, HEAD_DIM, HEAD_DIM), lambda b, c: (b, 0, 0, 0))],
        out_shape=[jax.ShapeDtypeStruct((t, RET_W), F32),
                   jax.ShapeDtypeStruct((n_batch, N_RET_HEADS, HEAD_DIM, HEAD_DIM), F32)],
        scratch_shapes=[pltpu.VMEM((RET_W, RET_W), F32)],
        compiler_params=_cparams(("parallel", "arbitrary")),
        name="ret_prompt",
    )(rq, rk, rv)


def _softmax_lanes(s):
    m = jnp.max(s, axis=-1, keepdims=True)
    e = jnp.exp(s - m)
    return e / jnp.sum(e, axis=-1, keepdims=True)


def _mematt_body(mq_ref, mk_ref, mv_ref, o_ref):
    mq = mq_ref[...].astype(F32)
    mk = mk_ref[...].astype(BF16)
    mv = mv_ref[...].astype(BF16)
    head = lax.broadcasted_iota(I32, (1, MEM_W), 1) // HEAD_DIM
    out = jnp.zeros(mq.shape, F32)
    for h in range(N_MEM_HEADS):
        qm = jnp.where(head == h, mq, 0.0).astype(BF16)
        p = _softmax_lanes(_nt(qm, mk)).astype(BF16)
        out = out + jnp.where(head == h, jnp.dot(p, mv, preferred_element_type=F32), 0.0)
    o_ref[...] = out


def _mematt_prompt(mq, mk, mv, n_batch):
    t = mq.shape[0]
    n_mem = mk.shape[0] // n_batch
    s_len = t // n_batch
    tm = min(ROW_TILE, s_len)
    nsb = s_len // tm
    row = pl.BlockSpec((tm, MEM_W), lambda i: (i, 0))
    mem = pl.BlockSpec((n_mem, MEM_W), lambda i: (i // nsb, 0))
    return pl.pallas_call(
        _mematt_body,
        grid=(t // tm,),
        in_specs=[row, mem, mem],
        out_specs=row,
        out_shape=jax.ShapeDtypeStruct((t, MEM_W), F32),
        compiler_params=_cparams(("parallel",)),
        name="mem_attend",
    )(mq, mk, mv)


def _mematt_sample_body(mq_ref, mk_ref, mv_ref, o_ref):
    rows = 8
    mq = jnp.broadcast_to(mq_ref[...].astype(F32), (rows, MEM_W))
    sel = lax.broadcasted_iota(I32, (rows, MEM_W), 1) // HEAD_DIM == lax.broadcasted_iota(I32, (rows, MEM_W), 0)
    qbd = jnp.where(sel, mq, 0.0).astype(BF16)
    p = _softmax_lanes(_nt(qbd, mk_ref[...].astype(BF16))).astype(BF16)
    o = jnp.dot(p, mv_ref[...].astype(BF16), preferred_element_type=F32)
    o_ref[...] = jnp.sum(jnp.where(sel, o, 0.0), axis=0, keepdims=True)


def _mematt_sample(mq, mk, mv):
    n, n_mem = mk.shape[0], mk.shape[1]
    one = pl.BlockSpec((None, 1, MEM_W), lambda b: (b, 0, 0))
    mem = pl.BlockSpec((None, n_mem, MEM_W), lambda b: (b, 0, 0))
    out = pl.pallas_call(
        _mematt_sample_body,
        grid=(n,),
        in_specs=[one, mem, mem],
        out_specs=one,
        out_shape=jax.ShapeDtypeStruct((n, 1, MEM_W), F32),
        compiler_params=_cparams(("parallel",)),
        name="mem_attend_sample",
    )(mq.reshape(n, 1, MEM_W), mk, mv)
    return out.reshape(n, MEM_W)


def _mixout_body(y_ref, att_ref, ret_ref, rg_ref, mo_ref, gn_ref, bd_ref, w_ref, o_ref):
    retn = _head_rms(ret_ref[...], gn_ref[...], bd_ref[...])
    rg = rg_ref[...]
    gated = (rg * jax.nn.sigmoid(rg)) * retn
    acc = jnp.dot(att_ref[...].astype(BF16), w_ref[:ATT_W, :], preferred_element_type=F32)
    acc = acc + jnp.dot(gated.astype(BF16), w_ref[ATT_W:ATT_W + RET_W, :], preferred_element_type=F32)
    acc = acc + jnp.dot(mo_ref[...].astype(BF16), w_ref[ATT_W + RET_W:, :], preferred_element_type=F32)
    o_ref[...] = y_ref[...] + acc


def _mixout(y, att, ret, rg, mo, gn, bd, w):
    t, d = y.shape
    tm = min(ROW_TILE, t)
    row = lambda w_: pl.BlockSpec((tm, w_), lambda i: (i, 0))
    return pl.pallas_call(
        _mixout_body,
        grid=(t // tm,),
        in_specs=[row(d), row(ATT_W), row(RET_W), row(RET_W), row(MEM_W), _resident((1, RET_W)),
                  _resident(bd.shape), _resident(w.shape)],
        out_specs=row(d),
        out_shape=jax.ShapeDtypeStruct((t, d), F32),
        compiler_params=_cparams(("parallel",)),
        name="mix_out",
    )(y, att, ret, rg, mo, gn, bd, w)


def _sample_score_body(pt_ref, qi_ref, w_ref, kin_ref, *rest, pps):
    page_refs, (past_ref, self_ref) = rest[:pps], rest[pps:]
    q8 = qi_ref[...]
    w8 = (w_ref[...] * IDX_HEAD_SCALE) * IDX_SCALE

    def head_sum(lg):
        s = jnp.sum(jnp.maximum(lg, 0.0) * w8, axis=0, keepdims=True)
        return jnp.where(s == 0.0, 0.0, s)

    for i in range(pps):
        lg = jnp.dot(q8, page_refs[i][...].astype(BF16), preferred_element_type=F32)
        past_ref[i:i + 1, :] = head_sum(lg)

    @pl.when(pl.program_id(1) == 0)
    def _():
        kn = kin_ref[...].astype(BF16).astype(F32)
        lg = jnp.sum(q8.astype(F32) * kn, axis=1, keepdims=True)
        self_ref[...] = jnp.broadcast_to(head_sum(lg), self_ref.shape)


def _sample_scores(page_table, qi, wi, ki_new, kidx_t):
    n, n_pages = page_table.shape
    pps = min(PAGES_PER_STEP, n_pages)
    page_specs = [pl.BlockSpec((None, IDX_DIM, PAGE_SIZE),
                               functools.partial(lambda b, g, pt, i: (pt[b, g * pps + i], 0, 0), i=i))
                  for i in range(pps)]
    grid_spec = pltpu.PrefetchScalarGridSpec(
        num_scalar_prefetch=1,
        grid=(n, n_pages // pps),
        in_specs=[pl.BlockSpec((None, N_IDX_HEADS, IDX_DIM), lambda b, g, pt: (b, 0, 0)),
                  pl.BlockSpec((None, N_IDX_HEADS, 1), lambda b, g, pt: (b, 0, 0)),
                  pl.BlockSpec((None, 1, IDX_DIM), lambda b, g, pt: (b, 0, 0))] + page_specs,
        out_specs=[pl.BlockSpec((None, pps, PAGE_SIZE), lambda b, g, pt: (b, g, 0)),
                   pl.BlockSpec((None, 1, LANES), lambda b, g, pt: (b, 0, 0))],
    )
    return pl.pallas_call(
        functools.partial(_sample_score_body, pps=pps),
        grid_spec=grid_spec,
        out_shape=[jax.ShapeDtypeStruct((n, n_pages, PAGE_SIZE), F32),
                   jax.ShapeDtypeStruct((n, 1, LANES), F32)],
        compiler_params=_cparams(("parallel", "arbitrary")),
        name="sample_scores",
    )(page_table, qi.reshape(n, N_IDX_HEADS, IDX_DIM), wi.reshape(n, N_IDX_HEADS, 1),
      ki_new.reshape(n, 1, IDX_DIM), *([kidx_t] * pps))


def _sample_select_body(past_ref, self_ref, thr_ref, cut_ref, *, topk, past_len):
    keys = _to_key(past_ref[...])
    kself = _to_key(self_ref[...][:, 0:1])
    n = keys.shape[0]
    kpos = lax.broadcasted_iota(I32, keys.shape, 1)

    def count(pred):
        c = jnp.sum(jnp.where(pred(keys, kpos), 1.0, 0.0), axis=1, keepdims=True)
        return c + jnp.where(pred(kself, past_len), 1.0, 0.0)

    def bisect(i, thr):
        cand = thr + lax.shift_left(jnp.int32(1), 31 - i)
        return jnp.where(count(lambda key, _: key >= cand) >= topk, cand, thr)

    thr = lax.fori_loop(0, 32, bisect, jnp.full((n, 1), INT_MIN, I32))
    c_gt = count(lambda key, _: key > thr)
    c_ge = count(lambda key, _: key >= thr)
    need = topk - c_gt
    tie = (c_ge - c_gt) > need
    thr_ref[...] = jnp.broadcast_to(thr, thr_ref.shape)
    cut_ref[...] = jnp.full(cut_ref.shape, past_len + 1, I32)

    @pl.when(jnp.max(jnp.where(tie, 1.0, 0.0)) > 0.0)
    def _():
        nbits = max(1, past_len.bit_length())

        def bisect_pos(i, lo):
            cand = lo + lax.shift_left(jnp.int32(1), nbits - 1 - i)
            cnt = count(lambda key, pos: jnp.where(key == thr, pos, past_len + 1) < cand)
            return jnp.where(cnt < need, cand, lo)

        lo = lax.fori_loop(0, nbits, bisect_pos, jnp.zeros((n, 1), I32))
        cut_ref[...] = jnp.broadcast_to(jnp.where(tie, lo, past_len + 1), cut_ref.shape)


def _sample_select(sc_past, sc_self, topk):
    n, past_len = sc_past.shape
    full = lambda shape: pl.BlockSpec(shape, lambda i: (0,) * len(shape))
    return pl.pallas_call(
        functools.partial(_sample_select_body, topk=topk, past_len=past_len),
        grid=(1,),
        in_specs=[full((n, past_len)), full((n, LANES))],
        out_specs=[full((n, LANES)), full((n, LANES))],
        out_shape=[jax.ShapeDtypeStruct((n, LANES), I32)] * 2,
        compiler_params=_cparams(("arbitrary",)),
        name="sample_select",
    )(sc_past, sc_self)


def _select_bias(key, kpos, thr, cut):
    bias = jnp.where(key >= jnp.maximum(thr, NEG_INF_KEY + 1), 0.0, NEG_BIG)
    bias = jnp.where(key >= POS_INF_KEY, NEG_BIG, bias)
    return jnp.where(key == thr, jnp.where(kpos > cut, NEG_BIG, bias), bias)


def _heads_to_rows(x):
    h, w = x.shape
    return jnp.broadcast_to(x[:, None, :], (h, HEAD_DIM, w)).reshape(h * HEAD_DIM, w)


def _sample_att_body(pt_ref, thr_ref, cut_ref, q_ref, kn_ref, vn_ref, past_ref, self_ref, *rest, pps, past_len):
    k_refs, v_refs = rest[:pps], rest[pps:2 * pps]
    o_ref, m_scr, l_scr, acc_scr = rest[2 * pps:]
    b = pl.program_id(0)
    g = pl.program_id(1)
    thr = thr_ref[b]
    cut = cut_ref[b]

    @pl.when(g == 0)
    def _():
        m_scr[...] = jnp.full(m_scr.shape, NEG_BIG, F32)
        l_scr[...] = jnp.zeros(l_scr.shape, F32)
        acc_scr[...] = jnp.zeros(acc_scr.shape, F32)

    q = jnp.broadcast_to(q_ref[...].astype(F32), (ATT_W, PAGE_SIZE))
    lane = lax.broadcasted_iota(I32, (1, PAGE_SIZE), 1)

    def head_sums(x):
        return x.reshape(N_ATT_HEADS, HEAD_DIM, x.shape[1]).sum(axis=1)

    parts = []
    for i in range(pps):
        kpos = (g * pps + i) * PAGE_SIZE + lane
        bias = _select_bias(_to_key(past_ref[i:i + 1, :]), kpos, thr, cut)
        parts.append(head_sums(k_refs[i][...] * q) + bias)
    m_old = m_scr[...]
    m_new = m_old
    for s in parts:
        m_new = jnp.maximum(m_new, jnp.max(s, axis=1, keepdims=True))
    alpha = jnp.exp(m_old - m_new)
    l_new = alpha * l_scr[...]
    acc = acc_scr[...] * _heads_to_rows(alpha)
    for i in range(pps):
        e = jnp.exp(parts[i] - m_new)
        l_new = l_new + jnp.sum(e, axis=1, keepdims=True)
        acc = acc + _heads_to_rows(e) * v_refs[i][...]
    m_scr[...] = m_new
    l_scr[...] = l_new
    acc_scr[...] = acc

    @pl.when(g == pl.num_programs(1) - 1)
    def _():
        s_self = head_sums(kn_ref[...] * q_ref[...].astype(F32))
        s_self = s_self + _select_bias(_to_key(self_ref[...][:, 0:1]), past_len, thr, cut)
        m_fin = jnp.maximum(m_new, s_self)
        a2 = jnp.exp(m_new - m_fin)
        e_self = jnp.exp(s_self - m_fin)
        l_fin = a2 * l_new + e_self
        num = jnp.sum(acc, axis=1, keepdims=True) * _heads_to_rows(a2) + _heads_to_rows(e_self) * vn_ref[...]
        o_ref[...] = num / _heads_to_rows(l_fin)


def _sample_attend(page_table, thr, cut, q, k_new, v_new, sc_past, sc_self, k_t, v_t):
    n, n_pages = page_table.shape
    pps = min(PAGES_PER_STEP, n_pages)
    past_len = n_pages * PAGE_SIZE
    page_specs = [pl.BlockSpec((None, ATT_W, PAGE_SIZE),
                               functools.partial(lambda b, g, pt, th, cu, i: (pt[b, g * pps + i], 0, 0), i=i))
                  for i in range(pps)]
    col = pl.BlockSpec((None, ATT_W, 1), lambda b, g, pt, th, cu: (b, 0, 0))
    grid_spec = pltpu.PrefetchScalarGridSpec(
        num_scalar_prefetch=3,
        grid=(n, n_pages // pps),
        in_specs=[col, col, col,
                  pl.BlockSpec((None, pps, PAGE_SIZE), lambda b, g, pt, th, cu: (b, g, 0)),
                  pl.BlockSpec((None, 1, LANES), lambda b, g, pt, th, cu: (b, 0, 0))] + page_specs + page_specs,
        out_specs=col,
        scratch_shapes=[pltpu.VMEM((N_ATT_HEADS, 1), F32), pltpu.VMEM((N_ATT_HEADS, 1), F32),
                        pltpu.VMEM((ATT_W, PAGE_SIZE), F32)],
    )
    out = pl.pallas_call(
        functools.partial(_sample_att_body, pps=pps, past_len=past_len),
        grid_spec=grid_spec,
        out_shape=jax.ShapeDtypeStruct((n, ATT_W, 1), F32),
        compiler_params=_cparams(("parallel", "arbitrary")),
        name="sample_attend",
    )(page_table, thr, cut, q.reshape(n, ATT_W, 1), k_new.reshape(n, ATT_W, 1), v_new.reshape(n, ATT_W, 1),
      sc_past, sc_self, *([k_t] * pps), *([v_t] * pps))
    return out.reshape(n, ATT_W)


def _ret_sample_body(st_ref, q_ref, k_ref, v_ref, o_ref, ns_ref):
    state = st_ref[...]
    q = q_ref[...]
    k = k_ref[...]
    v = v_ref[...]
    hh = lax.broadcasted_iota(I32, (N_RET_HEADS, 1, 1), 0)
    g = jnp.zeros((N_RET_HEADS, 1, 1), F32)
    for h in range(N_RET_HEADS):
        g = jnp.where(hh == h, math.exp(LOG_G[h]), g)
    inner = jnp.sum(q * k, axis=1, keepdims=True) * v
    cross = jnp.sum(q * state, axis=1, keepdims=True) * g
    o_ref[...] = inner + cross
    ns_ref[...] = g * state + k * v


def _ret_sample(state, rq, rk, rv):
    n = state.shape[0]
    col = pl.BlockSpec((None, N_RET_HEADS, HEAD_DIM, 1), lambda b: (b, 0, 0, 0))
    rowv = pl.BlockSpec((None, N_RET_HEADS, 1, HEAD_DIM), lambda b: (b, 0, 0, 0))
    st = pl.BlockSpec((None, N_RET_HEADS, HEAD_DIM, HEAD_DIM), lambda b: (b, 0, 0, 0))
    out, new_state = pl.pallas_call(
        _ret_sample_body,
        grid=(n,),
        in_specs=[st, col, col, rowv],
        out_specs=[rowv, st],
        out_shape=[jax.ShapeDtypeStruct((n, N_RET_HEADS, 1, HEAD_DIM), F32),
                   jax.ShapeDtypeStruct(state.shape, F32)],
        compiler_params=_cparams(("parallel",)),
        name="ret_sample",
    )(state, rq.reshape(n, N_RET_HEADS, HEAD_DIM, 1), rk.reshape(n, N_RET_HEADS, HEAD_DIM, 1),
      rv.reshape(n, N_RET_HEADS, 1, HEAD_DIM))
    return out.reshape(n, RET_W), new_state


def _rope_tables(pos):
    half = HEAD_DIM // 2
    inv = ROPE_BASE ** (-jnp.arange(half, dtype=F32) / half)
    ang = pos.astype(F32)[:, None] * inv[None, :]
    cos, sin = jnp.cos(ang), jnp.sin(ang)
    cos = jnp.tile(jnp.concatenate([cos, cos], axis=1), (1, N_RET_HEADS))
    sin = jnp.tile(jnp.concatenate([-sin, sin], axis=1), (1, N_RET_HEADS))
    return cos, sin


def _block_diag_ones(width):
    r = np.arange(width)[:, None] // HEAD_DIM
    c = np.arange(width)[None, :] // HEAD_DIM
    return jnp.asarray((r == c).astype(np.float32), dtype=BF16)


def _repack_w_in(w):
    o = np.cumsum([0, ATT_W, ATT_W, ATT_W, IDXQ_W, IDX_DIM, N_IDX_HEADS, RET_W, RET_W, RET_W, RET_W, MEM_W])
    pad = jnp.zeros((w.shape[0], KW_W - IDX_DIM - N_IDX_HEADS), w.dtype)
    return jnp.concatenate([w[:, o[0]:o[4]], w[:, o[4]:o[6]], pad, w[:, o[6]:o[11]]], axis=1).astype(BF16)


def kernel(x_prompt, x_sample, mem_prompt, cache_k, cache_v, cache_kidx, state_ret, cache_mem_k, cache_mem_v,
           page_table, ffn1_norm_g, ffn1_w_gu, ffn1_w_down, mix_norm_g, w_in, att_q_norm_g, att_k_norm_g,
           ret_gn_g, mem_norm_g, w_mem_kv, mem_q_norm_g, mem_k_norm_g, w_out, ffn2_norm_g, ffn2_w_gu, ffn2_w_down):
    n_b, s_len, d = x_prompt.shape
    n_s, t_s, _ = x_sample.shape
    assert t_s == 1, "the sample group decodes one token per sequence"
    depth = w_in.shape[0]
    n_mem = mem_prompt.shape[1]
    n_pool = cache_k.shape[1]
    past_len = page_table.shape[1] * PAGE_SIZE
    page_table = page_table.astype(I32)

    cos_p, sin_p = _rope_tables(jnp.arange(s_len, dtype=I32))
    cos_s, sin_s = _rope_tables(jnp.full((n_s,), past_len, I32))
    bd = _block_diag_ones(ATT_W)
    bd_ret = _block_diag_ones(RET_W)
    tile8 = lambda g_: jnp.tile(g_, N_ATT_HEADS).reshape(1, ATT_W)
    tile4 = lambda g_: jnp.tile(g_, N_MEM_HEADS).reshape(1, MEM_W)

    yp = x_prompt.reshape(n_b * s_len, d)
    ys = x_sample.reshape(n_s, d)
    mem = mem_prompt.reshape(n_b * n_mem, d)
    outs = [[] for _ in range(10)]
    for l in range(depth):
        g1 = ffn1_norm_g[l].reshape(1, d)
        g2 = ffn2_norm_g[l].reshape(1, d)
        gmix = mix_norm_g[l].reshape(1, d)
        w1gu, w1d = ffn1_w_gu[l].astype(BF16), ffn1_w_down[l].astype(BF16)
        w2gu, w2d = ffn2_w_gu[l].astype(BF16), ffn2_w_down[l].astype(BF16)
        w_in_l = _repack_w_in(w_in[l])
        w_out_l = w_out[l].astype(BF16)
        gq, gk, gm = tile8(att_q_norm_g[l]), tile8(att_k_norm_g[l]), tile4(mem_q_norm_g[l])
        gn = ret_gn_g[l].reshape(1, RET_W)

        yp = _ffn(yp, g1, w1gu, w1d)
        (q, k, kb, v, vt, qi, kw, ki, ki2, rq, rk, rv, rg, mq) = _proj(
            yp, gmix, w_in_l, gq, gk, gm, cos_p, sin_p, bd, n_b)
        mk, mv = _memkv(mem, mem_norm_g[l].reshape(1, d), w_mem_kv[l].astype(BF16), tile4(mem_k_norm_g[l]), bd_ret)
        att = _dsa_prompt(q, qi, kw, kb, vt, ki2, n_b, s_len)
        ret, s_fin = _ret_prompt(rq, rk, rv, n_b, s_len)
        mo = _mematt_prompt(mq, mk, mv, n_b)
        yp = _mixout(yp, att, ret, rg, mo, gn, bd_ret, w_out_l)
        yp = _ffn(yp, g2, w2gu, w2d)
        outs[0].append(k.reshape(n_b, s_len, N_ATT_HEADS, HEAD_DIM))
        outs[1].append(v.reshape(n_b, s_len, N_ATT_HEADS, HEAD_DIM))
        outs[2].append(ki.reshape(n_b, s_len, IDX_DIM))
        outs[3].append(s_fin)
        outs[4].append(mk.reshape(n_b, n_mem, N_MEM_HEADS, HEAD_DIM))
        outs[5].append(mv.reshape(n_b, n_mem, N_MEM_HEADS, HEAD_DIM))

        ys = _ffn(ys, g1, w1gu, w1d)
        (q, k, kb, v, vt, qi, kw, ki, ki2, rq, rk, rv, rg, mq) = _proj(
            ys, gmix, w_in_l, gq, gk, gm, cos_s, sin_s, bd, 1)
        wi = kw[:, IDX_DIM:IDX_DIM + N_IDX_HEADS]
        sc_past, sc_self = _sample_scores(page_table, qi, wi, ki, jnp.transpose(cache_kidx[l], (0, 2, 1)))
        topk = min(TOPK_MAX, (past_len + t_s) // 4)
        thr, cut = _sample_select(sc_past.reshape(n_s, past_len), sc_self.reshape(n_s, LANES), topk)
        k_t = jnp.transpose(cache_k[l], (0, 2, 3, 1)).reshape(n_pool, ATT_W, PAGE_SIZE)
        v_t = jnp.transpose(cache_v[l], (0, 2, 3, 1)).reshape(n_pool, ATT_W, PAGE_SIZE)
        att = _sample_attend(page_table, thr[:, 0], cut[:, 0], q, k, v, sc_past, sc_self, k_t, v_t)
        ret, s_new = _ret_sample(state_ret[l].astype(F32), rq, rk, rv)
        mo = _mematt_sample(mq, cache_mem_k[l].reshape(n_s, n_mem, MEM_W), cache_mem_v[l].reshape(n_s, n_mem, MEM_W))
        ys = _mixout(ys, att, ret, rg, mo, gn, bd_ret, w_out_l)
        ys = _ffn(ys, g2, w2gu, w2d)
        outs[6].append(k.reshape(n_s, t_s, N_ATT_HEADS, HEAD_DIM))
        outs[7].append(v.reshape(n_s, t_s, N_ATT_HEADS, HEAD_DIM))
        outs[8].append(ki.reshape(n_s, t_s, IDX_DIM))
        outs[9].append(s_new)

    return (yp.reshape(n_b, s_len, d), ys.reshape(n_s, t_s, d)) + tuple(jnp.stack(o) for o in outs)
```

```python
import functools
import math

import numpy as np
import jax
import jax.numpy as jnp
from jax import lax
from jax.experimental import pallas as pl
from jax.experimental.pallas import tpu as pltpu

F32 = jnp.float32
BF16 = jnp.bfloat16
I32 = jnp.int32

HEAD_DIM = 64
N_ATT_HEADS = 8
N_IDX_HEADS = 8
IDX_DIM = 64
TOPK_MAX = 256
N_RET_HEADS = 4
N_MEM_HEADS = 4
PAGE_SIZE = 128
EPS = 1e-6
ROPE_BASE = 10000.0

ATT_W = N_ATT_HEADS * HEAD_DIM
RET_W = N_RET_HEADS * HEAD_DIM
MEM_W = N_MEM_HEADS * HEAD_DIM
IDXQ_W = N_IDX_HEADS * IDX_DIM
LANES = 128
KW_W = LANES
C_Q = 0
C_K = C_Q + ATT_W
C_V = C_K + ATT_W
C_QI = C_V + ATT_W
C_KW = C_QI + IDXQ_W
C_RQ = C_KW + KW_W
C_RK = C_RQ + RET_W
C_RV = C_RK + RET_W
C_RG = C_RV + RET_W
C_MQ = C_RG + RET_W
PROJ_W = C_MQ + MEM_W

QK_SCALE = HEAD_DIM ** -0.5
IDX_SCALE = IDX_DIM ** -0.5
IDX_HEAD_SCALE = N_IDX_HEADS ** -0.5
NEG_BIG = -1e30
M_INIT = -(2.0 ** 100)
INT_MIN = -(2 ** 31)
NEG_INF_KEY = int(np.int32(np.uint32(0x807FFFFF)))
POS_INF_KEY = 0x7F800000
LOG_G = [float(np.log1p(np.float32(-(2.0 ** (-5.0 - h))))) for h in range(N_RET_HEADS)]

VMEM_LIMIT = 56 * 1024 * 1024
ROW_TILE = 512
KEY_CHUNK = 512
Q_TILE = LANES
VT_ROWS = LANES + 16
BISECT_GROUP = 4
MAX_BISECT = 64
INT_MAX = 2 ** 31 - 1
RET_CHUNK = 128
PAGES_PER_STEP = 16


def _cparams(sem):
    return pltpu.CompilerParams(dimension_semantics=sem, vmem_limit_bytes=VMEM_LIMIT)


def _resident(shape):
    nd = len(shape)
    return pl.BlockSpec(shape, lambda *_: (0,) * nd, pipeline_mode=pl.Buffered(1))


def _rms(x, g):
    return x * lax.rsqrt(jnp.mean(x * x, axis=-1, keepdims=True) + EPS) * g


def _head_rms(z, g, bd):
    z2 = z * z
    hi = z2.astype(BF16)
    lo = (z2 - hi.astype(F32)).astype(BF16)
    ss = jnp.dot(hi, bd, preferred_element_type=F32) + jnp.dot(lo, bd, preferred_element_type=F32)
    return z * lax.rsqrt(ss * (1.0 / HEAD_DIM) + EPS) * g


def _nt(a, b):
    return lax.dot_general(a, b, (((1,), (1,)), ((), ())), preferred_element_type=F32)


def _to_key(x):
    b = lax.bitcast_convert_type(x, I32)
    return jnp.where(b < 0, b ^ jnp.int32(0x7FFFFFFF), b)


def _ffn_body(x_ref, g_ref, wgu_ref, wd_ref, o_ref, *, d_ff, fc):
    x = x_ref[...]
    h = _rms(x, g_ref[...]).astype(BF16)
    acc = jnp.zeros(x.shape, F32)
    for c in range(d_ff // fc):
        gate = jnp.dot(h, wgu_ref[:, c * fc:(c + 1) * fc], preferred_element_type=F32)
        up = jnp.dot(h, wgu_ref[:, d_ff + c * fc:d_ff + (c + 1) * fc], preferred_element_type=F32)
        act = (gate * jax.nn.sigmoid(gate) * up).astype(BF16)
        acc = acc + jnp.dot(act, wd_ref[c * fc:(c + 1) * fc, :], preferred_element_type=F32)
    o_ref[...] = x + 0.5 * acc


def _ffn(x, g, wgu, wd):
    t, d = x.shape
    d_ff = wd.shape[0]
    tm = min(ROW_TILE, t)
    fc = d_ff // 2 if (d_ff // 2) % LANES == 0 else d_ff
    return pl.pallas_call(
        functools.partial(_ffn_body, d_ff=d_ff, fc=fc),
        grid=(t // tm,),
        in_specs=[pl.BlockSpec((tm, d), lambda i: (i, 0)), _resident((1, d)),
                  _resident(wgu.shape), _resident(wd.shape)],
        out_specs=pl.BlockSpec((tm, d), lambda i: (i, 0)),
        out_shape=jax.ShapeDtypeStruct((t, d), F32),
        compiler_params=_cparams(("parallel",)),
        name="ffn_half",
    )(x, g, wgu, wd)


def _proj_body(x_ref, g_ref, w_ref, gq_ref, gk_ref, gm_ref, cos_ref, sin_ref, bd_ref,
               q_ref, k_ref, kb_ref, v_ref, vt_ref, qi_ref, kw_ref, ki_ref, ki2_ref,
               rq_ref, rk_ref, rv_ref, rg_ref, mq_ref):
    h = _rms(x_ref[...], g_ref[...]).astype(BF16)

    def proj(a, width):
        return jnp.dot(h, w_ref[:, a:a + width], preferred_element_type=F32)

    bd = bd_ref[...]
    q = _head_rms(proj(C_Q, ATT_W), gq_ref[...], bd)
    q_ref[...] = (q * QK_SCALE).astype(BF16)
    k = _head_rms(proj(C_K, ATT_W), gk_ref[...], bd)
    k_ref[...] = k
    kb_ref[...] = k.astype(BF16)
    v = proj(C_V, ATT_W)
    v_ref[...] = v
    vt = jnp.transpose(v)
    tail_row = lax.broadcasted_iota(I32, (VT_ROWS - LANES, vt.shape[1]), 0)
    tail = jnp.where(tail_row == 0, 1.0, 0.0)
    for p in range(N_ATT_HEADS // 2):
        vt_ref[p] = jnp.concatenate([vt[p * LANES:(p + 1) * LANES, :], tail], axis=0).astype(BF16)
    qi_ref[...] = proj(C_QI, IDXQ_W).astype(BF16)
    kw = proj(C_KW, KW_W)
    kw_ref[...] = kw
    ki = kw[:, :IDX_DIM]
    ki_ref[...] = ki
    kib = ki.astype(BF16)
    ki2_ref[...] = jnp.concatenate([kib, kib], axis=1)

    lane = lax.broadcasted_iota(I32, (1, RET_W), 1)
    first_half = (lane % HEAD_DIM) < (HEAD_DIM // 2)
    cos = cos_ref[...]
    sin = sin_ref[...]

    def rot(x):
        swapped = jnp.where(first_half, pltpu.roll(x, RET_W - HEAD_DIM // 2, axis=1),
                            pltpu.roll(x, HEAD_DIM // 2, axis=1))
        return x * cos + swapped * sin

    rq_ref[...] = rot(proj(C_RQ, RET_W))
    rk_ref[...] = rot(proj(C_RK, RET_W)) * QK_SCALE
    rv_ref[...] = proj(C_RV, RET_W)
    rg_ref[...] = proj(C_RG, RET_W)
    mq = _head_rms(proj(C_MQ, MEM_W), gm_ref[...], bd[:MEM_W, :MEM_W])
    mq_ref[...] = (mq * QK_SCALE).astype(BF16)


def _proj(y, g, w, gq, gk, gm, cos, sin, bd, n_batch):
    t, d = y.shape
    s_len = t // n_batch
    tm = min(ROW_TILE, s_len)
    nsb = s_len // tm
    row = lambda w_: pl.BlockSpec((tm, w_), lambda i: (i, 0))
    tab = pl.BlockSpec((tm, RET_W), lambda i: (i % nsb, 0))
    out_shapes = [
        (jax.ShapeDtypeStruct((t, ATT_W), BF16), row(ATT_W)),
        (jax.ShapeDtypeStruct((t, ATT_W), F32), row(ATT_W)),
        (jax.ShapeDtypeStruct((t, ATT_W), BF16), row(ATT_W)),
        (jax.ShapeDtypeStruct((t, ATT_W), F32), row(ATT_W)),
        (jax.ShapeDtypeStruct((n_batch, nsb, N_ATT_HEADS // 2, VT_ROWS, tm), BF16),
         pl.BlockSpec((None, None, N_ATT_HEADS // 2, VT_ROWS, tm),
                      lambda i: (i // nsb, i % nsb, 0, 0, 0))),
        (jax.ShapeDtypeStruct((t, IDXQ_W), BF16), row(IDXQ_W)),
        (jax.ShapeDtypeStruct((t, KW_W), F32), row(KW_W)),
        (jax.ShapeDtypeStruct((t, IDX_DIM), F32), row(IDX_DIM)),
        (jax.ShapeDtypeStruct((t, 2 * IDX_DIM), BF16), row(2 * IDX_DIM)),
        (jax.ShapeDtypeStruct((t, RET_W), F32), row(RET_W)),
        (jax.ShapeDtypeStruct((t, RET_W), F32), row(RET_W)),
        (jax.ShapeDtypeStruct((t, RET_W), F32), row(RET_W)),
        (jax.ShapeDtypeStruct((t, RET_W), F32), row(RET_W)),
        (jax.ShapeDtypeStruct((t, MEM_W), BF16), row(MEM_W)),
    ]
    return pl.pallas_call(
        _proj_body,
        grid=(t // tm,),
        in_specs=[row(d), _resident((1, d)), _resident(w.shape), _resident((1, ATT_W)),
                  _resident((1, ATT_W)), _resident((1, MEM_W)), tab, tab, _resident(bd.shape)],
        out_specs=[s for _, s in out_shapes],
        out_shape=[s for s, _ in out_shapes],
        compiler_params=_cparams(("parallel",)),
        name="mix_proj",
    )(y, g, w, gq, gk, gm, cos, sin, bd)


def _memkv_body(x_ref, g_ref, w_ref, gk_ref, bd_ref, mk_ref, mv_ref):
    h = _rms(x_ref[...], g_ref[...]).astype(BF16)
    z = jnp.dot(h, w_ref[...], preferred_element_type=F32)
    mk_ref[...] = _head_rms(z[:, :MEM_W], gk_ref[...], bd_ref[...])
    mv_ref[...] = z[:, MEM_W:]


def _memkv(mem, g, w, gk, bd):
    t, d = mem.shape
    tm = min(ROW_TILE, t)
    row = lambda w_: pl.BlockSpec((tm, w_), lambda i: (i, 0))
    return pl.pallas_call(
        _memkv_body,
        grid=(t // tm,),
        in_specs=[row(d), _resident((1, d)), _resident(w.shape), _resident((1, MEM_W)), _resident(bd.shape)],
        out_specs=[row(MEM_W), row(MEM_W)],
        out_shape=[jax.ShapeDtypeStruct((t, MEM_W), F32)] * 2,
        compiler_params=_cparams(("parallel",)),
        name="mem_kv",
    )(mem, g, w, gk, bd)


def _dsa_body(q_ref, qi_ref, kw_ref, k_ref, vt_ref, ki2_ref, o_ref,
              key_scr, qi2_scr, q2_scr, cut_scr, m_scr, acc_scr, sa_scr, sb_scr, *, topk, sc, s_len):
    j = pl.program_id(1)
    n_ch = (j * Q_TILE + Q_TILE + sc - 1) // sc
    n_pair = N_ATT_HEADS // 2
    lane = lax.broadcasted_iota(I32, (1, LANES), 1)
    qpos = j * Q_TILE + lane
    kiota = lax.broadcasted_iota(I32, (sc, LANES), 0)

    r2 = lax.broadcasted_iota(I32, (2 * Q_TILE, LANES), 0)
    c2 = lax.broadcasted_iota(I32, (2 * Q_TILE, LANES), 1)
    keep = (r2 < Q_TILE) == (c2 < HEAD_DIM)
    eye2 = jnp.where(r2 % Q_TILE == c2, 1.0, 0.0).astype(BF16)
    for p in range(n_pair):
        a = qi_ref[:, p * LANES:(p + 1) * LANES].astype(F32)
        qi2_scr[p] = jnp.where(keep, jnp.concatenate([a, a], axis=0), 0.0).astype(BF16)
        a = q_ref[:, p * LANES:(p + 1) * LANES].astype(F32)
        q2 = jnp.where(keep, jnp.concatenate([a, a], axis=0), 0.0).astype(BF16)
        q2_scr[p] = jnp.concatenate([q2, eye2], axis=1)

    w_t = jnp.transpose(kw_ref[...])[IDX_DIM:IDX_DIM + N_IDX_HEADS, :]
    w_s = (w_t * IDX_HEAD_SCALE) * IDX_SCALE

    def score_chunk(c, carry):
        kmax, kmin = carry
        off = pl.multiple_of(c * sc, sc)
        kc = ki2_ref[pl.ds(off, sc), :]
        acc = jnp.zeros((sc, LANES), F32)
        for p in range(n_pair):
            lg = _nt(kc, qi2_scr[p])
            acc = acc + jnp.maximum(lg[:, :LANES], 0.0) * w_s[2 * p:2 * p + 1, :]
            acc = acc + jnp.maximum(lg[:, LANES:], 0.0) * w_s[2 * p + 1:2 * p + 2, :]
        valid = off + kiota <= qpos
        key = _to_key(acc)
        key_scr[pl.ds(off, sc), :] = jnp.where(valid, key, INT_MIN)
        kmax = jnp.maximum(kmax, jnp.where(valid, key, INT_MIN).reshape(sc // 64, 64, LANES).max(axis=0))
        kmin = jnp.minimum(kmin, jnp.where(valid, key, INT_MAX).reshape(sc // 64, 64, LANES).min(axis=0))
        return kmax, kmin

    kmax, kmin = lax.fori_loop(0, n_ch, score_chunk, (jnp.full((64, LANES), INT_MIN, I32),
                                                      jnp.full((64, LANES), INT_MAX, I32)))
    kmax = kmax.reshape(8, 8, LANES).max(axis=0).max(axis=0, keepdims=True)
    kmin = kmin.reshape(8, 8, LANES).min(axis=0).min(axis=0, keepdims=True)

    def count(pred):
        def body(c, acc):
            off = pl.multiple_of(c * sc, sc)
            for g in range(sc // 64):
                rows = pl.ds(off + g * 64, 64)
                acc = jnp.where(pred(key_scr[rows, :], off + kiota[g * 64:(g + 1) * 64]), acc + 1.0, acc)
            return acc
        acc = lax.fori_loop(0, n_ch, body, jnp.zeros((64, LANES), F32))
        return acc.reshape(8, 8, LANES).sum(axis=0).sum(axis=0, keepdims=True)

    def key_to_f32(k):
        return lax.bitcast_convert_type(jnp.where(k < 0, k ^ jnp.int32(0x7FFFFFFF), k), F32)

    def bisect_group(state):
        i, lo, hi, active, tied, _ = state
        for _ in range(BISECT_GROUP):
            mid_f = _to_key(0.5 * key_to_f32(lo) + 0.5 * key_to_f32(hi))
            mid_i = (lo >> 1) + (hi >> 1) + (lo & hi & 1)
            mid = jnp.where((mid_f > lo) & (mid_f < hi), mid_f, mid_i)
            cnt = count(lambda key, _: key >= mid)
            on = active > 0.0
            lo = jnp.where(on & (cnt >= topk), mid, lo)
            hi = jnp.where(on & (cnt < topk), mid, hi)
            spent = on & (cnt != topk) & (hi <= lo + 1)
            tied = jnp.where(spent, 1.0, tied)
            active = jnp.where((cnt == topk) | spent, 0.0, active)
        return i + BISECT_GROUP, lo, hi, active, tied, jnp.max(active)

    active0 = jnp.where(qpos + 1 <= topk, 0.0, 1.0)
    _, thr, _, _, tied, _ = lax.while_loop(
        lambda st: (st[0] < MAX_BISECT) & (st[5] > 0.0), bisect_group,
        (jnp.int32(0), jnp.where(active0 > 0.0, kmin, INT_MIN), kmax + 1, active0,
         jnp.zeros((1, LANES), F32), jnp.max(active0)))
    cut_scr[...] = jnp.full((1, LANES), s_len, I32)

    @pl.when(jnp.max(tied) > 0.0)
    def _():
        c_gt = count(lambda key, _: key > thr)
        c_ge = count(lambda key, _: key >= thr)
        need = topk - c_gt
        tie = ((c_ge - c_gt) > need) & (tied > 0.0)
        nbits = max(1, (s_len - 1).bit_length())

        def bisect_pos(i, lo):
            cand = lo + lax.shift_left(jnp.int32(1), nbits - 1 - i)
            cnt = count(lambda key, kpos: jnp.where(key == thr, kpos, s_len) < cand)
            return jnp.where(cnt < need, cand, lo)

        lo = lax.fori_loop(0, nbits, bisect_pos, jnp.zeros((1, LANES), I32))
        cut_scr[...] = jnp.where(tie, lo, s_len)

    cut = cut_scr[...]
    thr_lo = jnp.maximum(thr, NEG_INF_KEY + 1)

    m_scr[...] = jnp.full(m_scr.shape, M_INIT, F32)
    acc_scr[...] = jnp.zeros(acc_scr.shape, F32)

    def qk_chunk(c, s_scr):
        off = pl.multiple_of(c * sc, sc)
        key = key_scr[pl.ds(off, sc), :]
        bias = jnp.where(key >= thr_lo, 0.0, NEG_BIG)
        bias = jnp.where(key >= POS_INF_KEY, NEG_BIG, bias)
        bias = jnp.where(key == thr, jnp.where(off + kiota > cut, NEG_BIG, bias), bias).astype(BF16)
        for p in range(n_pair):
            kc = jnp.concatenate([k_ref[pl.ds(off, sc), p * LANES:(p + 1) * LANES], bias], axis=1)
            s_scr[p] = _nt(kc, q2_scr[p])

    def softmax_pv_chunk(c, s_scr):
        es, alphas = [], []
        for p in range(n_pair):
            s = s_scr[p].astype(BF16)
            m_c = s.reshape(sc // 64, 64, 2 * Q_TILE).max(axis=0)
            m_c = m_c.reshape(4, 16, 2 * Q_TILE).max(axis=0).astype(F32).max(axis=0, keepdims=True)
            m_old = m_scr[p]
            m_new = jnp.maximum(m_old, m_c)
            alphas.append(jnp.exp(m_old - m_new))
            es.append(jnp.exp(s - m_new.astype(BF16)))
            m_scr[p] = m_new
        for p in range(n_pair):
            pv = jnp.dot(vt_ref[c, p], es[p], preferred_element_type=F32)
            acc_scr[p] = acc_scr[p] * alphas[p] + pv

    def att_two_chunks(i, carry):
        c = 2 * i
        qk_chunk(c + 1, sb_scr)
        softmax_pv_chunk(c, sa_scr)
        qk_chunk(c + 2, sa_scr)
        softmax_pv_chunk(c + 1, sb_scr)
        return carry

    qk_chunk(0, sa_scr)
    n_two = (n_ch - 1) // 2
    lax.fori_loop(0, n_two, att_two_chunks, 0)

    @pl.when(n_ch % 2 == 0)
    def _():
        qk_chunk(n_ch - 1, sb_scr)
        softmax_pv_chunk(n_ch - 2, sa_scr)
        softmax_pv_chunk(n_ch - 1, sb_scr)

    @pl.when(n_ch % 2 == 1)
    def _():
        softmax_pv_chunk(n_ch - 1, sa_scr)

    outs = []
    for p in range(n_pair):
        a = acc_scr[p]
        l = a[LANES:LANES + 1, :]
        outs.append(a[:HEAD_DIM, :LANES] / l[:, :LANES])
        outs.append(a[HEAD_DIM:LANES, LANES:] / l[:, LANES:])
    o_ref[...] = jnp.transpose(jnp.concatenate(outs, axis=0))


def _dsa_prompt(q, qi, kw, kb, vt, ki2, n_batch, s_len):
    t = q.shape[0]
    sc = min(KEY_CHUNK, s_len)
    nq = s_len // Q_TILE
    topk = min(TOPK_MAX, s_len // 4)
    qrow = lambda w_: pl.BlockSpec((Q_TILE, w_), lambda b, j: (b * nq + j, 0))
    per_batch = lambda w_: pl.BlockSpec((s_len, w_), lambda b, j: (b, 0), pipeline_mode=pl.Buffered(1))
    return pl.pallas_call(
        functools.partial(_dsa_body, topk=topk, sc=sc, s_len=s_len),
        grid=(n_batch, nq),
        in_specs=[qrow(ATT_W), qrow(IDXQ_W), qrow(KW_W), per_batch(ATT_W),
                  pl.BlockSpec((None, s_len // sc, N_ATT_HEADS // 2, VT_ROWS, sc), lambda b, j: (b, 0, 0, 0, 0),
                               pipeline_mode=pl.Buffered(1)),
                  per_batch(2 * IDX_DIM)],
        out_specs=qrow(ATT_W),
        out_shape=jax.ShapeDtypeStruct((t, ATT_W), F32),
        scratch_shapes=[pltpu.VMEM((s_len, LANES), I32),
                        pltpu.VMEM((N_ATT_HEADS // 2, 2 * Q_TILE, LANES), BF16),
                        pltpu.VMEM((N_ATT_HEADS // 2, 2 * Q_TILE, 2 * LANES), BF16),
                        pltpu.VMEM((1, LANES), I32),
                        pltpu.VMEM((N_ATT_HEADS // 2, 1, 2 * Q_TILE), F32),
                        pltpu.VMEM((N_ATT_HEADS // 2, VT_ROWS, 2 * Q_TILE), F32),
                        pltpu.VMEM((N_ATT_HEADS // 2, sc, 2 * Q_TILE), F32),
                        pltpu.VMEM((N_ATT_HEADS // 2, sc, 2 * Q_TILE), F32)],
        compiler_params=_cparams(("parallel", "arbitrary")),
        name="dsa_prompt",
    )(q, qi, kw, kb, vt, ki2)


def _ret_lane_const(vals):
    lane = lax.broadcasted_iota(I32, (1, RET_W), 1)
    out = jnp.zeros((1, RET_W), F32)
    for h, v in enumerate(vals):
        out = jnp.where(lane // HEAD_DIM == h, v, out)
    return out


def _ret_body(rq_ref, rk_ref, rv_ref, o_ref, st_ref, sbd_scr, *, ch):
    c = pl.program_id(1)

    @pl.when(c == 0)
    def _():
        sbd_scr[...] = jnp.zeros(sbd_scr.shape, F32)

    q = rq_ref[...]
    k = rk_ref[...]
    vb = rv_ref[...].astype(BF16)
    qb = q.astype(BF16)
    kb = k.astype(BF16)
    head = lax.broadcasted_iota(I32, (1, RET_W), 1) // HEAD_DIM
    log_g = _ret_lane_const(LOG_G)
    i_col = lax.broadcasted_iota(I32, (ch, 1), 0).astype(F32)
    state = sbd_scr[...]

    cross = jnp.dot(qb, state.astype(BF16), preferred_element_type=F32) * jnp.exp(log_g * (i_col + 1.0))
    ii = lax.broadcasted_iota(I32, (ch, ch), 0)
    jj = lax.broadcasted_iota(I32, (ch, ch), 1)
    causal = ii >= jj
    diff = jnp.where(causal, ii - jj, 0).astype(F32)
    inner = jnp.zeros((ch, RET_W), F32)
    for h in range(N_RET_HEADS):
        qm = jnp.where(head == h, q, 0.0).astype(BF16)
        decay = jnp.where(causal, jnp.exp(LOG_G[h] * diff), 0.0)
        sc = (_nt(qm, kb) * decay).astype(BF16)
        inner = inner + jnp.where(head == h, jnp.dot(sc, vb, preferred_element_type=F32), 0.0)
    o_ref[...] = inner + cross

    kd = (k * jnp.exp(log_g * (ch - 1.0 - i_col))).astype(BF16)
    kv = lax.dot_general(kd, vb, (((0,), (0,)), ((), ())), preferred_element_type=F32)
    rh = lax.broadcasted_iota(I32, (RET_W, RET_W), 0) // HEAD_DIM
    ch_ = lax.broadcasted_iota(I32, (RET_W, RET_W), 1) // HEAD_DIM
    new_state = state * jnp.exp(log_g * float(ch)) + jnp.where(rh == ch_, kv, 0.0)
    sbd_scr[...] = new_state

    @pl.when(c == pl.num_programs(1) - 1)
    def _():
        for h in range(N_RET_HEADS):
            st_ref[h] = new_state[h * HEAD_DIM:(h + 1) * HEAD_DIM, h * HEAD_DIM:(h + 1) * HEAD_DIM]


def _ret_prompt(rq, rk, rv, n_batch, s_len):
    t = rq.shape[0]
    ch = min(RET_CHUNK, s_len)
    nc = s_len // ch
    blk = pl.BlockSpec((ch, RET_W), lambda b, c: (b * nc + c, 0))
    return pl.pallas_call(
        functools.partial(_ret_body, ch=ch),
        grid=(n_batch, nc),
        in_specs=[blk, blk, blk],
        out_specs=[blk, pl.BlockSpec((None, N_RET_HEADS, HEAD_DIM, HEAD_DIM), lambda b, c: (b, 0, 0, 0))],
        out_shape=[jax.ShapeDtypeStruct((t, RET_W), F32),
                   jax.ShapeDtypeStruct((n_batch, N_RET_HEADS, HEAD_DIM, HEAD_DIM), F32)],
        scratch_shapes=[pltpu.VMEM((RET_W, RET_W), F32)],
        compiler_params=_cparams(("parallel", "arbitrary")),
        name="ret_prompt",
    )(rq, rk, rv)


def _softmax_lanes(s):
    m = jnp.max(s, axis=-1, keepdims=True)
    e = jnp.exp(s - m)
    return e / jnp.sum(e, axis=-1, keepdims=True)


def _mematt_body(mq_ref, mk_ref, mv_ref, o_ref):
    mq = mq_ref[...].astype(F32)
    mk = mk_ref[...].astype(BF16)
    mv = mv_ref[...].astype(BF16)
    head = lax.broadcasted_iota(I32, (1, MEM_W), 1) // HEAD_DIM
    out = jnp.zeros(mq.shape, F32)
    for h in range(N_MEM_HEADS):
        qm = jnp.where(head == h, mq, 0.0).astype(BF16)
        p = _softmax_lanes(_nt(qm, mk)).astype(BF16)
        out = out + jnp.where(head == h, jnp.dot(p, mv, preferred_element_type=F32), 0.0)
    o_ref[...] = out


def _mematt_prompt(mq, mk, mv, n_batch):
    t = mq.shape[0]
    n_mem = mk.shape[0] // n_batch
    s_len = t // n_batch
    tm = min(ROW_TILE, s_len)
    nsb = s_len // tm
    row = pl.BlockSpec((tm, MEM_W), lambda i: (i, 0))
    mem = pl.BlockSpec((n_mem, MEM_W), lambda i: (i // nsb, 0))
    return pl.pallas_call(
        _mematt_body,
        grid=(t // tm,),
        in_specs=[row, mem, mem],
        out_specs=row,
        out_shape=jax.ShapeDtypeStruct((t, MEM_W), F32),
        compiler_params=_cparams(("parallel",)),
        name="mem_attend",
    )(mq, mk, mv)


def _mematt_sample_body(mq_ref, mk_ref, mv_ref, o_ref):
    rows = 8
    mq = jnp.broadcast_to(mq_ref[...].astype(F32), (rows, MEM_W))
    sel = lax.broadcasted_iota(I32, (rows, MEM_W), 1) // HEAD_DIM == lax.broadcasted_iota(I32, (rows, MEM_W), 0)
    qbd = jnp.where(sel, mq, 0.0).astype(BF16)
    p = _softmax_lanes(_nt(qbd, mk_ref[...].astype(BF16))).astype(BF16)
    o = jnp.dot(p, mv_ref[...].astype(BF16), preferred_element_type=F32)
    o_ref[...] = jnp.sum(jnp.where(sel, o, 0.0), axis=0, keepdims=True)


def _mematt_sample(mq, mk, mv):
    n, n_mem = mk.shape[0], mk.shape[1]
    one = pl.BlockSpec((None, 1, MEM_W), lambda b: (b, 0, 0))
    mem = pl.BlockSpec((None, n_mem, MEM_W), lambda b: (b, 0, 0))
    out = pl.pallas_call(
        _mematt_sample_body,
        grid=(n,),
        in_specs=[one, mem, mem],
        out_specs=one,
        out_shape=jax.ShapeDtypeStruct((n, 1, MEM_W), F32),
        compiler_params=_cparams(("parallel",)),
        name="mem_attend_sample",
    )(mq.reshape(n, 1, MEM_W), mk, mv)
    return out.reshape(n, MEM_W)


def _mixout_body(y_ref, att_ref, ret_ref, rg_ref, mo_ref, gn_ref, bd_ref, w_ref, o_ref):
    retn = _head_rms(ret_ref[...], gn_ref[...], bd_ref[...])
    rg = rg_ref[...]
    gated = (rg * jax.nn.sigmoid(rg)) * retn
    acc = jnp.dot(att_ref[...].astype(BF16), w_ref[:ATT_W, :], preferred_element_type=F32)
    acc = acc + jnp.dot(gated.astype(BF16), w_ref[ATT_W:ATT_W + RET_W, :], preferred_element_type=F32)
    acc = acc + jnp.dot(mo_ref[...].astype(BF16), w_ref[ATT_W + RET_W:, :], preferred_element_type=F32)
    o_ref[...] = y_ref[...] + acc


def _mixout(y, att, ret, rg, mo, gn, bd, w):
    t, d = y.shape
    tm = min(ROW_TILE, t)
    row = lambda w_: pl.BlockSpec((tm, w_), lambda i: (i, 0))
    return pl.pallas_call(
        _mixout_body,
        grid=(t // tm,),
        in_specs=[row(d), row(ATT_W), row(RET_W), row(RET_W), row(MEM_W), _resident((1, RET_W)),
                  _resident(bd.shape), _resident(w.shape)],
        out_specs=row(d),
        out_shape=jax.ShapeDtypeStruct((t, d), F32),
        compiler_params=_cparams(("parallel",)),
        name="mix_out",
    )(y, att, ret, rg, mo, gn, bd, w)


def _sample_score_body(pt_ref, qi_ref, w_ref, kin_ref, *rest, pps):
    page_refs, (past_ref, self_ref) = rest[:pps], rest[pps:]
    q8 = qi_ref[...]
    w8 = (w_ref[...] * IDX_HEAD_SCALE) * IDX_SCALE

    def head_sum(lg):
        s = jnp.sum(jnp.maximum(lg, 0.0) * w8, axis=0, keepdims=True)
        return jnp.where(s == 0.0, 0.0, s)

    for i in range(pps):
        lg = jnp.dot(q8, page_refs[i][...].astype(BF16), preferred_element_type=F32)
        past_ref[i:i + 1, :] = head_sum(lg)

    @pl.when(pl.program_id(1) == 0)
    def _():
        kn = kin_ref[...].astype(BF16).astype(F32)
        lg = jnp.sum(q8.astype(F32) * kn, axis=1, keepdims=True)
        self_ref[...] = jnp.broadcast_to(head_sum(lg), self_ref.shape)


def _sample_scores(page_table, qi, wi, ki_new, kidx_t):
    n, n_pages = page_table.shape
    pps = min(2 * PAGES_PER_STEP, n_pages)
    page_specs = [pl.BlockSpec((None, IDX_DIM, PAGE_SIZE),
                               functools.partial(lambda b, g, pt, i: (pt[b, g * pps + i], 0, 0), i=i))
                  for i in range(pps)]
    grid_spec = pltpu.PrefetchScalarGridSpec(
        num_scalar_prefetch=1,
        grid=(n, n_pages // pps),
        in_specs=[pl.BlockSpec((None, N_IDX_HEADS, IDX_DIM), lambda b, g, pt: (b, 0, 0)),
                  pl.BlockSpec((None, N_IDX_HEADS, 1), lambda b, g, pt: (b, 0, 0)),
                  pl.BlockSpec((None, 1, IDX_DIM), lambda b, g, pt: (b, 0, 0))] + page_specs,
        out_specs=[pl.BlockSpec((None, pps, PAGE_SIZE), lambda b, g, pt: (b, g, 0)),
                   pl.BlockSpec((None, 1, LANES), lambda b, g, pt: (b, 0, 0))],
    )
    return pl.pallas_call(
        functools.partial(_sample_score_body, pps=pps),
        grid_spec=grid_spec,
        out_shape=[jax.ShapeDtypeStruct((n, n_pages, PAGE_SIZE), F32),
                   jax.ShapeDtypeStruct((n, 1, LANES), F32)],
        compiler_params=_cparams(("parallel", "arbitrary")),
        name="sample_scores",
    )(page_table, qi.reshape(n, N_IDX_HEADS, IDX_DIM), wi.reshape(n, N_IDX_HEADS, 1),
      ki_new.reshape(n, 1, IDX_DIM), *([kidx_t] * pps))


def _sample_select_body(past_ref, self_ref, thr_ref, cut_ref, *, topk, past_len):
    keys = _to_key(past_ref[...])
    kself = _to_key(self_ref[...][:, 0:1])
    n = keys.shape[0]
    kpos = lax.broadcasted_iota(I32, keys.shape, 1)

    def count(pred):
        c = jnp.sum(jnp.where(pred(keys, kpos), 1.0, 0.0), axis=1, keepdims=True)
        return c + jnp.where(pred(kself, past_len), 1.0, 0.0)

    def bisect(i, thr):
        cand = thr + lax.shift_left(jnp.int32(1), 31 - i)
        return jnp.where(count(lambda key, _: key >= cand) >= topk, cand, thr)

    thr = lax.fori_loop(0, 32, bisect, jnp.full((n, 1), INT_MIN, I32))
    c_gt = count(lambda key, _: key > thr)
    c_ge = count(lambda key, _: key >= thr)
    need = topk - c_gt
    tie = (c_ge - c_gt) > need
    thr_ref[...] = jnp.broadcast_to(thr, thr_ref.shape)
    cut_ref[...] = jnp.full(cut_ref.shape, past_len + 1, I32)

    @pl.when(jnp.max(jnp.where(tie, 1.0, 0.0)) > 0.0)
    def _():
        nbits = max(1, past_len.bit_length())

        def bisect_pos(i, lo):
            cand = lo + lax.shift_left(jnp.int32(1), nbits - 1 - i)
            cnt = count(lambda key, pos: jnp.where(key == thr, pos, past_len + 1) < cand)
            return jnp.where(cnt < need, cand, lo)

        lo = lax.fori_loop(0, nbits, bisect_pos, jnp.zeros((n, 1), I32))
        cut_ref[...] = jnp.broadcast_to(jnp.where(tie, lo, past_len + 1), cut_ref.shape)


def _sample_select(sc_past, sc_self, topk):
    n, past_len = sc_past.shape
    full = lambda shape: pl.BlockSpec(shape, lambda i: (0,) * len(shape))
    return pl.pallas_call(
        functools.partial(_sample_select_body, topk=topk, past_len=past_len),
        grid=(1,),
        in_specs=[full((n, past_len)), full((n, LANES))],
        out_specs=[full((n, LANES)), full((n, LANES))],
        out_shape=[jax.ShapeDtypeStruct((n, LANES), I32)] * 2,
        compiler_params=_cparams(("arbitrary",)),
        name="sample_select",
    )(sc_past, sc_self)


def _select_bias(key, kpos, thr, cut):
    bias = jnp.where(key >= jnp.maximum(thr, NEG_INF_KEY + 1), 0.0, NEG_BIG)
    bias = jnp.where(key >= POS_INF_KEY, NEG_BIG, bias)
    return jnp.where(key == thr, jnp.where(kpos > cut, NEG_BIG, bias), bias)


def _heads_to_rows(x):
    h, w = x.shape
    return jnp.broadcast_to(x[:, None, :], (h, HEAD_DIM, w)).reshape(h * HEAD_DIM, w)


def _sample_att_body(pt_ref, thr_ref, cut_ref, q_ref, kn_ref, vn_ref, past_ref, self_ref, *rest, pps, past_len):
    k_refs, v_refs = rest[:pps], rest[pps:2 * pps]
    o_ref, m_scr, l_scr, acc_scr = rest[2 * pps:]
    b = pl.program_id(0)
    g = pl.program_id(1)
    thr = thr_ref[b]
    cut = cut_ref[b]

    @pl.when(g == 0)
    def _():
        m_scr[...] = jnp.full(m_scr.shape, NEG_BIG, F32)
        l_scr[...] = jnp.zeros(l_scr.shape, F32)
        acc_scr[...] = jnp.zeros(acc_scr.shape, F32)

    q = jnp.broadcast_to(q_ref[...].astype(F32), (ATT_W, PAGE_SIZE))
    lane = lax.broadcasted_iota(I32, (1, PAGE_SIZE), 1)

    def head_sums(x):
        return x.reshape(N_ATT_HEADS, HEAD_DIM, x.shape[1]).sum(axis=1)

    parts = []
    for i in range(pps):
        kpos = (g * pps + i) * PAGE_SIZE + lane
        bias = _select_bias(_to_key(past_ref[i:i + 1, :]), kpos, thr, cut)
        parts.append(head_sums(k_refs[i][...] * q) + bias)
    m_old = m_scr[...]
    m_new = m_old
    for s in parts:
        m_new = jnp.maximum(m_new, jnp.max(s, axis=1, keepdims=True))
    alpha = jnp.exp(m_old - m_new)
    l_new = alpha * l_scr[...]
    acc = acc_scr[...] * _heads_to_rows(alpha)
    for i in range(pps):
        e = jnp.exp(parts[i] - m_new)
        l_new = l_new + jnp.sum(e, axis=1, keepdims=True)
        acc = acc + _heads_to_rows(e) * v_refs[i][...]
    m_scr[...] = m_new
    l_scr[...] = l_new
    acc_scr[...] = acc

    @pl.when(g == pl.num_programs(1) - 1)
    def _():
        s_self = head_sums(kn_ref[...] * q_ref[...].astype(F32))
        s_self = s_self + _select_bias(_to_key(self_ref[...][:, 0:1]), past_len, thr, cut)
        m_fin = jnp.maximum(m_new, s_self)
        a2 = jnp.exp(m_new - m_fin)
        e_self = jnp.exp(s_self - m_fin)
        l_fin = a2 * l_new + e_self
        num = jnp.sum(acc, axis=1, keepdims=True) * _heads_to_rows(a2) + _heads_to_rows(e_self) * vn_ref[...]
        o_ref[...] = num / _heads_to_rows(l_fin)


def _sample_attend(page_table, thr, cut, q, k_new, v_new, sc_past, sc_self, k_t, v_t):
    n, n_pages = page_table.shape
    pps = min(PAGES_PER_STEP, n_pages)
    past_len = n_pages * PAGE_SIZE
    page_specs = [pl.BlockSpec((None, ATT_W, PAGE_SIZE),
                               functools.partial(lambda b, g, pt, th, cu, i: (pt[b, g * pps + i], 0, 0), i=i))
                  for i in range(pps)]
    col = pl.BlockSpec((None, ATT_W, 1), lambda b, g, pt, th, cu: (b, 0, 0))
    grid_spec = pltpu.PrefetchScalarGridSpec(
        num_scalar_prefetch=3,
        grid=(n, n_pages // pps),
        in_specs=[col, col, col,
                  pl.BlockSpec((None, pps, PAGE_SIZE), lambda b, g, pt, th, cu: (b, g, 0)),
                  pl.BlockSpec((None, 1, LANES), lambda b, g, pt, th, cu: (b, 0, 0))] + page_specs + page_specs,
        out_specs=col,
        scratch_shapes=[pltpu.VMEM((N_ATT_HEADS, 1), F32), pltpu.VMEM((N_ATT_HEADS, 1), F32),
                        pltpu.VMEM((ATT_W, PAGE_SIZE), F32)],
    )
    out = pl.pallas_call(
        functools.partial(_sample_att_body, pps=pps, past_len=past_len),
        grid_spec=grid_spec,
        out_shape=jax.ShapeDtypeStruct((n, ATT_W, 1), F32),
        compiler_params=_cparams(("parallel", "arbitrary")),
        name="sample_attend",
    )(page_table, thr, cut, q.reshape(n, ATT_W, 1), k_new.reshape(n, ATT_W, 1), v_new.reshape(n, ATT_W, 1),
      sc_past, sc_self, *([k_t] * pps), *([v_t] * pps))
    return out.reshape(n, ATT_W)


def _ret_sample_body(st_ref, q_ref, k_ref, v_ref, o_ref, ns_ref):
    state = st_ref[...]
    q = q_ref[...]
    k = k_ref[...]
    v = v_ref[...]
    hh = lax.broadcasted_iota(I32, (N_RET_HEADS, 1, 1), 0)
    g = jnp.zeros((N_RET_HEADS, 1, 1), F32)
    for h in range(N_RET_HEADS):
        g = jnp.where(hh == h, math.exp(LOG_G[h]), g)
    inner = jnp.sum(q * k, axis=1, keepdims=True) * v
    cross = jnp.sum(q * state, axis=1, keepdims=True) * g
    o_ref[...] = inner + cross
    ns_ref[...] = g * state + k * v


def _ret_sample(state, rq, rk, rv):
    n = state.shape[0]
    col = pl.BlockSpec((None, N_RET_HEADS, HEAD_DIM, 1), lambda b: (b, 0, 0, 0))
    rowv = pl.BlockSpec((None, N_RET_HEADS, 1, HEAD_DIM), lambda b: (b, 0, 0, 0))
    st = pl.BlockSpec((None, N_RET_HEADS, HEAD_DIM, HEAD_DIM), lambda b: (b, 0, 0, 0))
    out, new_state = pl.pallas_call(
        _ret_sample_body,
        grid=(n,),
        in_specs=[st, col, col, rowv],
        out_specs=[rowv, st],
        out_shape=[jax.ShapeDtypeStruct((n, N_RET_HEADS, 1, HEAD_DIM), F32),
                   jax.ShapeDtypeStruct(state.shape, F32)],
        compiler_params=_cparams(("parallel",)),
        name="ret_sample",
    )(state, rq.reshape(n, N_RET_HEADS, HEAD_DIM, 1), rk.reshape(n, N_RET_HEADS, HEAD_DIM, 1),
      rv.reshape(n, N_RET_HEADS, 1, HEAD_DIM))
    return out.reshape(n, RET_W), new_state


def _rope_tables(pos):
    half = HEAD_DIM // 2
    inv = ROPE_BASE ** (-jnp.arange(half, dtype=F32) / half)
    ang = pos.astype(F32)[:, None] * inv[None, :]
    cos, sin = jnp.cos(ang), jnp.sin(ang)
    cos = jnp.tile(jnp.concatenate([cos, cos], axis=1), (1, N_RET_HEADS))
    sin = jnp.tile(jnp.concatenate([-sin, sin], axis=1), (1, N_RET_HEADS))
    return cos, sin


def _block_diag_ones(width):
    r = np.arange(width)[:, None] // HEAD_DIM
    c = np.arange(width)[None, :] // HEAD_DIM
    return jnp.asarray((r == c).astype(np.float32), dtype=BF16)


def _repack_w_in(w):
    o = np.cumsum([0, ATT_W, ATT_W, ATT_W, IDXQ_W, IDX_DIM, N_IDX_HEADS, RET_W, RET_W, RET_W, RET_W, MEM_W])
    pad = jnp.zeros((w.shape[0], KW_W - IDX_DIM - N_IDX_HEADS), w.dtype)
    return jnp.concatenate([w[:, o[0]:o[4]], w[:, o[4]:o[6]], pad, w[:, o[6]:o[11]]], axis=1).astype(BF16)


def kernel(x_prompt, x_sample, mem_prompt, cache_k, cache_v, cache_kidx, state_ret, cache_mem_k, cache_mem_v,
           page_table, ffn1_norm_g, ffn1_w_gu, ffn1_w_down, mix_norm_g, w_in, att_q_norm_g, att_k_norm_g,
           ret_gn_g, mem_norm_g, w_mem_kv, mem_q_norm_g, mem_k_norm_g, w_out, ffn2_norm_g, ffn2_w_gu, ffn2_w_down):
    n_b, s_len, d = x_prompt.shape
    n_s, t_s, _ = x_sample.shape
    assert t_s == 1, "the sample group decodes one token per sequence"
    depth = w_in.shape[0]
    n_mem = mem_prompt.shape[1]
    n_pool = cache_k.shape[1]
    past_len = page_table.shape[1] * PAGE_SIZE
    page_table = page_table.astype(I32)

    cos_p, sin_p = _rope_tables(jnp.arange(s_len, dtype=I32))
    cos_s, sin_s = _rope_tables(jnp.full((n_s,), past_len, I32))
    bd = _block_diag_ones(ATT_W)
    bd_ret = _block_diag_ones(RET_W)
    tile8 = lambda g_: jnp.tile(g_, N_ATT_HEADS).reshape(1, ATT_W)
    tile4 = lambda g_: jnp.tile(g_, N_MEM_HEADS).reshape(1, MEM_W)

    yp = x_prompt.reshape(n_b * s_len, d)
    ys = x_sample.reshape(n_s, d)
    mem = mem_prompt.reshape(n_b * n_mem, d)
    outs = [[] for _ in range(10)]
    for l in range(depth):
        g1 = ffn1_norm_g[l].reshape(1, d)
        g2 = ffn2_norm_g[l].reshape(1, d)
        gmix = mix_norm_g[l].reshape(1, d)
        w1gu, w1d = ffn1_w_gu[l].astype(BF16), ffn1_w_down[l].astype(BF16)
        w2gu, w2d = ffn2_w_gu[l].astype(BF16), ffn2_w_down[l].astype(BF16)
        w_in_l = _repack_w_in(w_in[l])
        w_out_l = w_out[l].astype(BF16)
        gq, gk, gm = tile8(att_q_norm_g[l]), tile8(att_k_norm_g[l]), tile4(mem_q_norm_g[l])
        gn = ret_gn_g[l].reshape(1, RET_W)

        yp = _ffn(yp, g1, w1gu, w1d)
        (q, k, kb, v, vt, qi, kw, ki, ki2, rq, rk, rv, rg, mq) = _proj(
            yp, gmix, w_in_l, gq, gk, gm, cos_p, sin_p, bd, n_b)
        mk, mv = _memkv(mem, mem_norm_g[l].reshape(1, d), w_mem_kv[l].astype(BF16), tile4(mem_k_norm_g[l]), bd_ret)
        att = _dsa_prompt(q, qi, kw, kb, vt, ki2, n_b, s_len)
        ret, s_fin = _ret_prompt(rq, rk, rv, n_b, s_len)
        mo = _mematt_prompt(mq, mk, mv, n_b)
        yp = _mixout(yp, att, ret, rg, mo, gn, bd_ret, w_out_l)
        yp = _ffn(yp, g2, w2gu, w2d)
        outs[0].append(k.reshape(n_b, s_len, N_ATT_HEADS, HEAD_DIM))
        outs[1].append(v.reshape(n_b, s_len, N_ATT_HEADS, HEAD_DIM))
        outs[2].append(ki.reshape(n_b, s_len, IDX_DIM))
        outs[3].append(s_fin)
        outs[4].append(mk.reshape(n_b, n_mem, N_MEM_HEADS, HEAD_DIM))
        outs[5].append(mv.reshape(n_b, n_mem, N_MEM_HEADS, HEAD_DIM))

        ys = _ffn(ys, g1, w1gu, w1d)
        (q, k, kb, v, vt, qi, kw, ki, ki2, rq, rk, rv, rg, mq) = _proj(
            ys, gmix, w_in_l, gq, gk, gm, cos_s, sin_s, bd, 1)
        wi = kw[:, IDX_DIM:IDX_DIM + N_IDX_HEADS]
        sc_past, sc_self = _sample_scores(page_table, qi, wi, ki, jnp.transpose(cache_kidx[l], (0, 2, 1)))
        topk = min(TOPK_MAX, (past_len + t_s) // 4)
        thr, cut = _sample_select(sc_past.reshape(n_s, past_len), sc_self.reshape(n_s, LANES), topk)
        k_t = jnp.transpose(cache_k[l], (0, 2, 3, 1)).reshape(n_pool, ATT_W, PAGE_SIZE)
        v_t = jnp.transpose(cache_v[l], (0, 2, 3, 1)).reshape(n_pool, ATT_W, PAGE_SIZE)
        att = _sample_attend(page_table, thr[:, 0], cut[:, 0], q, k, v, sc_past, sc_self, k_t, v_t)
        ret, s_new = _ret_sample(state_ret[l].astype(F32), rq, rk, rv)
        mo = _mematt_sample(mq, cache_mem_k[l].reshape(n_s, n_mem, MEM_W), cache_mem_v[l].reshape(n_s, n_mem, MEM_W))
        ys = _mixout(ys, att, ret, rg, mo, gn, bd_ret, w_out_l)
        ys = _ffn(ys, g2, w2gu, w2d)
        outs[6].append(k.reshape(n_s, t_s, N_ATT_HEADS, HEAD_DIM))
        outs[7].append(v.reshape(n_s, t_s, N_ATT_HEADS, HEAD_DIM))
        outs[8].append(ki.reshape(n_s, t_s, IDX_DIM))
        outs[9].append(s_new)

    return (yp.reshape(n_b, s_len, d), ys.reshape(n_s, t_s, d)) + tuple(jnp.stack(o) for o in outs)
```

```python
import functools
import math

import numpy as np
import jax
import jax.numpy as jnp
from jax import lax
from jax.experimental import pallas as pl
from jax.experimental.pallas import tpu as pltpu

F32 = jnp.float32
BF16 = jnp.bfloat16
I32 = jnp.int32

HEAD_DIM = 64
N_ATT_HEADS = 8
N_IDX_HEADS = 8
IDX_DIM = 64
TOPK_MAX = 256
N_RET_HEADS = 4
N_MEM_HEADS = 4
PAGE_SIZE = 128
EPS = 1e-6
ROPE_BASE = 10000.0

ATT_W = N_ATT_HEADS * HEAD_DIM
RET_W = N_RET_HEADS * HEAD_DIM
MEM_W = N_MEM_HEADS * HEAD_DIM
IDXQ_W = N_IDX_HEADS * IDX_DIM
LANES = 128
KW_W = LANES
C_Q = 0
C_K = C_Q + ATT_W
C_V = C_K + ATT_W
C_QI = C_V + ATT_W
C_KW = C_QI + IDXQ_W
C_RQ = C_KW + KW_W
C_RK = C_RQ + RET_W
C_RV = C_RK + RET_W
C_RG = C_RV + RET_W
C_MQ = C_RG + RET_W
PROJ_W = C_MQ + MEM_W

QK_SCALE = HEAD_DIM ** -0.5
IDX_SCALE = IDX_DIM ** -0.5
IDX_HEAD_SCALE = N_IDX_HEADS ** -0.5
NEG_BIG = -1e30
M_INIT = -(2.0 ** 100)
INT_MIN = -(2 ** 31)
F32_LOWEST = float(np.finfo(np.float32).min)
LOG_G = [float(np.log1p(np.float32(-(2.0 ** (-5.0 - h))))) for h in range(N_RET_HEADS)]

VMEM_LIMIT = 56 * 1024 * 1024
ROW_TILE = 512
KEY_CHUNK = 512
Q_TILE = LANES
VT_ROWS = LANES + 16
BISECT_GROUP = 4
RET_CHUNK = 128
PAGES_PER_STEP = 16


def _cparams(sem):
    return pltpu.CompilerParams(dimension_semantics=sem, vmem_limit_bytes=VMEM_LIMIT)


def _resident(shape):
    nd = len(shape)
    return pl.BlockSpec(shape, lambda *_: (0,) * nd, pipeline_mode=pl.Buffered(1))


def _rms(x, g):
    return x * lax.rsqrt(jnp.mean(x * x, axis=-1, keepdims=True) + EPS) * g


def _head_rms(z, g, bd):
    z2 = z * z
    hi = z2.astype(BF16)
    lo = (z2 - hi.astype(F32)).astype(BF16)
    ss = jnp.dot(hi, bd, preferred_element_type=F32) + jnp.dot(lo, bd, preferred_element_type=F32)
    return z * lax.rsqrt(ss * (1.0 / HEAD_DIM) + EPS) * g


def _nt(a, b):
    return lax.dot_general(a, b, (((1,), (1,)), ((), ())), preferred_element_type=F32)


def _from_key(k):
    return lax.bitcast_convert_type(jnp.where(k < 0, k ^ jnp.int32(0x7FFFFFFF), k), F32)


def _finite_threshold(thr_key):
    return jnp.where(thr_key == INT_MIN, F32_LOWEST, jnp.maximum(_from_key(thr_key), F32_LOWEST))


def _ffn_body(x_ref, g_ref, wgu_ref, wd_ref, o_ref, *, d_ff, fc):
    x = x_ref[...]
    h = _rms(x, g_ref[...]).astype(BF16)
    acc = jnp.zeros(x.shape, F32)
    for c in range(d_ff // fc):
        gate = jnp.dot(h, wgu_ref[:, c * fc:(c + 1) * fc], preferred_element_type=F32)
        up = jnp.dot(h, wgu_ref[:, d_ff + c * fc:d_ff + (c + 1) * fc], preferred_element_type=F32)
        act = (gate * jax.nn.sigmoid(gate) * up).astype(BF16)
        acc = acc + jnp.dot(act, wd_ref[c * fc:(c + 1) * fc, :], preferred_element_type=F32)
    o_ref[...] = x + 0.5 * acc


def _ffn(x, g, wgu, wd):
    t, d = x.shape
    d_ff = wd.shape[0]
    tm = min(ROW_TILE, t)
    fc = d_ff // 2 if (d_ff // 2) % LANES == 0 else d_ff
    return pl.pallas_call(
        functools.partial(_ffn_body, d_ff=d_ff, fc=fc),
        grid=(t // tm,),
        in_specs=[pl.BlockSpec((tm, d), lambda i: (i, 0)), _resident((1, d)),
                  _resident(wgu.shape), _resident(wd.shape)],
        out_specs=pl.BlockSpec((tm, d), lambda i: (i, 0)),
        out_shape=jax.ShapeDtypeStruct((t, d), F32),
        compiler_params=_cparams(("parallel",)),
        name="ffn_half",
    )(x, g, wgu, wd)


def _proj_body(x_ref, g_ref, w_ref, gq_ref, gk_ref, gm_ref, cos_ref, sin_ref, bd_ref,
               q_ref, k_ref, kb_ref, v_ref, vt_ref, qi_ref, kw_ref, ki_ref, ki2_ref,
               rq_ref, rk_ref, rv_ref, rg_ref, mq_ref):
    h = _rms(x_ref[...], g_ref[...]).astype(BF16)

    def proj(a, width):
        return jnp.dot(h, w_ref[:, a:a + width], preferred_element_type=F32)

    bd = bd_ref[...]
    q = _head_rms(proj(C_Q, ATT_W), gq_ref[...], bd)
    q_ref[...] = (q * QK_SCALE).astype(BF16)
    k = _head_rms(proj(C_K, ATT_W), gk_ref[...], bd)
    k_ref[...] = k
    kb_ref[...] = k.astype(BF16)
    v = proj(C_V, ATT_W)
    v_ref[...] = v
    vt = jnp.transpose(v)
    tail_row = lax.broadcasted_iota(I32, (VT_ROWS - LANES, vt.shape[1]), 0)
    tail = jnp.where(tail_row == 0, 1.0, 0.0)
    for p in range(N_ATT_HEADS // 2):
        vt_ref[p] = jnp.concatenate([vt[p * LANES:(p + 1) * LANES, :], tail], axis=0).astype(BF16)
    qi_ref[...] = proj(C_QI, IDXQ_W).astype(BF16)
    kw = proj(C_KW, KW_W)
    kw_ref[...] = kw
    ki = kw[:, :IDX_DIM]
    ki_ref[...] = ki
    kib = ki.astype(BF16)
    ki2_ref[...] = jnp.concatenate([kib, kib], axis=1)

    lane = lax.broadcasted_iota(I32, (1, RET_W), 1)
    first_half = (lane % HEAD_DIM) < (HEAD_DIM // 2)
    cos = cos_ref[...]
    sin = sin_ref[...]

    def rot(x):
        swapped = jnp.where(first_half, pltpu.roll(x, RET_W - HEAD_DIM // 2, axis=1),
                            pltpu.roll(x, HEAD_DIM // 2, axis=1))
        return x * cos + swapped * sin

    rq_ref[...] = rot(proj(C_RQ, RET_W))
    rk_ref[...] = rot(proj(C_RK, RET_W)) * QK_SCALE
    rv_ref[...] = proj(C_RV, RET_W)
    rg_ref[...] = proj(C_RG, RET_W)
    mq = _head_rms(proj(C_MQ, MEM_W), gm_ref[...], bd[:MEM_W, :MEM_W])
    mq_ref[...] = (mq * QK_SCALE).astype(BF16)


def _proj(y, g, w, gq, gk, gm, cos, sin, bd, n_batch):
    t, d = y.shape
    s_len = t // n_batch
    tm = min(ROW_TILE, s_len)
    nsb = s_len // tm
    row = lambda w_: pl.BlockSpec((tm, w_), lambda i: (i, 0))
    tab = pl.BlockSpec((tm, RET_W), lambda i: (i % nsb, 0))
    out_shapes = [
        (jax.ShapeDtypeStruct((t, ATT_W), BF16), row(ATT_W)),
        (jax.ShapeDtypeStruct((t, ATT_W), F32), row(ATT_W)),
        (jax.ShapeDtypeStruct((t, ATT_W), BF16), row(ATT_W)),
        (jax.ShapeDtypeStruct((t, ATT_W), F32), row(ATT_W)),
        (jax.ShapeDtypeStruct((n_batch, nsb, N_ATT_HEADS // 2, VT_ROWS, tm), BF16),
         pl.BlockSpec((None, None, N_ATT_HEADS // 2, VT_ROWS, tm),
                      lambda i: (i // nsb, i % nsb, 0, 0, 0))),
        (jax.ShapeDtypeStruct((t, IDXQ_W), BF16), row(IDXQ_W)),
        (jax.ShapeDtypeStruct((t, KW_W), F32), row(KW_W)),
        (jax.ShapeDtypeStruct((t, IDX_DIM), F32), row(IDX_DIM)),
        (jax.ShapeDtypeStruct((t, 2 * IDX_DIM), BF16), row(2 * IDX_DIM)),
        (jax.ShapeDtypeStruct((t, RET_W), F32), row(RET_W)),
        (jax.ShapeDtypeStruct((t, RET_W), F32), row(RET_W)),
        (jax.ShapeDtypeStruct((t, RET_W), F32), row(RET_W)),
        (jax.ShapeDtypeStruct((t, RET_W), F32), row(RET_W)),
        (jax.ShapeDtypeStruct((t, MEM_W), BF16), row(MEM_W)),
    ]
    return pl.pallas_call(
        _proj_body,
        grid=(t // tm,),
        in_specs=[row(d), _resident((1, d)), _resident(w.shape), _resident((1, ATT_W)),
                  _resident((1, ATT_W)), _resident((1, MEM_W)), tab, tab, _resident(bd.shape)],
        out_specs=[s for _, s in out_shapes],
        out_shape=[s for s, _ in out_shapes],
        compiler_params=_cparams(("parallel",)),
        name="mix_proj",
    )(y, g, w, gq, gk, gm, cos, sin, bd)


def _memkv_body(x_ref, g_ref, w_ref, gk_ref, bd_ref, mk_ref, mv_ref):
    h = _rms(x_ref[...], g_ref[...]).astype(BF16)
    z = jnp.dot(h, w_ref[...], preferred_element_type=F32)
    mk_ref[...] = _head_rms(z[:, :MEM_W], gk_ref[...], bd_ref[...])
    mv_ref[...] = z[:, MEM_W:]


def _memkv(mem, g, w, gk, bd):
    t, d = mem.shape
    tm = min(ROW_TILE, t)
    row = lambda w_: pl.BlockSpec((tm, w_), lambda i: (i, 0))
    return pl.pallas_call(
        _memkv_body,
        grid=(t // tm,),
        in_specs=[row(d), _resident((1, d)), _resident(w.shape), _resident((1, MEM_W)), _resident(bd.shape)],
        out_specs=[row(MEM_W), row(MEM_W)],
        out_shape=[jax.ShapeDtypeStruct((t, MEM_W), F32)] * 2,
        compiler_params=_cparams(("parallel",)),
        name="mem_kv",
    )(mem, g, w, gk, bd)


def _dsa_pair_body(q_ref, qi_ref, kw_ref, k_ref, vt_ref, ki2_ref, o_ref,
                   key_scr, qi2_scr, q2_scr, cut_scr, m_scr, acc_scr, sa_scr, sb_scr, *, topk, sc, s_len):
    n_st = 2
    jj = pl.program_id(1)
    n_ch = ((n_st * jj + 1) * Q_TILE + Q_TILE + sc - 1) // sc
    n_pair = N_ATT_HEADS // 2
    lane = lax.broadcasted_iota(I32, (1, LANES), 1)
    qpos = [(n_st * jj + s) * Q_TILE + lane for s in range(n_st)]
    kiota = lax.broadcasted_iota(I32, (sc, LANES), 0)
    streams = range(n_st)

    r2 = lax.broadcasted_iota(I32, (2 * Q_TILE, LANES), 0)
    c2 = lax.broadcasted_iota(I32, (2 * Q_TILE, LANES), 1)
    keep = (r2 < Q_TILE) == (c2 < HEAD_DIM)
    eye2 = jnp.where(r2 % Q_TILE == c2, 1.0, 0.0).astype(BF16)
    w_s = []
    for s in streams:
        rows = slice(s * Q_TILE, (s + 1) * Q_TILE)
        for p in range(n_pair):
            a = qi_ref[rows, p * LANES:(p + 1) * LANES].astype(F32)
            qi2_scr[s, p] = jnp.where(keep, jnp.concatenate([a, a], axis=0), 0.0).astype(BF16)
            a = q_ref[rows, p * LANES:(p + 1) * LANES].astype(F32)
            q2 = jnp.where(keep, jnp.concatenate([a, a], axis=0), 0.0).astype(BF16)
            q2_scr[s, p] = jnp.concatenate([q2, eye2], axis=1)
        w_t = jnp.transpose(kw_ref[rows, :])[IDX_DIM:IDX_DIM + N_IDX_HEADS, :]
        w_s.append((w_t * IDX_HEAD_SCALE) * IDX_SCALE)

    def score_chunk(c, carry):
        off = pl.multiple_of(c * sc, sc)
        kc = ki2_ref[pl.ds(off, sc), :]
        acc = [jnp.zeros((sc, LANES), F32) for _ in streams]
        for p in range(n_pair):
            for s in streams:
                lg = _nt(kc, qi2_scr[s, p])
                acc[s] = acc[s] + jnp.maximum(lg[:, :LANES], 0.0) * w_s[s][2 * p:2 * p + 1, :]
                acc[s] = acc[s] + jnp.maximum(lg[:, LANES:], 0.0) * w_s[s][2 * p + 1:2 * p + 2, :]
        for s in streams:
            key_scr[s, pl.ds(off, sc), :] = jnp.where(off + kiota <= qpos[s], acc[s], -jnp.inf)
        return carry

    lax.fori_loop(0, n_ch, score_chunk, 0)

    def count(preds):
        def body(c, accs):
            off = pl.multiple_of(c * sc, sc)
            accs = list(accs)
            for g in range(sc // 64):
                rows = pl.ds(off + g * 64, 64)
                for s in streams:
                    hit = preds[s](key_scr[s, rows, :], off + kiota[g * 64:(g + 1) * 64])
                    accs[s] = jnp.where(hit, accs[s] + 1.0, accs[s])
            return tuple(accs)
        accs = lax.fori_loop(0, n_ch, body, tuple(jnp.zeros((64, LANES), F32) for _ in streams))
        return [a.reshape(8, 8, LANES).sum(axis=0).sum(axis=0, keepdims=True) for a in accs]

    def bisect_group(state):
        i, thrs, actives, _ = state
        thrs, actives = list(thrs), list(actives)
        for _ in range(BISECT_GROUP):
            cands = [t + lax.shift_left(jnp.int32(1), 31 - i) for t in thrs]
            cnts = count([functools.partial(lambda x, _, cand: x >= cand, cand=_from_key(cand)) for cand in cands])
            for s in streams:
                thrs[s] = jnp.where((cnts[s] >= topk) & (actives[s] > 0.0), cands[s], thrs[s])
                actives[s] = jnp.where(cnts[s] == topk, 0.0, actives[s])
            i = i + 1
        return i, tuple(thrs), tuple(actives), jnp.max(jnp.maximum(actives[0], actives[1]))

    actives0 = tuple(jnp.where(qpos[s] + 1 <= topk, 0.0, 1.0) for s in streams)
    _, thrs, actives, n_active = lax.while_loop(
        lambda st: (st[0] < 32) & (st[3] > 0.0), bisect_group,
        (jnp.int32(0), tuple(jnp.full((1, LANES), INT_MIN, I32) for _ in streams), actives0,
         jnp.max(jnp.maximum(actives0[0], actives0[1]))))
    for s in streams:
        cut_scr[s] = jnp.full((1, LANES), s_len, I32)

    thr_f = [_finite_threshold(t) for t in thrs]

    @pl.when(n_active > 0.0)
    def _():
        c_gt = count([functools.partial(lambda x, _, t: x > t, t=t) for t in thr_f])
        c_ge = count([functools.partial(lambda x, _, t: x >= t, t=t) for t in thr_f])
        need = [topk - c for c in c_gt]
        tie = [((c_ge[s] - c_gt[s]) > need[s]) & (actives[s] > 0.0) for s in streams]
        nbits = max(1, (s_len - 1).bit_length())

        def bisect_pos(i, los):
            cands = [lo + lax.shift_left(jnp.int32(1), nbits - 1 - i) for lo in los]
            cnts = count([functools.partial(lambda x, kpos, t, cand: jnp.where(x == t, kpos, s_len) < cand,
                                            t=thr_f[s], cand=cands[s]) for s in streams])
            return tuple(jnp.where(cnts[s] < need[s], cands[s], los[s]) for s in streams)

        los = lax.fori_loop(0, nbits, bisect_pos, tuple(jnp.zeros((1, LANES), I32) for _ in streams))
        for s in streams:
            cut_scr[s] = jnp.where(tie[s], los[s], s_len)

    cuts = [cut_scr[s] for s in streams]

    m_scr[...] = jnp.full(m_scr.shape, M_INIT, F32)
    acc_scr[...] = jnp.zeros(acc_scr.shape, F32)

    def qk_chunk(c, s_scr):
        off = pl.multiple_of(c * sc, sc)
        biases = []
        for s in streams:
            bias = _select_bias(key_scr[s, pl.ds(off, sc), :], off + kiota, thr_f[s], cuts[s])
            biases.append(bias.astype(BF16))
        for p in range(n_pair):
            kp = k_ref[pl.ds(off, sc), p * LANES:(p + 1) * LANES]
            for s in streams:
                s_scr[s, p] = _nt(jnp.concatenate([kp, biases[s]], axis=1), q2_scr[s, p])

    def softmax_pv_chunk(c, s_scr):
        es, alphas = {}, {}
        for p in range(n_pair):
            for s in streams:
                x = s_scr[s, p].astype(BF16)
                m_c = x.reshape(sc // 64, 64, 2 * Q_TILE).max(axis=0)
                m_c = m_c.reshape(4, 16, 2 * Q_TILE).max(axis=0).astype(F32).max(axis=0, keepdims=True)
                m_old = m_scr[s, p]
                m_new = jnp.maximum(m_old, m_c)
                alphas[s, p] = jnp.exp(m_old - m_new)
                es[s, p] = jnp.exp(x - m_new.astype(BF16))
                m_scr[s, p] = m_new
            for s in streams:
                pv = jnp.dot(vt_ref[c, p], es[s, p], preferred_element_type=F32)
                acc_scr[s, p] = acc_scr[s, p] * alphas[s, p] + pv

    def att_two_chunks(i, carry):
        c = 2 * i
        qk_chunk(c + 1, sb_scr)
        softmax_pv_chunk(c, sa_scr)
        qk_chunk(c + 2, sa_scr)
        softmax_pv_chunk(c + 1, sb_scr)
        return carry

    qk_chunk(0, sa_scr)
    n_two = (n_ch - 1) // 2
    lax.fori_loop(0, n_two, att_two_chunks, 0)

    @pl.when(n_ch % 2 == 0)
    def _():
        qk_chunk(n_ch - 1, sb_scr)
        softmax_pv_chunk(n_ch - 2, sa_scr)
        softmax_pv_chunk(n_ch - 1, sb_scr)

    @pl.when(n_ch % 2 == 1)
    def _():
        softmax_pv_chunk(n_ch - 1, sa_scr)

    for s in streams:
        outs = []
        for p in range(n_pair):
            a = acc_scr[s, p]
            l = a[LANES:LANES + 1, :]
            outs.append(a[:HEAD_DIM, :LANES] / l[:, :LANES])
            outs.append(a[HEAD_DIM:LANES, LANES:] / l[:, LANES:])
        o_ref[s * Q_TILE:(s + 1) * Q_TILE, :] = jnp.transpose(jnp.concatenate(outs, axis=0))


def _dsa_prompt_pairs(q, qi, kw, kb, vt, ki2, n_batch, s_len):
    t = q.shape[0]
    sc = min(KEY_CHUNK, s_len)
    n_st = 2
    nq = s_len // (n_st * Q_TILE)
    n_pair = N_ATT_HEADS // 2
    topk = min(TOPK_MAX, s_len // 4)
    qrow = lambda w_: pl.BlockSpec((n_st * Q_TILE, w_), lambda b, j: (b * nq + j, 0))
    per_batch = lambda w_: pl.BlockSpec((s_len, w_), lambda b, j: (b, 0), pipeline_mode=pl.Buffered(1))
    return pl.pallas_call(
        functools.partial(_dsa_pair_body, topk=topk, sc=sc, s_len=s_len),
        grid=(n_batch, nq),
        in_specs=[qrow(ATT_W), qrow(IDXQ_W), qrow(KW_W), per_batch(ATT_W),
                  pl.BlockSpec((None, s_len // sc, n_pair, VT_ROWS, sc), lambda b, j: (b, 0, 0, 0, 0),
                               pipeline_mode=pl.Buffered(1)),
                  per_batch(2 * IDX_DIM)],
        out_specs=qrow(ATT_W),
        out_shape=jax.ShapeDtypeStruct((t, ATT_W), F32),
        scratch_shapes=[pltpu.VMEM((n_st, s_len, LANES), F32),
                        pltpu.VMEM((n_st, n_pair, 2 * Q_TILE, LANES), BF16),
                        pltpu.VMEM((n_st, n_pair, 2 * Q_TILE, 2 * LANES), BF16),
                        pltpu.VMEM((n_st, 1, LANES), I32),
                        pltpu.VMEM((n_st, n_pair, 1, 2 * Q_TILE), F32),
                        pltpu.VMEM((n_st, n_pair, VT_ROWS, 2 * Q_TILE), F32),
                        pltpu.VMEM((n_st, n_pair, sc, 2 * Q_TILE), F32),
                        pltpu.VMEM((n_st, n_pair, sc, 2 * Q_TILE), F32)],
        compiler_params=_cparams(("parallel", "arbitrary")),
        name="dsa_prompt",
    )(q, qi, kw, kb, vt, ki2)


def _ret_lane_const(vals):
    lane = lax.broadcasted_iota(I32, (1, RET_W), 1)
    out = jnp.zeros((1, RET_W), F32)
    for h, v in enumerate(vals):
        out = jnp.where(lane // HEAD_DIM == h, v, out)
    return out


def _ret_body(rq_ref, rk_ref, rv_ref, o_ref, st_ref, sbd_scr, *, ch):
    c = pl.program_id(1)

    @pl.when(c == 0)
    def _():
        sbd_scr[...] = jnp.zeros(sbd_scr.shape, F32)

    q = rq_ref[...]
    k = rk_ref[...]
    vb = rv_ref[...].astype(BF16)
    qb = q.astype(BF16)
    kb = k.astype(BF16)
    head = lax.broadcasted_iota(I32, (1, RET_W), 1) // HEAD_DIM
    log_g = _ret_lane_const(LOG_G)
    i_col = lax.broadcasted_iota(I32, (ch, 1), 0).astype(F32)
    state = sbd_scr[...]

    cross = jnp.dot(qb, state.astype(BF16), preferred_element_type=F32) * jnp.exp(log_g * (i_col + 1.0))
    ii = lax.broadcasted_iota(I32, (ch, ch), 0)
    jj = lax.broadcasted_iota(I32, (ch, ch), 1)
    causal = ii >= jj
    diff = jnp.where(causal, ii - jj, 0).astype(F32)
    inner = jnp.zeros((ch, RET_W), F32)
    for h in range(N_RET_HEADS):
        qm = jnp.where(head == h, q, 0.0).astype(BF16)
        decay = jnp.where(causal, jnp.exp(LOG_G[h] * diff), 0.0)
        sc = (_nt(qm, kb) * decay).astype(BF16)
        inner = inner + jnp.where(head == h, jnp.dot(sc, vb, preferred_element_type=F32), 0.0)
    o_ref[...] = inner + cross

    kd = (k * jnp.exp(log_g * (ch - 1.0 - i_col))).astype(BF16)
    kv = lax.dot_general(kd, vb, (((0,), (0,)), ((), ())), preferred_element_type=F32)
    rh = lax.broadcasted_iota(I32, (RET_W, RET_W), 0) // HEAD_DIM
    ch_ = lax.broadcasted_iota(I32, (RET_W, RET_W), 1) // HEAD_DIM
    new_state = state * jnp.exp(log_g * float(ch)) + jnp.where(rh == ch_, kv, 0.0)
    sbd_scr[...] = new_state

    @pl.when(c == pl.num_programs(1) - 1)
    def _():
        for h in range(N_RET_HEADS):
            st_ref[h] = new_state[h * HEAD_DIM:(h + 1) * HEAD_DIM, h * HEAD_DIM:(h + 1) * HEAD_DIM]


def _ret_prompt(rq, rk, rv, n_batch, s_len):
    t = rq.shape[0]
    ch = min(RET_CHUNK, s_len)
    nc = s_len // ch
    blk = pl.BlockSpec((ch, RET_W), lambda b, c: (b * nc + c, 0))
    return pl.pallas_call(
        functools.partial(_ret_body, ch=ch),
        grid=(n_batch, nc),
        in_specs=[blk, blk, blk],
        out_specs=[blk, pl.BlockSpec((None, N_RET_HEADS, HEAD_DIM, HEAD_DIM), lambda b, c: (b, 0, 0, 0))],
        out_shape=[jax.ShapeDtypeStruct((t, RET_W), F32),
                   jax.ShapeDtypeStruct((n_batch, N_RET_HEADS, HEAD_DIM, HEAD_DIM), F32)],
        scratch_shapes=[pltpu.VMEM((RET_W, RET_W), F32)],
        compiler_params=_cparams(("parallel", "arbitrary")),
        name="ret_prompt",
    )(rq, rk, rv)


def _softmax_lanes(s):
    m = jnp.max(s, axis=-1, keepdims=True)
    e = jnp.exp(s - m)
    return e / jnp.sum(e, axis=-1, keepdims=True)


def _mematt_body(mq_ref, mk_ref, mv_ref, o_ref):
    mq = mq_ref[...].astype(F32)
    mk = mk_ref[...].astype(BF16)
    mv = mv_ref[...].astype(BF16)
    head = lax.broadcasted_iota(I32, (1, MEM_W), 1) // HEAD_DIM
    out = jnp.zeros(mq.shape, F32)
    for h in range(N_MEM_HEADS):
        qm = jnp.where(head == h, mq, 0.0).astype(BF16)
        p = _softmax_lanes(_nt(qm, mk)).astype(BF16)
        out = out + jnp.where(head == h, jnp.dot(p, mv, preferred_element_type=F32), 0.0)
    o_ref[...] = out


def _mematt_prompt(mq, mk, mv, n_batch):
    t = mq.shape[0]
    n_mem = mk.shape[0] // n_batch
    s_len = t // n_batch
    tm = min(ROW_TILE, s_len)
    nsb = s_len // tm
    row = pl.BlockSpec((tm, MEM_W), lambda i: (i, 0))
    mem = pl.BlockSpec((n_mem, MEM_W), lambda i: (i // nsb, 0))
    return pl.pallas_call(
        _mematt_body,
        grid=(t // tm,),
        in_specs=[row, mem, mem],
        out_specs=row,
        out_shape=jax.ShapeDtypeStruct((t, MEM_W), F32),
        compiler_params=_cparams(("parallel",)),
        name="mem_attend",
    )(mq, mk, mv)


def _mematt_sample_body(mq_ref, mk_ref, mv_ref, o_ref):
    rows = 8
    mq = jnp.broadcast_to(mq_ref[...].astype(F32), (rows, MEM_W))
    sel = lax.broadcasted_iota(I32, (rows, MEM_W), 1) // HEAD_DIM == lax.broadcasted_iota(I32, (rows, MEM_W), 0)
    qbd = jnp.where(sel, mq, 0.0).astype(BF16)
    p = _softmax_lanes(_nt(qbd, mk_ref[...].astype(BF16))).astype(BF16)
    o = jnp.dot(p, mv_ref[...].astype(BF16), preferred_element_type=F32)
    o_ref[...] = jnp.sum(jnp.where(sel, o, 0.0), axis=0, keepdims=True)


def _mematt_sample(mq, mk, mv):
    n, n_mem = mk.shape[0], mk.shape[1]
    one = pl.BlockSpec((None, 1, MEM_W), lambda b: (b, 0, 0))
    mem = pl.BlockSpec((None, n_mem, MEM_W), lambda b: (b, 0, 0))
    out = pl.pallas_call(
        _mematt_sample_body,
        grid=(n,),
        in_specs=[one, mem, mem],
        out_specs=one,
        out_shape=jax.ShapeDtypeStruct((n, 1, MEM_W), F32),
        compiler_params=_cparams(("parallel",)),
        name="mem_attend_sample",
    )(mq.reshape(n, 1, MEM_W), mk, mv)
    return out.reshape(n, MEM_W)


def _mixout_body(y_ref, att_ref, ret_ref, rg_ref, mo_ref, gn_ref, bd_ref, w_ref, o_ref):
    retn = _head_rms(ret_ref[...], gn_ref[...], bd_ref[...])
    rg = rg_ref[...]
    gated = (rg * jax.nn.sigmoid(rg)) * retn
    acc = jnp.dot(att_ref[...].astype(BF16), w_ref[:ATT_W, :], preferred_element_type=F32)
    acc = acc + jnp.dot(gated.astype(BF16), w_ref[ATT_W:ATT_W + RET_W, :], preferred_element_type=F32)
    acc = acc + jnp.dot(mo_ref[...].astype(BF16), w_ref[ATT_W + RET_W:, :], preferred_element_type=F32)
    o_ref[...] = y_ref[...] + acc


def _mixout(y, att, ret, rg, mo, gn, bd, w):
    t, d = y.shape
    tm = min(ROW_TILE, t)
    row = lambda w_: pl.BlockSpec((tm, w_), lambda i: (i, 0))
    return pl.pallas_call(
        _mixout_body,
        grid=(t // tm,),
        in_specs=[row(d), row(ATT_W), row(RET_W), row(RET_W), row(MEM_W), _resident((1, RET_W)),
                  _resident(bd.shape), _resident(w.shape)],
        out_specs=row(d),
        out_shape=jax.ShapeDtypeStruct((t, d), F32),
        compiler_params=_cparams(("parallel",)),
        name="mix_out",
    )(y, att, ret, rg, mo, gn, bd, w)


def _sample_score_body(pt_ref, qi_ref, w_ref, kin_ref, *rest, pps):
    page_refs, (past_ref, self_ref) = rest[:pps], rest[pps:]
    q8 = qi_ref[...]
    w8 = (w_ref[...] * IDX_HEAD_SCALE) * IDX_SCALE

    def head_sum(lg):
        s = jnp.sum(jnp.maximum(lg, 0.0) * w8, axis=0, keepdims=True)
        return jnp.where(s == 0.0, 0.0, s)

    for i in range(pps):
        lg = jnp.dot(q8, page_refs[i][...].astype(BF16), preferred_element_type=F32)
        past_ref[i:i + 1, :] = head_sum(lg)

    @pl.when(pl.program_id(1) == 0)
    def _():
        kn = kin_ref[...].astype(BF16).astype(F32)
        lg = jnp.sum(q8.astype(F32) * kn, axis=1, keepdims=True)
        self_ref[...] = jnp.broadcast_to(head_sum(lg), self_ref.shape)


def _sample_scores(page_table, qi, wi, ki_new, kidx_t):
    n, n_pages = page_table.shape
    pps = min(2 * PAGES_PER_STEP, n_pages)
    page_specs = [pl.BlockSpec((None, IDX_DIM, PAGE_SIZE),
                               functools.partial(lambda b, g, pt, i: (pt[b, g * pps + i], 0, 0), i=i))
                  for i in range(pps)]
    grid_spec = pltpu.PrefetchScalarGridSpec(
        num_scalar_prefetch=1,
        grid=(n, n_pages // pps),
        in_specs=[pl.BlockSpec((None, N_IDX_HEADS, IDX_DIM), lambda b, g, pt: (b, 0, 0)),
                  pl.BlockSpec((None, N_IDX_HEADS, 1), lambda b, g, pt: (b, 0, 0)),
                  pl.BlockSpec((None, 1, IDX_DIM), lambda b, g, pt: (b, 0, 0))] + page_specs,
        out_specs=[pl.BlockSpec((None, pps, PAGE_SIZE), lambda b, g, pt: (b, g, 0)),
                   pl.BlockSpec((None, 1, LANES), lambda b, g, pt: (b, 0, 0))],
    )
    return pl.pallas_call(
        functools.partial(_sample_score_body, pps=pps),
        grid_spec=grid_spec,
        out_shape=[jax.ShapeDtypeStruct((n, n_pages, PAGE_SIZE), F32),
                   jax.ShapeDtypeStruct((n, 1, LANES), F32)],
        compiler_params=_cparams(("parallel", "arbitrary")),
        name="sample_scores",
    )(page_table, qi.reshape(n, N_IDX_HEADS, IDX_DIM), wi.reshape(n, N_IDX_HEADS, 1),
      ki_new.reshape(n, 1, IDX_DIM), *([kidx_t] * pps))


def _sample_select_body(past_ref, self_ref, thr_ref, cut_ref, *, topk, past_len):
    past = past_ref[...]
    own = self_ref[...][:, 0:1]
    n = past.shape[0]
    kpos = lax.broadcasted_iota(I32, past.shape, 1)

    def count(pred):
        c = jnp.sum(jnp.where(pred(past, kpos), 1.0, 0.0), axis=1, keepdims=True)
        return c + jnp.where(pred(own, past_len), 1.0, 0.0)

    def bisect(i, thr_key):
        cand = thr_key + lax.shift_left(jnp.int32(1), 31 - i)
        cand_f = _from_key(cand)
        return jnp.where(count(lambda x, _: x >= cand_f) >= topk, cand, thr_key)

    thr = _finite_threshold(lax.fori_loop(0, 32, bisect, jnp.full((n, 1), INT_MIN, I32)))
    c_gt = count(lambda x, _: x > thr)
    c_ge = count(lambda x, _: x >= thr)
    need = topk - c_gt
    tie = (c_ge - c_gt) > need
    thr_ref[...] = jnp.broadcast_to(thr, thr_ref.shape)
    cut_ref[...] = jnp.full(cut_ref.shape, past_len + 1, I32)

    @pl.when(jnp.max(jnp.where(tie, 1.0, 0.0)) > 0.0)
    def _():
        nbits = max(1, past_len.bit_length())

        def bisect_pos(i, lo):
            cand = lo + lax.shift_left(jnp.int32(1), nbits - 1 - i)
            cnt = count(lambda x, pos: jnp.where(x == thr, pos, past_len + 1) < cand)
            return jnp.where(cnt < need, cand, lo)

        lo = lax.fori_loop(0, nbits, bisect_pos, jnp.zeros((n, 1), I32))
        cut_ref[...] = jnp.broadcast_to(jnp.where(tie, lo, past_len + 1), cut_ref.shape)


def _sample_select(sc_past, sc_self, topk):
    n, past_len = sc_past.shape
    full = lambda shape: pl.BlockSpec(shape, lambda i: (0,) * len(shape))
    return pl.pallas_call(
        functools.partial(_sample_select_body, topk=topk, past_len=past_len),
        grid=(1,),
        in_specs=[full((n, past_len)), full((n, LANES))],
        out_specs=[full((n, LANES)), full((n, LANES))],
        out_shape=[jax.ShapeDtypeStruct((n, LANES), F32), jax.ShapeDtypeStruct((n, LANES), I32)],
        compiler_params=_cparams(("arbitrary",)),
        name="sample_select",
    )(sc_past, sc_self)


def _select_bias(x, kpos, thr, cut):
    bias = jnp.where(x >= thr, 0.0, NEG_BIG)
    bias = jnp.where(x == jnp.inf, NEG_BIG, bias)
    return jnp.where(x == thr, jnp.where(kpos > cut, NEG_BIG, bias), bias)


def _heads_to_rows(x):
    h, w = x.shape
    return jnp.broadcast_to(x[:, None, :], (h, HEAD_DIM, w)).reshape(h * HEAD_DIM, w)


def _sample_att_body(pt_ref, thr_ref, cut_ref, q_ref, kn_ref, vn_ref, past_ref, self_ref, *rest, pps, past_len):
    k_refs, v_refs = rest[:pps], rest[pps:2 * pps]
    o_ref, m_scr, l_scr, acc_scr = rest[2 * pps:]
    b = pl.program_id(0)
    g = pl.program_id(1)
    thr = thr_ref[b]
    cut = cut_ref[b]

    @pl.when(g == 0)
    def _():
        m_scr[...] = jnp.full(m_scr.shape, NEG_BIG, F32)
        l_scr[...] = jnp.zeros(l_scr.shape, F32)
        acc_scr[...] = jnp.zeros(acc_scr.shape, F32)

    q = jnp.broadcast_to(q_ref[...].astype(F32), (ATT_W, PAGE_SIZE))
    lane = lax.broadcasted_iota(I32, (1, PAGE_SIZE), 1)

    def head_sums(x):
        return x.reshape(N_ATT_HEADS, HEAD_DIM, x.shape[1]).sum(axis=1)

    parts = []
    for i in range(pps):
        kpos = (g * pps + i) * PAGE_SIZE + lane
        bias = _select_bias(past_ref[i:i + 1, :], kpos, thr, cut)
        parts.append(head_sums(k_refs[i][...] * q) + bias)
    m_old = m_scr[...]
    m_new = m_old
    for s in parts:
        m_new = jnp.maximum(m_new, jnp.max(s, axis=1, keepdims=True))
    alpha = jnp.exp(m_old - m_new)
    l_new = alpha * l_scr[...]
    acc = acc_scr[...] * _heads_to_rows(alpha)
    for i in range(pps):
        e = jnp.exp(parts[i] - m_new)
        l_new = l_new + jnp.sum(e, axis=1, keepdims=True)
        acc = acc + _heads_to_rows(e) * v_refs[i][...]
    m_scr[...] = m_new
    l_scr[...] = l_new
    acc_scr[...] = acc

    @pl.when(g == pl.num_programs(1) - 1)
    def _():
        s_self = head_sums(kn_ref[...] * q_ref[...].astype(F32))
        s_self = s_self + _select_bias(self_ref[...][:, 0:1], past_len, thr, cut)
        m_fin = jnp.maximum(m_new, s_self)
        a2 = jnp.exp(m_new - m_fin)
        e_self = jnp.exp(s_self - m_fin)
        l_fin = a2 * l_new + e_self
        num = jnp.sum(acc, axis=1, keepdims=True) * _heads_to_rows(a2) + _heads_to_rows(e_self) * vn_ref[...]
        o_ref[...] = num / _heads_to_rows(l_fin)


def _sample_attend(page_table, thr, cut, q, k_new, v_new, sc_past, sc_self, k_t, v_t):
    n, n_pages = page_table.shape
    pps = min(PAGES_PER_STEP, n_pages)
    past_len = n_pages * PAGE_SIZE
    page_specs = [pl.BlockSpec((None, ATT_W, PAGE_SIZE),
                               functools.partial(lambda b, g, pt, th, cu, i: (pt[b, g * pps + i], 0, 0), i=i))
                  for i in range(pps)]
    col = pl.BlockSpec((None, ATT_W, 1), lambda b, g, pt, th, cu: (b, 0, 0))
    grid_spec = pltpu.PrefetchScalarGridSpec(
        num_scalar_prefetch=3,
        grid=(n, n_pages // pps),
        in_specs=[col, col, col,
                  pl.BlockSpec((None, pps, PAGE_SIZE), lambda b, g, pt, th, cu: (b, g, 0)),
                  pl.BlockSpec((None, 1, LANES), lambda b, g, pt, th, cu: (b, 0, 0))] + page_specs + page_specs,
        out_specs=col,
        scratch_shapes=[pltpu.VMEM((N_ATT_HEADS, 1), F32), pltpu.VMEM((N_ATT_HEADS, 1), F32),
                        pltpu.VMEM((ATT_W, PAGE_SIZE), F32)],
    )
    out = pl.pallas_call(
        functools.partial(_sample_att_body, pps=pps, past_len=past_len),
        grid_spec=grid_spec,
        out_shape=jax.ShapeDtypeStruct((n, ATT_W, 1), F32),
        compiler_params=_cparams(("parallel", "arbitrary")),
        name="sample_attend",
    )(page_table, thr, cut, q.reshape(n, ATT_W, 1), k_new.reshape(n, ATT_W, 1), v_new.reshape(n, ATT_W, 1),
      sc_past, sc_self, *([k_t] * pps), *([v_t] * pps))
    return out.reshape(n, ATT_W)


def _ret_sample_body(st_ref, q_ref, k_ref, v_ref, o_ref, ns_ref):
    state = st_ref[...]
    q = q_ref[...]
    k = k_ref[...]
    v = v_ref[...]
    hh = lax.broadcasted_iota(I32, (N_RET_HEADS, 1, 1), 0)
    g = jnp.zeros((N_RET_HEADS, 1, 1), F32)
    for h in range(N_RET_HEADS):
        g = jnp.where(hh == h, math.exp(LOG_G[h]), g)
    inner = jnp.sum(q * k, axis=1, keepdims=True) * v
    cross = jnp.sum(q * state, axis=1, keepdims=True) * g
    o_ref[...] = inner + cross
    ns_ref[...] = g * state + k * v


def _ret_sample(state, rq, rk, rv):
    n = state.shape[0]
    col = pl.BlockSpec((None, N_RET_HEADS, HEAD_DIM, 1), lambda b: (b, 0, 0, 0))
    rowv = pl.BlockSpec((None, N_RET_HEADS, 1, HEAD_DIM), lambda b: (b, 0, 0, 0))
    st = pl.BlockSpec((None, N_RET_HEADS, HEAD_DIM, HEAD_DIM), lambda b: (b, 0, 0, 0))
    out, new_state = pl.pallas_call(
        _ret_sample_body,
        grid=(n,),
        in_specs=[st, col, col, rowv],
        out_specs=[rowv, st],
        out_shape=[jax.ShapeDtypeStruct((n, N_RET_HEADS, 1, HEAD_DIM), F32),
                   jax.ShapeDtypeStruct(state.shape, F32)],
        compiler_params=_cparams(("parallel",)),
        name="ret_sample",
    )(state, rq.reshape(n, N_RET_HEADS, HEAD_DIM, 1), rk.reshape(n, N_RET_HEADS, HEAD_DIM, 1),
      rv.reshape(n, N_RET_HEADS, 1, HEAD_DIM))
    return out.reshape(n, RET_W), new_state


def _rope_tables(pos):
    half = HEAD_DIM // 2
    inv = ROPE_BASE ** (-jnp.arange(half, dtype=F32) / half)
    ang = pos.astype(F32)[:, None] * inv[None, :]
    cos, sin = jnp.cos(ang), jnp.sin(ang)
    cos = jnp.tile(jnp.concatenate([cos, cos], axis=1), (1, N_RET_HEADS))
    sin = jnp.tile(jnp.concatenate([-sin, sin], axis=1), (1, N_RET_HEADS))
    return cos, sin


def _block_diag_ones(width):
    r = np.arange(width)[:, None] // HEAD_DIM
    c = np.arange(width)[None, :] // HEAD_DIM
    return jnp.asarray((r == c).astype(np.float32), dtype=BF16)


def _repack_w_in(w):
    o = np.cumsum([0, ATT_W, ATT_W, ATT_W, IDXQ_W, IDX_DIM, N_IDX_HEADS, RET_W, RET_W, RET_W, RET_W, MEM_W])
    pad = jnp.zeros((w.shape[0], KW_W - IDX_DIM - N_IDX_HEADS), w.dtype)
    return jnp.concatenate([w[:, o[0]:o[4]], w[:, o[4]:o[6]], pad, w[:, o[6]:o[11]]], axis=1).astype(BF16)


def kernel(x_prompt, x_sample, mem_prompt, cache_k, cache_v, cache_kidx, state_ret, cache_mem_k, cache_mem_v,
           page_table, ffn1_norm_g, ffn1_w_gu, ffn1_w_down, mix_norm_g, w_in, att_q_norm_g, att_k_norm_g,
           ret_gn_g, mem_norm_g, w_mem_kv, mem_q_norm_g, mem_k_norm_g, w_out, ffn2_norm_g, ffn2_w_gu, ffn2_w_down):
    n_b, s_len, d = x_prompt.shape
    n_s, t_s, _ = x_sample.shape
    assert t_s == 1, "the sample group decodes one token per sequence"
    depth = w_in.shape[0]
    n_mem = mem_prompt.shape[1]
    n_pool = cache_k.shape[1]
    past_len = page_table.shape[1] * PAGE_SIZE
    page_table = page_table.astype(I32)

    cos_p, sin_p = _rope_tables(jnp.arange(s_len, dtype=I32))
    cos_s, sin_s = _rope_tables(jnp.full((n_s,), past_len, I32))
    bd = _block_diag_ones(ATT_W)
    bd_ret = _block_diag_ones(RET_W)
    tile8 = lambda g_: jnp.tile(g_, N_ATT_HEADS).reshape(1, ATT_W)
    tile4 = lambda g_: jnp.tile(g_, N_MEM_HEADS).reshape(1, MEM_W)

    yp = x_prompt.reshape(n_b * s_len, d)
    ys = x_sample.reshape(n_s, d)
    mem = mem_prompt.reshape(n_b * n_mem, d)
    outs = [[] for _ in range(10)]
    for l in range(depth):
        g1 = ffn1_norm_g[l].reshape(1, d)
        g2 = ffn2_norm_g[l].reshape(1, d)
        gmix = mix_norm_g[l].reshape(1, d)
        w1gu, w1d = ffn1_w_gu[l].astype(BF16), ffn1_w_down[l].astype(BF16)
        w2gu, w2d = ffn2_w_gu[l].astype(BF16), ffn2_w_down[l].astype(BF16)
        w_in_l = _repack_w_in(w_in[l])
        w_out_l = w_out[l].astype(BF16)
        gq, gk, gm = tile8(att_q_norm_g[l]), tile8(att_k_norm_g[l]), tile4(mem_q_norm_g[l])
        gn = ret_gn_g[l].reshape(1, RET_W)

        yp = _ffn(yp, g1, w1gu, w1d)
        (q, k, kb, v, vt, qi, kw, ki, ki2, rq, rk, rv, rg, mq) = _proj(
            yp, gmix, w_in_l, gq, gk, gm, cos_p, sin_p, bd, n_b)
        mk, mv = _memkv(mem, mem_norm_g[l].reshape(1, d), w_mem_kv[l].astype(BF16), tile4(mem_k_norm_g[l]), bd_ret)
        att = _dsa_prompt_pairs(q, qi, kw, kb, vt, ki2, n_b, s_len)
        ret, s_fin = _ret_prompt(rq, rk, rv, n_b, s_len)
        mo = _mematt_prompt(mq, mk, mv, n_b)
        yp = _mixout(yp, att, ret, rg, mo, gn, bd_ret, w_out_l)
        yp = _ffn(yp, g2, w2gu, w2d)
        outs[0].append(k.reshape(n_b, s_len, N_ATT_HEADS, HEAD_DIM))
        outs[1].append(v.reshape(n_b, s_len, N_ATT_HEADS, HEAD_DIM))
        outs[2].append(ki.reshape(n_b, s_len, IDX_DIM))
        outs[3].append(s_fin)
        outs[4].append(mk.reshape(n_b, n_mem, N_MEM_HEADS, HEAD_DIM))
        outs[5].append(mv.reshape(n_b, n_mem, N_MEM_HEADS, HEAD_DIM))

        ys = _ffn(ys, g1, w1gu, w1d)
        (q, k, kb, v, vt, qi, kw, ki, ki2, rq, rk, rv, rg, mq) = _proj(
            ys, gmix, w_in_l, gq, gk, gm, cos_s, sin_s, bd, 1)
        wi = kw[:, IDX_DIM:IDX_DIM + N_IDX_HEADS]
        sc_past, sc_self = _sample_scores(page_table, qi, wi, ki, jnp.transpose(cache_kidx[l], (0, 2, 1)))
        topk = min(TOPK_MAX, (past_len + t_s) // 4)
        thr, cut = _sample_select(sc_past.reshape(n_s, past_len), sc_self.reshape(n_s, LANES), topk)
        k_t = jnp.transpose(cache_k[l], (0, 2, 3, 1)).reshape(n_pool, ATT_W, PAGE_SIZE)
        v_t = jnp.transpose(cache_v[l], (0, 2, 3, 1)).reshape(n_pool, ATT_W, PAGE_SIZE)
        att = _sample_attend(page_table, thr[:, 0], cut[:, 0], q, k, v, sc_past, sc_self, k_t, v_t)
        ret, s_new = _ret_sample(state_ret[l].astype(F32), rq, rk, rv)
        mo = _mematt_sample(mq, cache_mem_k[l].reshape(n_s, n_mem, MEM_W), cache_mem_v[l].reshape(n_s, n_mem, MEM_W))
        ys = _mixout(ys, att, ret, rg, mo, gn, bd_ret, w_out_l)
        ys = _ffn(ys, g2, w2gu, w2d)
        outs[6].append(k.reshape(n_s, t_s, N_ATT_HEADS, HEAD_DIM))
        outs[7].append(v.reshape(n_s, t_s, N_ATT_HEADS, HEAD_DIM))
        outs[8].append(ki.reshape(n_s, t_s, IDX_DIM))
        outs[9].append(s_new)

    return (yp.reshape(n_b, s_len, d), ys.reshape(n_s, t_s, d)) + tuple(jnp.stack(o) for o in outs)
```

```python
import functools
import math

import numpy as np
import jax
import jax.numpy as jnp
from jax import lax
from jax.experimental import pallas as pl
from jax.experimental.pallas import tpu as pltpu

F32 = jnp.float32
BF16 = jnp.bfloat16
I32 = jnp.int32

HEAD_DIM = 64
N_ATT_HEADS = 8
N_IDX_HEADS = 8
IDX_DIM = 64
TOPK_MAX = 256
N_RET_HEADS = 4
N_MEM_HEADS = 4
PAGE_SIZE = 128
EPS = 1e-6
ROPE_BASE = 10000.0

ATT_W = N_ATT_HEADS * HEAD_DIM
RET_W = N_RET_HEADS * HEAD_DIM
MEM_W = N_MEM_HEADS * HEAD_DIM
IDXQ_W = N_IDX_HEADS * IDX_DIM
LANES = 128
KW_W = LANES
C_Q = 0
C_K = C_Q + ATT_W
C_V = C_K + ATT_W
C_QI = C_V + ATT_W
C_KW = C_QI + IDXQ_W
C_RQ = C_KW + KW_W
C_RK = C_RQ + RET_W
C_RV = C_RK + RET_W
C_RG = C_RV + RET_W
C_MQ = C_RG + RET_W
PROJ_W = C_MQ + MEM_W

QK_SCALE = HEAD_DIM ** -0.5
IDX_SCALE = IDX_DIM ** -0.5
IDX_HEAD_SCALE = N_IDX_HEADS ** -0.5
NEG_BIG = -1e30
M_INIT = -(2.0 ** 100)
INT_MIN = -(2 ** 31)
F32_LOWEST = float(np.finfo(np.float32).min)
LOG_G = [float(np.log1p(np.float32(-(2.0 ** (-5.0 - h))))) for h in range(N_RET_HEADS)]

VMEM_LIMIT = 56 * 1024 * 1024
ROW_TILE = 512
KEY_CHUNK = 512
Q_TILE = LANES
VT_ROWS = LANES + 16
BISECT_GROUP = 4
RET_CHUNK = 128
PAGES_PER_STEP = 16


def _cparams(sem):
    return pltpu.CompilerParams(dimension_semantics=sem, vmem_limit_bytes=VMEM_LIMIT)


def _resident(shape):
    nd = len(shape)
    return pl.BlockSpec(shape, lambda *_: (0,) * nd, pipeline_mode=pl.Buffered(1))


def _rms(x, g):
    return x * lax.rsqrt(jnp.mean(x * x, axis=-1, keepdims=True) + EPS) * g


def _head_rms(z, g, bd):
    z2 = z * z
    hi = z2.astype(BF16)
    lo = (z2 - hi.astype(F32)).astype(BF16)
    ss = jnp.dot(hi, bd, preferred_element_type=F32) + jnp.dot(lo, bd, preferred_element_type=F32)
    return z * lax.rsqrt(ss * (1.0 / HEAD_DIM) + EPS) * g


def _nt(a, b):
    return lax.dot_general(a, b, (((1,), (1,)), ((), ())), preferred_element_type=F32)


def _from_key(k):
    return lax.bitcast_convert_type(jnp.where(k < 0, k ^ jnp.int32(0x7FFFFFFF), k), F32)


def _finite_threshold(thr_key):
    return jnp.where(thr_key == INT_MIN, F32_LOWEST, jnp.maximum(_from_key(thr_key), F32_LOWEST))


def _ffn_half_step(x, g, wgu_ref, wd_ref, d_ff, fc):
    h = _rms(x, g).astype(BF16)
    acc = jnp.zeros(x.shape, F32)
    for c in range(d_ff // fc):
        gate = jnp.dot(h, wgu_ref[:, c * fc:(c + 1) * fc], preferred_element_type=F32)
        up = jnp.dot(h, wgu_ref[:, d_ff + c * fc:d_ff + (c + 1) * fc], preferred_element_type=F32)
        act = (gate * jax.nn.sigmoid(gate) * up).astype(BF16)
        acc = acc + jnp.dot(act, wd_ref[c * fc:(c + 1) * fc, :], preferred_element_type=F32)
    return x + 0.5 * acc


def _ffn_chunk(d_ff):
    return d_ff // 2 if (d_ff // 2) % LANES == 0 else d_ff


def _ffn_body(x_ref, g_ref, wgu_ref, wd_ref, o_ref, *, d_ff, fc):
    o_ref[...] = _ffn_half_step(x_ref[...], g_ref[...], wgu_ref, wd_ref, d_ff, fc)


def _ffn(x, g, wgu, wd):
    t, d = x.shape
    d_ff = wd.shape[0]
    tm = min(ROW_TILE, t)
    fc = _ffn_chunk(d_ff)
    return pl.pallas_call(
        functools.partial(_ffn_body, d_ff=d_ff, fc=fc),
        grid=(t // tm,),
        in_specs=[pl.BlockSpec((tm, d), lambda i: (i, 0)), _resident((1, d)),
                  _resident(wgu.shape), _resident(wd.shape)],
        out_specs=pl.BlockSpec((tm, d), lambda i: (i, 0)),
        out_shape=jax.ShapeDtypeStruct((t, d), F32),
        compiler_params=_cparams(("parallel",)),
        name="ffn_half",
    )(x, g, wgu, wd)


def _proj_body(x_ref, g_ref, w_ref, gq_ref, gk_ref, gm_ref, cos_ref, sin_ref, bd_ref,
               q_ref, k_ref, kb_ref, v_ref, vt_ref, qi_ref, kw_ref, ki_ref, ki2_ref,
               rq_ref, rk_ref, rv_ref, rg_ref, mq_ref):
    h = _rms(x_ref[...], g_ref[...]).astype(BF16)

    def proj(a, width):
        return jnp.dot(h, w_ref[:, a:a + width], preferred_element_type=F32)

    bd = bd_ref[...]
    q = _head_rms(proj(C_Q, ATT_W), gq_ref[...], bd)
    q_ref[...] = (q * QK_SCALE).astype(BF16)
    k = _head_rms(proj(C_K, ATT_W), gk_ref[...], bd)
    k_ref[...] = k
    kb_ref[...] = k.astype(BF16)
    v = proj(C_V, ATT_W)
    v_ref[...] = v
    vt = jnp.transpose(v)
    tail_row = lax.broadcasted_iota(I32, (VT_ROWS - LANES, vt.shape[1]), 0)
    tail = jnp.where(tail_row == 0, 1.0, 0.0)
    for p in range(N_ATT_HEADS // 2):
        vt_ref[p] = jnp.concatenate([vt[p * LANES:(p + 1) * LANES, :], tail], axis=0).astype(BF16)
    qi_ref[...] = proj(C_QI, IDXQ_W).astype(BF16)
    kw = proj(C_KW, KW_W)
    kw_ref[...] = kw
    ki = kw[:, :IDX_DIM]
    ki_ref[...] = ki
    kib = ki.astype(BF16)
    ki2_ref[...] = jnp.concatenate([kib, kib], axis=1)

    lane = lax.broadcasted_iota(I32, (1, RET_W), 1)
    first_half = (lane % HEAD_DIM) < (HEAD_DIM // 2)
    cos = jnp.concatenate([cos_ref[...]] * (RET_W // LANES), axis=1)
    sin = jnp.concatenate([sin_ref[...]] * (RET_W // LANES), axis=1)

    def rot(x):
        swapped = jnp.where(first_half, pltpu.roll(x, RET_W - HEAD_DIM // 2, axis=1),
                            pltpu.roll(x, HEAD_DIM // 2, axis=1))
        return x * cos + swapped * sin

    rq_ref[...] = rot(proj(C_RQ, RET_W))
    rk_ref[...] = rot(proj(C_RK, RET_W)) * QK_SCALE
    rv_ref[...] = proj(C_RV, RET_W)
    rg_ref[...] = proj(C_RG, RET_W)
    mq = _head_rms(proj(C_MQ, MEM_W), gm_ref[...], bd[:MEM_W, :MEM_W])
    mq_ref[...] = (mq * QK_SCALE).astype(BF16)


def _proj(y, g, w, gq, gk, gm, cos, sin, bd, n_batch):
    t, d = y.shape
    s_len = t // n_batch
    tm = min(ROW_TILE, s_len)
    nsb = s_len // tm
    row = lambda w_: pl.BlockSpec((tm, w_), lambda i: (i, 0))
    tab = pl.BlockSpec((tm, LANES), lambda i: (i % nsb, 0))
    out_shapes = [
        (jax.ShapeDtypeStruct((t, ATT_W), BF16), row(ATT_W)),
        (jax.ShapeDtypeStruct((t, ATT_W), F32), row(ATT_W)),
        (jax.ShapeDtypeStruct((t, ATT_W), BF16), row(ATT_W)),
        (jax.ShapeDtypeStruct((t, ATT_W), F32), row(ATT_W)),
        (jax.ShapeDtypeStruct((n_batch, nsb, N_ATT_HEADS // 2, VT_ROWS, tm), BF16),
         pl.BlockSpec((None, None, N_ATT_HEADS // 2, VT_ROWS, tm),
                      lambda i: (i // nsb, i % nsb, 0, 0, 0))),
        (jax.ShapeDtypeStruct((t, IDXQ_W), BF16), row(IDXQ_W)),
        (jax.ShapeDtypeStruct((t, KW_W), F32), row(KW_W)),
        (jax.ShapeDtypeStruct((t, IDX_DIM), F32), row(IDX_DIM)),
        (jax.ShapeDtypeStruct((t, 2 * IDX_DIM), BF16), row(2 * IDX_DIM)),
        (jax.ShapeDtypeStruct((t, RET_W), F32), row(RET_W)),
        (jax.ShapeDtypeStruct((t, RET_W), F32), row(RET_W)),
        (jax.ShapeDtypeStruct((t, RET_W), F32), row(RET_W)),
        (jax.ShapeDtypeStruct((t, RET_W), F32), row(RET_W)),
        (jax.ShapeDtypeStruct((t, MEM_W), BF16), row(MEM_W)),
    ]
    return pl.pallas_call(
        _proj_body,
        grid=(t // tm,),
        in_specs=[row(d), _resident((1, d)), _resident(w.shape), _resident((1, ATT_W)),
                  _resident((1, ATT_W)), _resident((1, MEM_W)), tab, tab, _resident(bd.shape)],
        out_specs=[s for _, s in out_shapes],
        out_shape=[s for s, _ in out_shapes],
        compiler_params=_cparams(("parallel",)),
        name="mix_proj",
    )(y, g, w, gq, gk, gm, cos, sin, bd)


def _memkv_body(x_ref, g_ref, w_ref, gk_ref, bd_ref, mk_ref, mv_ref):
    h = _rms(x_ref[...], g_ref[...]).astype(BF16)
    z = jnp.dot(h, w_ref[...], preferred_element_type=F32)
    mk_ref[...] = _head_rms(z[:, :MEM_W], gk_ref[...], bd_ref[...])
    mv_ref[...] = z[:, MEM_W:]


def _memkv(mem, g, w, gk, bd):
    t, d = mem.shape
    tm = min(ROW_TILE, t)
    row = lambda w_: pl.BlockSpec((tm, w_), lambda i: (i, 0))
    return pl.pallas_call(
        _memkv_body,
        grid=(t // tm,),
        in_specs=[row(d), _resident((1, d)), _resident(w.shape), _resident((1, MEM_W)), _resident(bd.shape)],
        out_specs=[row(MEM_W), row(MEM_W)],
        out_shape=[jax.ShapeDtypeStruct((t, MEM_W), F32)] * 2,
        compiler_params=_cparams(("parallel",)),
        name="mem_kv",
    )(mem, g, w, gk, bd)


def _dsa_pair_body(q_ref, qi_ref, kw_ref, k_ref, vt_ref, ki2_ref, o_ref,
                   key_scr, qi2_scr, q2_scr, cut_scr, m_scr, acc_scr, sa_scr, sb_scr, *, topk, sc, s_len):
    n_st = 2
    jj = pl.program_id(1)
    n_ch = ((n_st * jj + 1) * Q_TILE + Q_TILE + sc - 1) // sc
    n_pair = N_ATT_HEADS // 2
    lane = lax.broadcasted_iota(I32, (1, LANES), 1)
    qpos = [(n_st * jj + s) * Q_TILE + lane for s in range(n_st)]
    kiota = lax.broadcasted_iota(I32, (sc, LANES), 0)
    streams = range(n_st)

    r2 = lax.broadcasted_iota(I32, (2 * Q_TILE, LANES), 0)
    c2 = lax.broadcasted_iota(I32, (2 * Q_TILE, LANES), 1)
    keep = (r2 < Q_TILE) == (c2 < HEAD_DIM)
    eye2 = jnp.where(r2 % Q_TILE == c2, 1.0, 0.0).astype(BF16)
    w_s = []
    for s in streams:
        rows = slice(s * Q_TILE, (s + 1) * Q_TILE)
        for p in range(n_pair):
            a = qi_ref[rows, p * LANES:(p + 1) * LANES].astype(F32)
            qi2_scr[s, p] = jnp.where(keep, jnp.concatenate([a, a], axis=0), 0.0).astype(BF16)
            a = q_ref[rows, p * LANES:(p + 1) * LANES].astype(F32)
            q2 = jnp.where(keep, jnp.concatenate([a, a], axis=0), 0.0).astype(BF16)
            q2_scr[s, p] = jnp.concatenate([q2, eye2], axis=1)
        w_t = jnp.transpose(kw_ref[rows, :])[IDX_DIM:IDX_DIM + N_IDX_HEADS, :]
        w_s.append((w_t * IDX_HEAD_SCALE) * IDX_SCALE)

    def score_chunk(c, carry):
        off = pl.multiple_of(c * sc, sc)
        kc = ki2_ref[pl.ds(off, sc), :]
        acc = [jnp.zeros((sc, LANES), F32) for _ in streams]
        for p in range(n_pair):
            for s in streams:
                lg = _nt(kc, qi2_scr[s, p])
                acc[s] = acc[s] + jnp.maximum(lg[:, :LANES], 0.0) * w_s[s][2 * p:2 * p + 1, :]
                acc[s] = acc[s] + jnp.maximum(lg[:, LANES:], 0.0) * w_s[s][2 * p + 1:2 * p + 2, :]
        for s in streams:
            key_scr[s, pl.ds(off, sc), :] = jnp.where(off + kiota <= qpos[s], acc[s], -jnp.inf)
        return carry

    lax.fori_loop(0, n_ch, score_chunk, 0)

    def count(preds):
        def body(c, accs):
            off = pl.multiple_of(c * sc, sc)
            accs = list(accs)
            for g in range(sc // 64):
                rows = pl.ds(off + g * 64, 64)
                for s in streams:
                    hit = preds[s](key_scr[s, rows, :], off + kiota[g * 64:(g + 1) * 64])
                    accs[s] = jnp.where(hit, accs[s] + 1.0, accs[s])
            return tuple(accs)
        accs = lax.fori_loop(0, n_ch, body, tuple(jnp.zeros((64, LANES), F32) for _ in streams))
        return [a.reshape(8, 8, LANES).sum(axis=0).sum(axis=0, keepdims=True) for a in accs]

    def bisect_group(state):
        i, thrs, actives, _ = state
        thrs, actives = list(thrs), list(actives)
        for _ in range(BISECT_GROUP):
            cands = [t + lax.shift_left(jnp.int32(1), 31 - i) for t in thrs]
            cnts = count([functools.partial(lambda x, _, cand: x >= cand, cand=_from_key(cand)) for cand in cands])
            for s in streams:
                thrs[s] = jnp.where((cnts[s] >= topk) & (actives[s] > 0.0), cands[s], thrs[s])
                actives[s] = jnp.where(cnts[s] == topk, 0.0, actives[s])
            i = i + 1
        return i, tuple(thrs), tuple(actives), jnp.max(jnp.maximum(actives[0], actives[1]))

    actives0 = tuple(jnp.where(qpos[s] + 1 <= topk, 0.0, 1.0) for s in streams)
    _, thrs, actives, n_active = lax.while_loop(
        lambda st: (st[0] < 32) & (st[3] > 0.0), bisect_group,
        (jnp.int32(0), tuple(jnp.full((1, LANES), INT_MIN, I32) for _ in streams), actives0,
         jnp.max(jnp.maximum(actives0[0], actives0[1]))))
    for s in streams:
        cut_scr[s] = jnp.full((1, LANES), s_len, I32)

    thr_f = [_finite_threshold(t) for t in thrs]

    @pl.when(n_active > 0.0)
    def _():
        c_gt = count([functools.partial(lambda x, _, t: x > t, t=t) for t in thr_f])
        c_ge = count([functools.partial(lambda x, _, t: x >= t, t=t) for t in thr_f])
        need = [topk - c for c in c_gt]
        tie = [((c_ge[s] - c_gt[s]) > need[s]) & (actives[s] > 0.0) for s in streams]
        nbits = max(1, (s_len - 1).bit_length())

        def bisect_pos(i, los):
            cands = [lo + lax.shift_left(jnp.int32(1), nbits - 1 - i) for lo in los]
            cnts = count([functools.partial(lambda x, kpos, t, cand: jnp.where(x == t, kpos, s_len) < cand,
                                            t=thr_f[s], cand=cands[s]) for s in streams])
            return tuple(jnp.where(cnts[s] < need[s], cands[s], los[s]) for s in streams)

        los = lax.fori_loop(0, nbits, bisect_pos, tuple(jnp.zeros((1, LANES), I32) for _ in streams))
        for s in streams:
            cut_scr[s] = jnp.where(tie[s], los[s], s_len)

    cuts = [cut_scr[s] for s in streams]

    m_scr[...] = jnp.full(m_scr.shape, M_INIT, F32)
    acc_scr[...] = jnp.zeros(acc_scr.shape, F32)

    def qk_chunk(c, s_scr):
        off = pl.multiple_of(c * sc, sc)
        biases = []
        for s in streams:
            bias = _select_bias(key_scr[s, pl.ds(off, sc), :], off + kiota, thr_f[s], cuts[s])
            biases.append(bias.astype(BF16))
        for p in range(n_pair):
            kp = k_ref[pl.ds(off, sc), p * LANES:(p + 1) * LANES]
            for s in streams:
                s_scr[s, p] = _nt(jnp.concatenate([kp, biases[s]], axis=1), q2_scr[s, p])

    def softmax_pv_chunk(c, s_scr):
        es, alphas = {}, {}
        for p in range(n_pair):
            for s in streams:
                x = s_scr[s, p].astype(BF16)
                m_c = x.reshape(sc // 64, 64, 2 * Q_TILE).max(axis=0)
                m_c = m_c.reshape(4, 16, 2 * Q_TILE).max(axis=0).astype(F32).max(axis=0, keepdims=True)
                m_old = m_scr[s, p]
                m_new = jnp.maximum(m_old, m_c)
                alphas[s, p] = jnp.exp(m_old - m_new)
                es[s, p] = jnp.exp(x - m_new.astype(BF16))
                m_scr[s, p] = m_new
            for s in streams:
                pv = jnp.dot(vt_ref[c, p], es[s, p], preferred_element_type=F32)
                acc_scr[s, p] = acc_scr[s, p] * alphas[s, p] + pv

    def att_two_chunks(i, carry):
        c = 2 * i
        qk_chunk(c + 1, sb_scr)
        softmax_pv_chunk(c, sa_scr)
        qk_chunk(c + 2, sa_scr)
        softmax_pv_chunk(c + 1, sb_scr)
        return carry

    qk_chunk(0, sa_scr)
    n_two = (n_ch - 1) // 2
    lax.fori_loop(0, n_two, att_two_chunks, 0)

    @pl.when(n_ch % 2 == 0)
    def _():
        qk_chunk(n_ch - 1, sb_scr)
        softmax_pv_chunk(n_ch - 2, sa_scr)
        softmax_pv_chunk(n_ch - 1, sb_scr)

    @pl.when(n_ch % 2 == 1)
    def _():
        softmax_pv_chunk(n_ch - 1, sa_scr)

    for s in streams:
        outs = []
        for p in range(n_pair):
            a = acc_scr[s, p]
            l = a[LANES:LANES + 1, :]
            outs.append(a[:HEAD_DIM, :LANES] / l[:, :LANES])
            outs.append(a[HEAD_DIM:LANES, LANES:] / l[:, LANES:])
        o_ref[s * Q_TILE:(s + 1) * Q_TILE, :] = jnp.transpose(jnp.concatenate(outs, axis=0))


def _dsa_prompt_pairs(q, qi, kw, kb, vt, ki2, n_batch, s_len):
    t = q.shape[0]
    sc = min(KEY_CHUNK, s_len)
    n_st = 2
    nq = s_len // (n_st * Q_TILE)
    n_pair = N_ATT_HEADS // 2
    topk = min(TOPK_MAX, s_len // 4)
    qrow = lambda w_: pl.BlockSpec((n_st * Q_TILE, w_), lambda b, j: (b * nq + j, 0))
    per_batch = lambda w_: pl.BlockSpec((s_len, w_), lambda b, j: (b, 0), pipeline_mode=pl.Buffered(1))
    return pl.pallas_call(
        functools.partial(_dsa_pair_body, topk=topk, sc=sc, s_len=s_len),
        grid=(n_batch, nq),
        in_specs=[qrow(ATT_W), qrow(IDXQ_W), qrow(KW_W), per_batch(ATT_W),
                  pl.BlockSpec((None, s_len // sc, n_pair, VT_ROWS, sc), lambda b, j: (b, 0, 0, 0, 0),
                               pipeline_mode=pl.Buffered(1)),
                  per_batch(2 * IDX_DIM)],
        out_specs=qrow(ATT_W),
        out_shape=jax.ShapeDtypeStruct((t, ATT_W), F32),
        scratch_shapes=[pltpu.VMEM((n_st, s_len, LANES), F32),
                        pltpu.VMEM((n_st, n_pair, 2 * Q_TILE, LANES), BF16),
                        pltpu.VMEM((n_st, n_pair, 2 * Q_TILE, 2 * LANES), BF16),
                        pltpu.VMEM((n_st, 1, LANES), I32),
                        pltpu.VMEM((n_st, n_pair, 1, 2 * Q_TILE), F32),
                        pltpu.VMEM((n_st, n_pair, VT_ROWS, 2 * Q_TILE), F32),
                        pltpu.VMEM((n_st, n_pair, sc, 2 * Q_TILE), F32),
                        pltpu.VMEM((n_st, n_pair, sc, 2 * Q_TILE), F32)],
        compiler_params=_cparams(("parallel", "arbitrary")),
        name="dsa_prompt",
    )(q, qi, kw, kb, vt, ki2)


def _ret_lane_const(vals):
    lane = lax.broadcasted_iota(I32, (1, RET_W), 1)
    out = jnp.zeros((1, RET_W), F32)
    for h, v in enumerate(vals):
        out = jnp.where(lane // HEAD_DIM == h, v, out)
    return out


def _ret_body(rq_ref, rk_ref, rv_ref, o_ref, st_ref, sbd_scr, *, ch):
    c = pl.program_id(0)
    n_seq = sbd_scr.shape[0]

    @pl.when(c == 0)
    def _():
        sbd_scr[...] = jnp.zeros(sbd_scr.shape, F32)

    head = lax.broadcasted_iota(I32, (1, RET_W), 1) // HEAD_DIM
    log_g = _ret_lane_const(LOG_G)
    i_col = lax.broadcasted_iota(I32, (ch, 1), 0).astype(F32)
    q_dec = jnp.exp(log_g * (i_col + 1.0))
    k_dec = jnp.exp(log_g * (ch - 1.0 - i_col))
    ii = lax.broadcasted_iota(I32, (ch, ch), 0)
    jj = lax.broadcasted_iota(I32, (ch, ch), 1)
    causal = ii >= jj
    diff = jnp.where(causal, ii - jj, 0).astype(F32)
    decays = [jnp.where(causal, jnp.exp(LOG_G[h] * diff), 0.0) for h in range(N_RET_HEADS)]
    rh = lax.broadcasted_iota(I32, (RET_W, RET_W), 0) // HEAD_DIM
    ch_ = lax.broadcasted_iota(I32, (RET_W, RET_W), 1) // HEAD_DIM

    for b in range(n_seq):
        q = rq_ref[b]
        k = rk_ref[b]
        vb = rv_ref[b].astype(BF16)
        kb = k.astype(BF16)
        state = sbd_scr[b]
        cross = jnp.dot(q.astype(BF16), state.astype(BF16), preferred_element_type=F32) * q_dec
        inner = jnp.zeros((ch, RET_W), F32)
        for h in range(N_RET_HEADS):
            qm = jnp.where(head == h, q, 0.0).astype(BF16)
            sc = (_nt(qm, kb) * decays[h]).astype(BF16)
            inner = inner + jnp.where(head == h, jnp.dot(sc, vb, preferred_element_type=F32), 0.0)
        o_ref[b] = inner + cross

        kd = (k * k_dec).astype(BF16)
        kv = lax.dot_general(kd, vb, (((0,), (0,)), ((), ())), preferred_element_type=F32)
        new_state = state * jnp.exp(log_g * float(ch)) + jnp.where(rh == ch_, kv, 0.0)
        sbd_scr[b] = new_state

    @pl.when(c == pl.num_programs(0) - 1)
    def _():
        for b in range(n_seq):
            for h in range(N_RET_HEADS):
                st_ref[b, h] = sbd_scr[b, h * HEAD_DIM:(h + 1) * HEAD_DIM, h * HEAD_DIM:(h + 1) * HEAD_DIM]


def _ret_prompt(rq, rk, rv, n_batch, s_len):
    ch = min(RET_CHUNK, s_len)
    blk = pl.BlockSpec((n_batch, ch, RET_W), lambda c: (0, c, 0))
    as3d = lambda a: a.reshape(n_batch, s_len, RET_W)
    out, state = pl.pallas_call(
        functools.partial(_ret_body, ch=ch),
        grid=(s_len // ch,),
        in_specs=[blk, blk, blk],
        out_specs=[blk, pl.BlockSpec((n_batch, N_RET_HEADS, HEAD_DIM, HEAD_DIM), lambda c: (0, 0, 0, 0))],
        out_shape=[jax.ShapeDtypeStruct((n_batch, s_len, RET_W), F32),
                   jax.ShapeDtypeStruct((n_batch, N_RET_HEADS, HEAD_DIM, HEAD_DIM), F32)],
        scratch_shapes=[pltpu.VMEM((n_batch, RET_W, RET_W), F32)],
        compiler_params=_cparams(("arbitrary",)),
        name="ret_prompt",
    )(as3d(rq), as3d(rk), as3d(rv))
    return out.reshape(n_batch * s_len, RET_W), state


def _softmax_lanes(s):
    m = jnp.max(s, axis=-1, keepdims=True)
    e = jnp.exp(s - m)
    return e / jnp.sum(e, axis=-1, keepdims=True)


def _mematt_body(mq_ref, mk_ref, mv_ref, o_ref):
    mq = mq_ref[...].astype(F32)
    mk = mk_ref[...].astype(BF16)
    mv = mv_ref[...].astype(BF16)
    head = lax.broadcasted_iota(I32, (1, MEM_W), 1) // HEAD_DIM
    out = jnp.zeros(mq.shape, F32)
    for h in range(N_MEM_HEADS):
        qm = jnp.where(head == h, mq, 0.0).astype(BF16)
        p = _softmax_lanes(_nt(qm, mk)).astype(BF16)
        out = out + jnp.where(head == h, jnp.dot(p, mv, preferred_element_type=F32), 0.0)
    o_ref[...] = out


def _mematt_prompt(mq, mk, mv, n_batch):
    t = mq.shape[0]
    n_mem = mk.shape[0] // n_batch
    s_len = t // n_batch
    tm = min(ROW_TILE, s_len)
    nsb = s_len // tm
    row = pl.BlockSpec((tm, MEM_W), lambda i: (i, 0))
    mem = pl.BlockSpec((n_mem, MEM_W), lambda i: (i // nsb, 0))
    return pl.pallas_call(
        _mematt_body,
        grid=(t // tm,),
        in_specs=[row, mem, mem],
        out_specs=row,
        out_shape=jax.ShapeDtypeStruct((t, MEM_W), F32),
        compiler_params=_cparams(("parallel",)),
        name="mem_attend",
    )(mq, mk, mv)


def _mematt_sample_body(mq_ref, mk_ref, mv_ref, o_ref):
    rows = 8
    mq = jnp.broadcast_to(mq_ref[...].astype(F32), (rows, MEM_W))
    sel = lax.broadcasted_iota(I32, (rows, MEM_W), 1) // HEAD_DIM == lax.broadcasted_iota(I32, (rows, MEM_W), 0)
    qbd = jnp.where(sel, mq, 0.0).astype(BF16)
    p = _softmax_lanes(jnp.dot(qbd, mk_ref[...].astype(BF16), preferred_element_type=F32)).astype(BF16)
    o = _nt(p, mv_ref[...].astype(BF16))
    o_ref[...] = jnp.sum(jnp.where(sel, o, 0.0), axis=0, keepdims=True)


def _mematt_sample(mq, mk_t, mv_t):
    n, n_mem = mk_t.shape[0], mk_t.shape[2]
    one = pl.BlockSpec((None, 1, MEM_W), lambda b: (b, 0, 0))
    mem = pl.BlockSpec((None, MEM_W, n_mem), lambda b: (b, 0, 0))
    out = pl.pallas_call(
        _mematt_sample_body,
        grid=(n,),
        in_specs=[one, mem, mem],
        out_specs=one,
        out_shape=jax.ShapeDtypeStruct((n, 1, MEM_W), F32),
        compiler_params=_cparams(("parallel",)),
        name="mem_attend_sample",
    )(mq.reshape(n, 1, MEM_W), mk_t, mv_t)
    return out.reshape(n, MEM_W)


def _mixout_ffn_body(y_ref, att_ref, ret_ref, rg_ref, mo_ref, gn_ref, bd_ref, w_ref, g2_ref, wgu_ref, wd_ref,
                     o_ref, *, d_ff, fc):
    retn = _head_rms(ret_ref[...], gn_ref[...], bd_ref[...])
    rg = rg_ref[...]
    gated = (rg * jax.nn.sigmoid(rg)) * retn
    acc = jnp.dot(att_ref[...].astype(BF16), w_ref[:ATT_W, :], preferred_element_type=F32)
    acc = acc + jnp.dot(gated.astype(BF16), w_ref[ATT_W:ATT_W + RET_W, :], preferred_element_type=F32)
    acc = acc + jnp.dot(mo_ref[...].astype(BF16), w_ref[ATT_W + RET_W:, :], preferred_element_type=F32)
    o_ref[...] = _ffn_half_step(y_ref[...] + acc, g2_ref[...], wgu_ref, wd_ref, d_ff, fc)


def _mixout_ffn(y, att, ret, rg, mo, gn, bd, w, g2, wgu, wd):
    t, d = y.shape
    d_ff = wd.shape[0]
    tm = min(ROW_TILE, t)
    row = lambda w_: pl.BlockSpec((tm, w_), lambda i: (i, 0))
    return pl.pallas_call(
        functools.partial(_mixout_ffn_body, d_ff=d_ff, fc=_ffn_chunk(d_ff)),
        grid=(t // tm,),
        in_specs=[row(d), row(ATT_W), row(RET_W), row(RET_W), row(MEM_W), _resident((1, RET_W)),
                  _resident(bd.shape), _resident(w.shape), _resident((1, d)), _resident(wgu.shape),
                  _resident(wd.shape)],
        out_specs=row(d),
        out_shape=jax.ShapeDtypeStruct((t, d), F32),
        compiler_params=_cparams(("parallel",)),
        name="mix_out_ffn",
    )(y, att, ret, rg, mo, gn, bd, w, g2, wgu, wd)


def _sample_score_body(pt_ref, qi_ref, w_ref, kin_ref, *rest, pps):
    page_refs, (past_ref, self_ref) = rest[:pps], rest[pps:]
    q8 = qi_ref[...]
    w8 = (w_ref[...] * IDX_HEAD_SCALE) * IDX_SCALE

    def head_sum(lg):
        s = jnp.sum(jnp.maximum(lg, 0.0) * w8, axis=0, keepdims=True)
        return jnp.where(s == 0.0, 0.0, s)

    for i in range(pps):
        lg = jnp.dot(q8, page_refs[i][...].astype(BF16), preferred_element_type=F32)
        past_ref[i:i + 1, :] = head_sum(lg)

    @pl.when(pl.program_id(1) == 0)
    def _():
        kn = kin_ref[...].astype(BF16).astype(F32)
        lg = jnp.sum(q8.astype(F32) * kn, axis=1, keepdims=True)
        self_ref[...] = jnp.broadcast_to(head_sum(lg), self_ref.shape)


def _sample_scores(page_table, qi, wi, ki_new, kidx_t):
    n, n_pages = page_table.shape
    pps = min(2 * PAGES_PER_STEP, n_pages)
    page_specs = [pl.BlockSpec((None, IDX_DIM, PAGE_SIZE),
                               functools.partial(lambda b, g, pt, i: (pt[b, g * pps + i], 0, 0), i=i))
                  for i in range(pps)]
    grid_spec = pltpu.PrefetchScalarGridSpec(
        num_scalar_prefetch=1,
        grid=(n, n_pages // pps),
        in_specs=[pl.BlockSpec((None, N_IDX_HEADS, IDX_DIM), lambda b, g, pt: (b, 0, 0)),
                  pl.BlockSpec((None, N_IDX_HEADS, 1), lambda b, g, pt: (b, 0, 0)),
                  pl.BlockSpec((None, 1, IDX_DIM), lambda b, g, pt: (b, 0, 0))] + page_specs,
        out_specs=[pl.BlockSpec((None, pps, PAGE_SIZE), lambda b, g, pt: (b, g, 0)),
                   pl.BlockSpec((None, 1, LANES), lambda b, g, pt: (b, 0, 0))],
    )
    return pl.pallas_call(
        functools.partial(_sample_score_body, pps=pps),
        grid_spec=grid_spec,
        out_shape=[jax.ShapeDtypeStruct((n, n_pages, PAGE_SIZE), F32),
                   jax.ShapeDtypeStruct((n, 1, LANES), F32)],
        compiler_params=_cparams(("parallel", "arbitrary")),
        name="sample_scores",
    )(page_table, qi.reshape(n, N_IDX_HEADS, IDX_DIM), wi.reshape(n, N_IDX_HEADS, 1),
      ki_new.reshape(n, 1, IDX_DIM), *([kidx_t] * pps))


def _sample_select_body(past_ref, self_ref, thr_ref, cut_ref, *, topk, past_len):
    past = past_ref[...]
    own = self_ref[...][:, 0:1]
    n = past.shape[0]
    kpos = lax.broadcasted_iota(I32, past.shape, 1)

    def count(pred):
        c = jnp.sum(jnp.where(pred(past, kpos), 1.0, 0.0), axis=1, keepdims=True)
        return c + jnp.where(pred(own, past_len), 1.0, 0.0)

    def bisect(i, thr_key):
        cand = thr_key + lax.shift_left(jnp.int32(1), 31 - i)
        cand_f = _from_key(cand)
        return jnp.where(count(lambda x, _: x >= cand_f) >= topk, cand, thr_key)

    thr = _finite_threshold(lax.fori_loop(0, 32, bisect, jnp.full((n, 1), INT_MIN, I32)))
    c_gt = count(lambda x, _: x > thr)
    c_ge = count(lambda x, _: x >= thr)
    need = topk - c_gt
    tie = (c_ge - c_gt) > need
    thr_ref[...] = jnp.broadcast_to(thr, thr_ref.shape)
    cut_ref[...] = jnp.full(cut_ref.shape, past_len + 1, I32)

    @pl.when(jnp.max(jnp.where(tie, 1.0, 0.0)) > 0.0)
    def _():
        nbits = max(1, past_len.bit_length())

        def bisect_pos(i, lo):
            cand = lo + lax.shift_left(jnp.int32(1), nbits - 1 - i)
            cnt = count(lambda x, pos: jnp.where(x == thr, pos, past_len + 1) < cand)
            return jnp.where(cnt < need, cand, lo)

        lo = lax.fori_loop(0, nbits, bisect_pos, jnp.zeros((n, 1), I32))
        cut_ref[...] = jnp.broadcast_to(jnp.where(tie, lo, past_len + 1), cut_ref.shape)


def _sample_select(sc_past, sc_self, topk):
    n, past_len = sc_past.shape
    full = lambda shape: pl.BlockSpec(shape, lambda i: (0,) * len(shape))
    return pl.pallas_call(
        functools.partial(_sample_select_body, topk=topk, past_len=past_len),
        grid=(1,),
        in_specs=[full((n, past_len)), full((n, LANES))],
        out_specs=[full((n, LANES)), full((n, LANES))],
        out_shape=[jax.ShapeDtypeStruct((n, LANES), F32), jax.ShapeDtypeStruct((n, LANES), I32)],
        compiler_params=_cparams(("arbitrary",)),
        name="sample_select",
    )(sc_past, sc_self)


def _select_bias(x, kpos, thr, cut):
    bias = jnp.where(x >= thr, 0.0, NEG_BIG)
    bias = jnp.where(x == jnp.inf, NEG_BIG, bias)
    return jnp.where(x == thr, jnp.where(kpos > cut, NEG_BIG, bias), bias)


def _heads_to_rows(x):
    h, w = x.shape
    return jnp.broadcast_to(x[:, None, :], (h, HEAD_DIM, w)).reshape(h * HEAD_DIM, w)


def _sample_att_body(pt_ref, thr_ref, cut_ref, q_ref, kn_ref, vn_ref, past_ref, self_ref, *rest, pps, past_len):
    k_refs, v_refs = rest[:pps], rest[pps:2 * pps]
    o_ref, m_scr, l_scr, acc_scr = rest[2 * pps:]
    b = pl.program_id(0)
    g = pl.program_id(1)
    thr = thr_ref[b]
    cut = cut_ref[b]

    @pl.when(g == 0)
    def _():
        m_scr[...] = jnp.full(m_scr.shape, NEG_BIG, F32)
        l_scr[...] = jnp.zeros(l_scr.shape, F32)
        acc_scr[...] = jnp.zeros(acc_scr.shape, F32)

    q = jnp.broadcast_to(q_ref[...].astype(F32), (ATT_W, PAGE_SIZE))
    lane = lax.broadcasted_iota(I32, (1, PAGE_SIZE), 1)

    def head_sums(x):
        return x.reshape(N_ATT_HEADS, HEAD_DIM, x.shape[1]).sum(axis=1)

    parts = []
    for i in range(pps):
        kpos = (g * pps + i) * PAGE_SIZE + lane
        bias = _select_bias(past_ref[i:i + 1, :], kpos, thr, cut)
        parts.append(head_sums(k_refs[i][...] * q) + bias)
    m_old = m_scr[...]
    m_new = m_old
    for s in parts:
        m_new = jnp.maximum(m_new, jnp.max(s, axis=1, keepdims=True))
    alpha = jnp.exp(m_old - m_new)
    l_new = alpha * l_scr[...]
    acc = acc_scr[...] * _heads_to_rows(alpha)
    for i in range(pps):
        e = jnp.exp(parts[i] - m_new)
        l_new = l_new + jnp.sum(e, axis=1, keepdims=True)
        acc = acc + _heads_to_rows(e) * v_refs[i][...]
    m_scr[...] = m_new
    l_scr[...] = l_new
    acc_scr[...] = acc

    @pl.when(g == pl.num_programs(1) - 1)
    def _():
        s_self = head_sums(kn_ref[...] * q_ref[...].astype(F32))
        s_self = s_self + _select_bias(self_ref[...][:, 0:1], past_len, thr, cut)
        m_fin = jnp.maximum(m_new, s_self)
        a2 = jnp.exp(m_new - m_fin)
        e_self = jnp.exp(s_self - m_fin)
        l_fin = a2 * l_new + e_self
        num = jnp.sum(acc, axis=1, keepdims=True) * _heads_to_rows(a2) + _heads_to_rows(e_self) * vn_ref[...]
        o_ref[...] = num / _heads_to_rows(l_fin)


def _sample_attend(page_table, thr, cut, q, k_new, v_new, sc_past, sc_self, k_t, v_t):
    n, n_pages = page_table.shape
    pps = min(PAGES_PER_STEP, n_pages)
    past_len = n_pages * PAGE_SIZE
    page_specs = [pl.BlockSpec((None, ATT_W, PAGE_SIZE),
                               functools.partial(lambda b, g, pt, th, cu, i: (pt[b, g * pps + i], 0, 0), i=i))
                  for i in range(pps)]
    col = pl.BlockSpec((None, ATT_W, 1), lambda b, g, pt, th, cu: (b, 0, 0))
    grid_spec = pltpu.PrefetchScalarGridSpec(
        num_scalar_prefetch=3,
        grid=(n, n_pages // pps),
        in_specs=[col, col, col,
                  pl.BlockSpec((None, pps, PAGE_SIZE), lambda b, g, pt, th, cu: (b, g, 0)),
                  pl.BlockSpec((None, 1, LANES), lambda b, g, pt, th, cu: (b, 0, 0))] + page_specs + page_specs,
        out_specs=col,
        scratch_shapes=[pltpu.VMEM((N_ATT_HEADS, 1), F32), pltpu.VMEM((N_ATT_HEADS, 1), F32),
                        pltpu.VMEM((ATT_W, PAGE_SIZE), F32)],
    )
    out = pl.pallas_call(
        functools.partial(_sample_att_body, pps=pps, past_len=past_len),
        grid_spec=grid_spec,
        out_shape=jax.ShapeDtypeStruct((n, ATT_W, 1), F32),
        compiler_params=_cparams(("parallel", "arbitrary")),
        name="sample_attend",
    )(page_table, thr, cut, q.reshape(n, ATT_W, 1), k_new.reshape(n, ATT_W, 1), v_new.reshape(n, ATT_W, 1),
      sc_past, sc_self, *([k_t] * pps), *([v_t] * pps))
    return out.reshape(n, ATT_W)


def _ret_sample_body(st_ref, q_ref, k_ref, v_ref, o_ref, ns_ref):
    state = st_ref[...]
    q = q_ref[...]
    k = k_ref[...]
    v = v_ref[...]
    hh = lax.broadcasted_iota(I32, (N_RET_HEADS, 1, 1), 0)
    g = jnp.zeros((N_RET_HEADS, 1, 1), F32)
    for h in range(N_RET_HEADS):
        g = jnp.where(hh == h, math.exp(LOG_G[h]), g)
    inner = jnp.sum(q * k, axis=1, keepdims=True) * v
    cross = jnp.sum(q * state, axis=1, keepdims=True) * g
    o_ref[...] = inner + cross
    ns_ref[...] = g * state + k * v


def _ret_sample(state, rq, rk, rv):
    n = state.shape[0]
    col = pl.BlockSpec((None, N_RET_HEADS, HEAD_DIM, 1), lambda b: (b, 0, 0, 0))
    rowv = pl.BlockSpec((None, N_RET_HEADS, 1, HEAD_DIM), lambda b: (b, 0, 0, 0))
    st = pl.BlockSpec((None, N_RET_HEADS, HEAD_DIM, HEAD_DIM), lambda b: (b, 0, 0, 0))
    out, new_state = pl.pallas_call(
        _ret_sample_body,
        grid=(n,),
        in_specs=[st, col, col, rowv],
        out_specs=[rowv, st],
        out_shape=[jax.ShapeDtypeStruct((n, N_RET_HEADS, 1, HEAD_DIM), F32),
                   jax.ShapeDtypeStruct(state.shape, F32)],
        compiler_params=_cparams(("parallel",)),
        name="ret_sample",
    )(state, rq.reshape(n, N_RET_HEADS, HEAD_DIM, 1), rk.reshape(n, N_RET_HEADS, HEAD_DIM, 1),
      rv.reshape(n, N_RET_HEADS, 1, HEAD_DIM))
    return out.reshape(n, RET_W), new_state


def _rope_tables(pos):
    half = HEAD_DIM // 2
    inv = ROPE_BASE ** (-jnp.arange(half, dtype=F32) / half)
    ang = pos.astype(F32)[:, None] * inv[None, :]
    cos, sin = jnp.cos(ang), jnp.sin(ang)
    return jnp.concatenate([cos, cos, cos, cos], axis=1), jnp.concatenate([-sin, sin, -sin, sin], axis=1)


def _block_diag_ones(width):
    r = np.arange(width)[:, None] // HEAD_DIM
    c = np.arange(width)[None, :] // HEAD_DIM
    return jnp.asarray((r == c).astype(np.float32), dtype=BF16)


def _repack_w_in(w):
    o = np.cumsum([0, ATT_W, ATT_W, ATT_W, IDXQ_W, IDX_DIM, N_IDX_HEADS, RET_W, RET_W, RET_W, RET_W, MEM_W])
    pad = jnp.zeros((w.shape[0], KW_W - IDX_DIM - N_IDX_HEADS), w.dtype)
    return jnp.concatenate([w[:, o[0]:o[4]], w[:, o[4]:o[6]], pad, w[:, o[6]:o[11]]], axis=1).astype(BF16)


def kernel(x_prompt, x_sample, mem_prompt, cache_k, cache_v, cache_kidx, state_ret, cache_mem_k, cache_mem_v,
           page_table, ffn1_norm_g, ffn1_w_gu, ffn1_w_down, mix_norm_g, w_in, att_q_norm_g, att_k_norm_g,
           ret_gn_g, mem_norm_g, w_mem_kv, mem_q_norm_g, mem_k_norm_g, w_out, ffn2_norm_g, ffn2_w_gu, ffn2_w_down):
    n_b, s_len, d = x_prompt.shape
    n_s, t_s, _ = x_sample.shape
    assert t_s == 1, "the sample group decodes one token per sequence"
    depth = w_in.shape[0]
    n_mem = mem_prompt.shape[1]
    n_pool = cache_k.shape[1]
    past_len = page_table.shape[1] * PAGE_SIZE
    page_table = page_table.astype(I32)

    cos_p, sin_p = _rope_tables(jnp.arange(s_len, dtype=I32))
    cos_s, sin_s = _rope_tables(jnp.full((n_s,), past_len, I32))
    bd = _block_diag_ones(ATT_W)
    bd_ret = _block_diag_ones(RET_W)
    tile8 = lambda g_: jnp.tile(g_, N_ATT_HEADS).reshape(1, ATT_W)
    tile4 = lambda g_: jnp.tile(g_, N_MEM_HEADS).reshape(1, MEM_W)

    yp = x_prompt.reshape(n_b * s_len, d)
    ys = x_sample.reshape(n_s, d)
    mem = mem_prompt.reshape(n_b * n_mem, d)
    outs = [[] for _ in range(10)]
    for l in range(depth):
        g1 = ffn1_norm_g[l].reshape(1, d)
        g2 = ffn2_norm_g[l].reshape(1, d)
        gmix = mix_norm_g[l].reshape(1, d)
        w1gu, w1d = ffn1_w_gu[l].astype(BF16), ffn1_w_down[l].astype(BF16)
        w2gu, w2d = ffn2_w_gu[l].astype(BF16), ffn2_w_down[l].astype(BF16)
        w_in_l = _repack_w_in(w_in[l])
        w_out_l = w_out[l].astype(BF16)
        gq, gk, gm = tile8(att_q_norm_g[l]), tile8(att_k_norm_g[l]), tile4(mem_q_norm_g[l])
        gn = ret_gn_g[l].reshape(1, RET_W)

        yp = _ffn(yp, g1, w1gu, w1d)
        (q, k, kb, v, vt, qi, kw, ki, ki2, rq, rk, rv, rg, mq) = _proj(
            yp, gmix, w_in_l, gq, gk, gm, cos_p, sin_p, bd, n_b)
        mk, mv = _memkv(mem, mem_norm_g[l].reshape(1, d), w_mem_kv[l].astype(BF16), tile4(mem_k_norm_g[l]), bd_ret)
        att = _dsa_prompt_pairs(q, qi, kw, kb, vt, ki2, n_b, s_len)
        ret, s_fin = _ret_prompt(rq, rk, rv, n_b, s_len)
        mo = _mematt_prompt(mq, mk, mv, n_b)
        yp = _mixout_ffn(yp, att, ret, rg, mo, gn, bd_ret, w_out_l, g2, w2gu, w2d)
        outs[0].append(k.reshape(n_b, s_len, N_ATT_HEADS, HEAD_DIM))
        outs[1].append(v.reshape(n_b, s_len, N_ATT_HEADS, HEAD_DIM))
        outs[2].append(ki.reshape(n_b, s_len, IDX_DIM))
        outs[3].append(s_fin)
        outs[4].append(mk.reshape(n_b, n_mem, N_MEM_HEADS, HEAD_DIM))
        outs[5].append(mv.reshape(n_b, n_mem, N_MEM_HEADS, HEAD_DIM))

        ys = _ffn(ys, g1, w1gu, w1d)
        (q, k, kb, v, vt, qi, kw, ki, ki2, rq, rk, rv, rg, mq) = _proj(
            ys, gmix, w_in_l, gq, gk, gm, cos_s, sin_s, bd, 1)
        wi = kw[:, IDX_DIM:IDX_DIM + N_IDX_HEADS]
        sc_past, sc_self = _sample_scores(page_table, qi, wi, ki, jnp.transpose(cache_kidx[l], (0, 2, 1)))
        topk = min(TOPK_MAX, (past_len + t_s) // 4)
        thr, cut = _sample_select(sc_past.reshape(n_s, past_len), sc_self.reshape(n_s, LANES), topk)
        k_t = jnp.transpose(cache_k[l], (0, 2, 3, 1)).reshape(n_pool, ATT_W, PAGE_SIZE)
        v_t = jnp.transpose(cache_v[l], (0, 2, 3, 1)).reshape(n_pool, ATT_W, PAGE_SIZE)
        att = _sample_attend(page_table, thr[:, 0], cut[:, 0], q, k, v, sc_past, sc_self, k_t, v_t)
        ret, s_new = _ret_sample(state_ret[l].astype(F32), rq, rk, rv)
        slot_minor = lambda a: jnp.transpose(a, (0, 2, 3, 1)).reshape(n_s, MEM_W, n_mem)
        mo = _mematt_sample(mq, slot_minor(cache_mem_k[l]), slot_minor(cache_mem_v[l]))
        ys = _mixout_ffn(ys, att, ret, rg, mo, gn, bd_ret, w_out_l, g2, w2gu, w2d)
        outs[6].append(k.reshape(n_s, t_s, N_ATT_HEADS, HEAD_DIM))
        outs[7].append(v.reshape(n_s, t_s, N_ATT_HEADS, HEAD_DIM))
        outs[8].append(ki.reshape(n_s, t_s, IDX_DIM))
        outs[9].append(s_new)

    return (yp.reshape(n_b, s_len, d), ys.reshape(n_s, t_s, d)) + tuple(jnp.stack(o) for o in outs)
```

```python
import functools
import math

import numpy as np
import jax
import jax.numpy as jnp
from jax import lax
from jax.experimental import pallas as pl
from jax.experimental.pallas import tpu as pltpu

F32 = jnp.float32
BF16 = jnp.bfloat16
I32 = jnp.int32

HEAD_DIM = 64
N_ATT_HEADS = 8
N_IDX_HEADS = 8
IDX_DIM = 64
TOPK_MAX = 256
N_RET_HEADS = 4
N_MEM_HEADS = 4
PAGE_SIZE = 128
EPS = 1e-6
ROPE_BASE = 10000.0

ATT_W = N_ATT_HEADS * HEAD_DIM
RET_W = N_RET_HEADS * HEAD_DIM
MEM_W = N_MEM_HEADS * HEAD_DIM
IDXQ_W = N_IDX_HEADS * IDX_DIM
LANES = 128
KW_W = LANES
C_Q = 0
C_K = C_Q + ATT_W
C_V = C_K + ATT_W
C_QI = C_V + ATT_W
C_KW = C_QI + IDXQ_W
C_RQ = C_KW + KW_W
C_RK = C_RQ + RET_W
C_RV = C_RK + RET_W
C_RG = C_RV + RET_W
C_MQ = C_RG + RET_W
PROJ_W = C_MQ + MEM_W

QK_SCALE = HEAD_DIM ** -0.5
IDX_SCALE = IDX_DIM ** -0.5
IDX_HEAD_SCALE = N_IDX_HEADS ** -0.5
NEG_BIG = -1e30
M_INIT = -(2.0 ** 100)
INT_MIN = -(2 ** 31)
F32_LOWEST = float(np.finfo(np.float32).min)
LOG_G = [float(np.log1p(np.float32(-(2.0 ** (-5.0 - h))))) for h in range(N_RET_HEADS)]

VMEM_LIMIT = 56 * 1024 * 1024
ROW_TILE = 512
KEY_CHUNK = 512
Q_TILE = LANES
VT_ROWS = LANES + 16
BISECT_GROUP = 4
RET_CHUNK = 128
PAGES_PER_STEP = 16
SAMPLE_GROUP = 8


def _cparams(sem):
    return pltpu.CompilerParams(dimension_semantics=sem, vmem_limit_bytes=VMEM_LIMIT)


def _resident(shape):
    nd = len(shape)
    return pl.BlockSpec(shape, lambda *_: (0,) * nd, pipeline_mode=pl.Buffered(1))


def _rms(x, g):
    return x * lax.rsqrt(jnp.mean(x * x, axis=-1, keepdims=True) + EPS) * g


def _head_rms(z, g, bd):
    z2 = z * z
    hi = z2.astype(BF16)
    lo = (z2 - hi.astype(F32)).astype(BF16)
    ss = jnp.dot(hi, bd, preferred_element_type=F32) + jnp.dot(lo, bd, preferred_element_type=F32)
    return z * lax.rsqrt(ss * (1.0 / HEAD_DIM) + EPS) * g


def _nt(a, b):
    return lax.dot_general(a, b, (((1,), (1,)), ((), ())), preferred_element_type=F32)


def _from_key(k):
    return lax.bitcast_convert_type(jnp.where(k < 0, k ^ jnp.int32(0x7FFFFFFF), k), F32)


def _finite_threshold(thr_key):
    return jnp.where(thr_key == INT_MIN, F32_LOWEST, jnp.maximum(_from_key(thr_key), F32_LOWEST))


def _ffn_half_step(x, g, wgu_ref, wd_ref, d_ff, fc):
    h = _rms(x, g).astype(BF16)
    acc = jnp.zeros(x.shape, F32)
    for c in range(d_ff // fc):
        gate = jnp.dot(h, wgu_ref[:, c * fc:(c + 1) * fc], preferred_element_type=F32)
        up = jnp.dot(h, wgu_ref[:, d_ff + c * fc:d_ff + (c + 1) * fc], preferred_element_type=F32)
        act = (gate * jax.nn.sigmoid(gate) * up).astype(BF16)
        acc = acc + jnp.dot(act, wd_ref[c * fc:(c + 1) * fc, :], preferred_element_type=F32)
    return x + 0.5 * acc


def _ffn_chunk(d_ff):
    return d_ff // 2 if (d_ff // 2) % LANES == 0 else d_ff


def _ffn_body(x_ref, g_ref, wgu_ref, wd_ref, o_ref, *, d_ff, fc):
    o_ref[...] = _ffn_half_step(x_ref[...], g_ref[...], wgu_ref, wd_ref, d_ff, fc)


def _ffn(x, g, wgu, wd):
    t, d = x.shape
    d_ff = wd.shape[0]
    tm = min(ROW_TILE, t)
    fc = _ffn_chunk(d_ff)
    return pl.pallas_call(
        functools.partial(_ffn_body, d_ff=d_ff, fc=fc),
        grid=(t // tm,),
        in_specs=[pl.BlockSpec((tm, d), lambda i: (i, 0)), _resident((1, d)),
                  _resident(wgu.shape), _resident(wd.shape)],
        out_specs=pl.BlockSpec((tm, d), lambda i: (i, 0)),
        out_shape=jax.ShapeDtypeStruct((t, d), F32),
        compiler_params=_cparams(("parallel",)),
        name="ffn_half",
    )(x, g, wgu, wd)


def _proj_body(x_ref, g_ref, w_ref, gq_ref, gk_ref, gm_ref, cos_ref, sin_ref, bd_ref,
               q_ref, k_ref, kb_ref, v_ref, vt_ref, qi_ref, kw_ref, ki_ref, ki2_ref,
               rq_ref, rk_ref, rv_ref, rg_ref, mq_ref):
    h = _rms(x_ref[...], g_ref[...]).astype(BF16)

    def proj(a, width):
        return jnp.dot(h, w_ref[:, a:a + width], preferred_element_type=F32)

    bd = bd_ref[...]
    q = _head_rms(proj(C_Q, ATT_W), gq_ref[...], bd)
    q_ref[...] = (q * QK_SCALE).astype(BF16)
    k = _head_rms(proj(C_K, ATT_W), gk_ref[...], bd)
    k_ref[...] = k
    kb_ref[...] = k.astype(BF16)
    v = proj(C_V, ATT_W)
    v_ref[...] = v
    vt = jnp.transpose(v)
    tail_row = lax.broadcasted_iota(I32, (VT_ROWS - LANES, vt.shape[1]), 0)
    tail = jnp.where(tail_row == 0, 1.0, 0.0)
    for p in range(N_ATT_HEADS // 2):
        vt_ref[p] = jnp.concatenate([vt[p * LANES:(p + 1) * LANES, :], tail], axis=0).astype(BF16)
    qi_ref[...] = proj(C_QI, IDXQ_W).astype(BF16)
    kw = proj(C_KW, KW_W)
    kw_ref[...] = kw
    ki = kw[:, :IDX_DIM]
    ki_ref[...] = ki
    kib = ki.astype(BF16)
    ki2_ref[...] = jnp.concatenate([kib, kib], axis=1)

    lane = lax.broadcasted_iota(I32, (1, RET_W), 1)
    first_half = (lane % HEAD_DIM) < (HEAD_DIM // 2)
    cos = jnp.concatenate([cos_ref[...]] * (RET_W // LANES), axis=1)
    sin = jnp.concatenate([sin_ref[...]] * (RET_W // LANES), axis=1)

    def rot(x):
        swapped = jnp.where(first_half, pltpu.roll(x, RET_W - HEAD_DIM // 2, axis=1),
                            pltpu.roll(x, HEAD_DIM // 2, axis=1))
        return x * cos + swapped * sin

    rq_ref[...] = rot(proj(C_RQ, RET_W))
    rk_ref[...] = rot(proj(C_RK, RET_W)) * QK_SCALE
    rv_ref[...] = proj(C_RV, RET_W)
    rg_ref[...] = proj(C_RG, RET_W)
    mq = _head_rms(proj(C_MQ, MEM_W), gm_ref[...], bd[:MEM_W, :MEM_W])
    mq_ref[...] = (mq * QK_SCALE).astype(BF16)


def _proj(y, g, w, gq, gk, gm, cos, sin, bd, n_batch):
    t, d = y.shape
    s_len = t // n_batch
    tm = min(ROW_TILE, s_len)
    nsb = s_len // tm
    row = lambda w_: pl.BlockSpec((tm, w_), lambda i: (i, 0))
    tab = pl.BlockSpec((tm, LANES), lambda i: (i % nsb, 0))
    out_shapes = [
        (jax.ShapeDtypeStruct((t, ATT_W), BF16), row(ATT_W)),
        (jax.ShapeDtypeStruct((t, ATT_W), F32), row(ATT_W)),
        (jax.ShapeDtypeStruct((t, ATT_W), BF16), row(ATT_W)),
        (jax.ShapeDtypeStruct((t, ATT_W), F32), row(ATT_W)),
        (jax.ShapeDtypeStruct((n_batch, nsb, N_ATT_HEADS // 2, VT_ROWS, tm), BF16),
         pl.BlockSpec((None, None, N_ATT_HEADS // 2, VT_ROWS, tm),
                      lambda i: (i // nsb, i % nsb, 0, 0, 0))),
        (jax.ShapeDtypeStruct((t, IDXQ_W), BF16), row(IDXQ_W)),
        (jax.ShapeDtypeStruct((t, KW_W), F32), row(KW_W)),
        (jax.ShapeDtypeStruct((t, IDX_DIM), F32), row(IDX_DIM)),
        (jax.ShapeDtypeStruct((t, 2 * IDX_DIM), BF16), row(2 * IDX_DIM)),
        (jax.ShapeDtypeStruct((t, RET_W), F32), row(RET_W)),
        (jax.ShapeDtypeStruct((t, RET_W), F32), row(RET_W)),
        (jax.ShapeDtypeStruct((t, RET_W), F32), row(RET_W)),
        (jax.ShapeDtypeStruct((t, RET_W), F32), row(RET_W)),
        (jax.ShapeDtypeStruct((t, MEM_W), BF16), row(MEM_W)),
    ]
    return pl.pallas_call(
        _proj_body,
        grid=(t // tm,),
        in_specs=[row(d), _resident((1, d)), _resident(w.shape), _resident((1, ATT_W)),
                  _resident((1, ATT_W)), _resident((1, MEM_W)), tab, tab, _resident(bd.shape)],
        out_specs=[s for _, s in out_shapes],
        out_shape=[s for s, _ in out_shapes],
        compiler_params=_cparams(("parallel",)),
        name="mix_proj",
    )(y, g, w, gq, gk, gm, cos, sin, bd)


def _memkv_body(x_ref, g_ref, w_ref, gk_ref, bd_ref, mk_ref, mv_ref):
    h = _rms(x_ref[...], g_ref[...]).astype(BF16)
    z = jnp.dot(h, w_ref[...], preferred_element_type=F32)
    mk_ref[...] = _head_rms(z[:, :MEM_W], gk_ref[...], bd_ref[...])
    mv_ref[...] = z[:, MEM_W:]


def _memkv(mem, g, w, gk, bd):
    t, d = mem.shape
    tm = min(ROW_TILE, t)
    row = lambda w_: pl.BlockSpec((tm, w_), lambda i: (i, 0))
    return pl.pallas_call(
        _memkv_body,
        grid=(t // tm,),
        in_specs=[row(d), _resident((1, d)), _resident(w.shape), _resident((1, MEM_W)), _resident(bd.shape)],
        out_specs=[row(MEM_W), row(MEM_W)],
        out_shape=[jax.ShapeDtypeStruct((t, MEM_W), F32)] * 2,
        compiler_params=_cparams(("parallel",)),
        name="mem_kv",
    )(mem, g, w, gk, bd)


def _dsa_pair_body(q_ref, qi_ref, kw_ref, k_ref, vt_ref, ki2_ref, o_ref,
                   key_scr, qi2_scr, q2_scr, cut_scr, m_scr, acc_scr, sa_scr, sb_scr, *, topk, sc, s_len):
    n_st = 2
    jj = pl.program_id(1)
    n_ch = ((n_st * jj + 1) * Q_TILE + Q_TILE + sc - 1) // sc
    n_pair = N_ATT_HEADS // 2
    lane = lax.broadcasted_iota(I32, (1, LANES), 1)
    qpos = [(n_st * jj + s) * Q_TILE + lane for s in range(n_st)]
    kiota = lax.broadcasted_iota(I32, (sc, LANES), 0)
    streams = range(n_st)

    r2 = lax.broadcasted_iota(I32, (2 * Q_TILE, LANES), 0)
    c2 = lax.broadcasted_iota(I32, (2 * Q_TILE, LANES), 1)
    keep = (r2 < Q_TILE) == (c2 < HEAD_DIM)
    eye2 = jnp.where(r2 % Q_TILE == c2, 1.0, 0.0).astype(BF16)
    w_s = []
    for s in streams:
        rows = slice(s * Q_TILE, (s + 1) * Q_TILE)
        for p in range(n_pair):
            a = qi_ref[rows, p * LANES:(p + 1) * LANES].astype(F32)
            qi2_scr[s, p] = jnp.where(keep, jnp.concatenate([a, a], axis=0), 0.0).astype(BF16)
            a = q_ref[rows, p * LANES:(p + 1) * LANES].astype(F32)
            q2 = jnp.where(keep, jnp.concatenate([a, a], axis=0), 0.0).astype(BF16)
            q2_scr[s, p] = jnp.concatenate([q2, eye2], axis=1)
        w_t = jnp.transpose(kw_ref[rows, :])[IDX_DIM:IDX_DIM + N_IDX_HEADS, :]
        w_s.append((w_t * IDX_HEAD_SCALE) * IDX_SCALE)

    def score_chunk(c, carry):
        off = pl.multiple_of(c * sc, sc)
        kc = ki2_ref[pl.ds(off, sc), :]
        acc = [jnp.zeros((sc, LANES), F32) for _ in streams]
        for p in range(n_pair):
            for s in streams:
                lg = _nt(kc, qi2_scr[s, p])
                acc[s] = acc[s] + jnp.maximum(lg[:, :LANES], 0.0) * w_s[s][2 * p:2 * p + 1, :]
                acc[s] = acc[s] + jnp.maximum(lg[:, LANES:], 0.0) * w_s[s][2 * p + 1:2 * p + 2, :]
        for s in streams:
            key_scr[s, pl.ds(off, sc), :] = jnp.where(off + kiota <= qpos[s], acc[s], -jnp.inf)
        return carry

    lax.fori_loop(0, n_ch, score_chunk, 0)

    def count(preds):
        def body(c, accs):
            off = pl.multiple_of(c * sc, sc)
            accs = list(accs)
            for g in range(sc // 64):
                rows = pl.ds(off + g * 64, 64)
                for s in streams:
                    hit = preds[s](key_scr[s, rows, :], off + kiota[g * 64:(g + 1) * 64])
                    accs[s] = jnp.where(hit, accs[s] + 1.0, accs[s])
            return tuple(accs)
        accs = lax.fori_loop(0, n_ch, body, tuple(jnp.zeros((64, LANES), F32) for _ in streams))
        return [a.reshape(8, 8, LANES).sum(axis=0).sum(axis=0, keepdims=True) for a in accs]

    def bisect_group(state):
        i, thrs, actives, _ = state
        thrs, actives = list(thrs), list(actives)
        for _ in range(BISECT_GROUP):
            cands = [t + lax.shift_left(jnp.int32(1), 31 - i) for t in thrs]
            cnts = count([functools.partial(lambda x, _, cand: x >= cand, cand=_from_key(cand)) for cand in cands])
            for s in streams:
                thrs[s] = jnp.where((cnts[s] >= topk) & (actives[s] > 0.0), cands[s], thrs[s])
                actives[s] = jnp.where(cnts[s] == topk, 0.0, actives[s])
            i = i + 1
        return i, tuple(thrs), tuple(actives), jnp.max(jnp.maximum(actives[0], actives[1]))

    actives0 = tuple(jnp.where(qpos[s] + 1 <= topk, 0.0, 1.0) for s in streams)
    _, thrs, actives, n_active = lax.while_loop(
        lambda st: (st[0] < 32) & (st[3] > 0.0), bisect_group,
        (jnp.int32(0), tuple(jnp.full((1, LANES), INT_MIN, I32) for _ in streams), actives0,
         jnp.max(jnp.maximum(actives0[0], actives0[1]))))
    for s in streams:
        cut_scr[s] = jnp.full((1, LANES), s_len, I32)

    thr_f = [_finite_threshold(t) for t in thrs]

    @pl.when(n_active > 0.0)
    def _():
        c_gt = count([functools.partial(lambda x, _, t: x > t, t=t) for t in thr_f])
        c_ge = count([functools.partial(lambda x, _, t: x >= t, t=t) for t in thr_f])
        need = [topk - c for c in c_gt]
        tie = [((c_ge[s] - c_gt[s]) > need[s]) & (actives[s] > 0.0) for s in streams]
        nbits = max(1, (s_len - 1).bit_length())

        def bisect_pos(i, los):
            cands = [lo + lax.shift_left(jnp.int32(1), nbits - 1 - i) for lo in los]
            cnts = count([functools.partial(lambda x, kpos, t, cand: jnp.where(x == t, kpos, s_len) < cand,
                                            t=thr_f[s], cand=cands[s]) for s in streams])
            return tuple(jnp.where(cnts[s] < need[s], cands[s], los[s]) for s in streams)

        los = lax.fori_loop(0, nbits, bisect_pos, tuple(jnp.zeros((1, LANES), I32) for _ in streams))
        for s in streams:
            cut_scr[s] = jnp.where(tie[s], los[s], s_len)

    cuts = [cut_scr[s] for s in streams]

    m_scr[...] = jnp.full(m_scr.shape, M_INIT, F32)
    acc_scr[...] = jnp.zeros(acc_scr.shape, F32)

    def qk_chunk(c, s_scr):
        off = pl.multiple_of(c * sc, sc)
        biases = []
        for s in streams:
            bias = _select_bias(key_scr[s, pl.ds(off, sc), :], off + kiota, thr_f[s], cuts[s])
            biases.append(bias.astype(BF16))
        for p in range(n_pair):
            kp = k_ref[pl.ds(off, sc), p * LANES:(p + 1) * LANES]
            for s in streams:
                s_scr[s, p] = _nt(jnp.concatenate([kp, biases[s]], axis=1), q2_scr[s, p])

    def softmax_pv_chunk(c, s_scr):
        es, alphas = {}, {}
        for p in range(n_pair):
            for s in streams:
                x = s_scr[s, p].astype(BF16)
                m_c = x.reshape(sc // 64, 64, 2 * Q_TILE).max(axis=0)
                m_c = m_c.reshape(4, 16, 2 * Q_TILE).max(axis=0).astype(F32).max(axis=0, keepdims=True)
                m_old = m_scr[s, p]
                m_new = jnp.maximum(m_old, m_c)
                alphas[s, p] = jnp.exp(m_old - m_new)
                es[s, p] = jnp.exp(x - m_new.astype(BF16))
                m_scr[s, p] = m_new
            for s in streams:
                pv = jnp.dot(vt_ref[c, p], es[s, p], preferred_element_type=F32)
                acc_scr[s, p] = acc_scr[s, p] * alphas[s, p] + pv

    def att_two_chunks(i, carry):
        c = 2 * i
        qk_chunk(c + 1, sb_scr)
        softmax_pv_chunk(c, sa_scr)
        qk_chunk(c + 2, sa_scr)
        softmax_pv_chunk(c + 1, sb_scr)
        return carry

    qk_chunk(0, sa_scr)
    n_two = (n_ch - 1) // 2
    lax.fori_loop(0, n_two, att_two_chunks, 0)

    @pl.when(n_ch % 2 == 0)
    def _():
        qk_chunk(n_ch - 1, sb_scr)
        softmax_pv_chunk(n_ch - 2, sa_scr)
        softmax_pv_chunk(n_ch - 1, sb_scr)

    @pl.when(n_ch % 2 == 1)
    def _():
        softmax_pv_chunk(n_ch - 1, sa_scr)

    for s in streams:
        outs = []
        for p in range(n_pair):
            a = acc_scr[s, p]
            l = a[LANES:LANES + 1, :]
            outs.append(a[:HEAD_DIM, :LANES] / l[:, :LANES])
            outs.append(a[HEAD_DIM:LANES, LANES:] / l[:, LANES:])
        o_ref[s * Q_TILE:(s + 1) * Q_TILE, :] = jnp.transpose(jnp.concatenate(outs, axis=0))


def _dsa_prompt_pairs(q, qi, kw, kb, vt, ki2, n_batch, s_len):
    t = q.shape[0]
    sc = min(KEY_CHUNK, s_len)
    n_st = 2
    nq = s_len // (n_st * Q_TILE)
    n_pair = N_ATT_HEADS // 2
    topk = min(TOPK_MAX, s_len // 4)
    qrow = lambda w_: pl.BlockSpec((n_st * Q_TILE, w_), lambda b, j: (b * nq + j, 0))
    per_batch = lambda w_: pl.BlockSpec((s_len, w_), lambda b, j: (b, 0), pipeline_mode=pl.Buffered(1))
    return pl.pallas_call(
        functools.partial(_dsa_pair_body, topk=topk, sc=sc, s_len=s_len),
        grid=(n_batch, nq),
        in_specs=[qrow(ATT_W), qrow(IDXQ_W), qrow(KW_W), per_batch(ATT_W),
                  pl.BlockSpec((None, s_len // sc, n_pair, VT_ROWS, sc), lambda b, j: (b, 0, 0, 0, 0),
                               pipeline_mode=pl.Buffered(1)),
                  per_batch(2 * IDX_DIM)],
        out_specs=qrow(ATT_W),
        out_shape=jax.ShapeDtypeStruct((t, ATT_W), F32),
        scratch_shapes=[pltpu.VMEM((n_st, s_len, LANES), F32),
                        pltpu.VMEM((n_st, n_pair, 2 * Q_TILE, LANES), BF16),
                        pltpu.VMEM((n_st, n_pair, 2 * Q_TILE, 2 * LANES), BF16),
                        pltpu.VMEM((n_st, 1, LANES), I32),
                        pltpu.VMEM((n_st, n_pair, 1, 2 * Q_TILE), F32),
                        pltpu.VMEM((n_st, n_pair, VT_ROWS, 2 * Q_TILE), F32),
                        pltpu.VMEM((n_st, n_pair, sc, 2 * Q_TILE), F32),
                        pltpu.VMEM((n_st, n_pair, sc, 2 * Q_TILE), F32)],
        compiler_params=_cparams(("parallel", "arbitrary")),
        name="dsa_prompt",
    )(q, qi, kw, kb, vt, ki2)


def _ret_lane_const(vals):
    lane = lax.broadcasted_iota(I32, (1, RET_W), 1)
    out = jnp.zeros((1, RET_W), F32)
    for h, v in enumerate(vals):
        out = jnp.where(lane // HEAD_DIM == h, v, out)
    return out


def _ret_body(rq_ref, rk_ref, rv_ref, o_ref, st_ref, sbd_scr, *, ch):
    c = pl.program_id(0)
    n_seq = sbd_scr.shape[0]

    @pl.when(c == 0)
    def _():
        sbd_scr[...] = jnp.zeros(sbd_scr.shape, F32)

    head = lax.broadcasted_iota(I32, (1, RET_W), 1) // HEAD_DIM
    log_g = _ret_lane_const(LOG_G)
    i_col = lax.broadcasted_iota(I32, (ch, 1), 0).astype(F32)
    q_dec = jnp.exp(log_g * (i_col + 1.0))
    k_dec = jnp.exp(log_g * (ch - 1.0 - i_col))
    ii = lax.broadcasted_iota(I32, (ch, ch), 0)
    jj = lax.broadcasted_iota(I32, (ch, ch), 1)
    causal = ii >= jj
    diff = jnp.where(causal, ii - jj, 0).astype(F32)
    decays = [jnp.where(causal, jnp.exp(LOG_G[h] * diff), 0.0) for h in range(N_RET_HEADS)]
    rh = lax.broadcasted_iota(I32, (RET_W, RET_W), 0) // HEAD_DIM
    ch_ = lax.broadcasted_iota(I32, (RET_W, RET_W), 1) // HEAD_DIM

    for b in range(n_seq):
        q = rq_ref[b]
        k = rk_ref[b]
        vb = rv_ref[b].astype(BF16)
        kb = k.astype(BF16)
        state = sbd_scr[b]
        cross = jnp.dot(q.astype(BF16), state.astype(BF16), preferred_element_type=F32) * q_dec
        inner = jnp.zeros((ch, RET_W), F32)
        for h in range(N_RET_HEADS):
            qm = jnp.where(head == h, q, 0.0).astype(BF16)
            sc = (_nt(qm, kb) * decays[h]).astype(BF16)
            inner = inner + jnp.where(head == h, jnp.dot(sc, vb, preferred_element_type=F32), 0.0)
        o_ref[b] = inner + cross

        kd = (k * k_dec).astype(BF16)
        kv = lax.dot_general(kd, vb, (((0,), (0,)), ((), ())), preferred_element_type=F32)
        new_state = state * jnp.exp(log_g * float(ch)) + jnp.where(rh == ch_, kv, 0.0)
        sbd_scr[b] = new_state

    @pl.when(c == pl.num_programs(0) - 1)
    def _():
        for b in range(n_seq):
            for h in range(N_RET_HEADS):
                st_ref[b, h] = sbd_scr[b, h * HEAD_DIM:(h + 1) * HEAD_DIM, h * HEAD_DIM:(h + 1) * HEAD_DIM]


def _ret_prompt(rq, rk, rv, n_batch, s_len):
    ch = min(RET_CHUNK, s_len)
    blk = pl.BlockSpec((n_batch, ch, RET_W), lambda c: (0, c, 0))
    as3d = lambda a: a.reshape(n_batch, s_len, RET_W)
    out, state = pl.pallas_call(
        functools.partial(_ret_body, ch=ch),
        grid=(s_len // ch,),
        in_specs=[blk, blk, blk],
        out_specs=[blk, pl.BlockSpec((n_batch, N_RET_HEADS, HEAD_DIM, HEAD_DIM), lambda c: (0, 0, 0, 0))],
        out_shape=[jax.ShapeDtypeStruct((n_batch, s_len, RET_W), F32),
                   jax.ShapeDtypeStruct((n_batch, N_RET_HEADS, HEAD_DIM, HEAD_DIM), F32)],
        scratch_shapes=[pltpu.VMEM((n_batch, RET_W, RET_W), F32)],
        compiler_params=_cparams(("arbitrary",)),
        name="ret_prompt",
    )(as3d(rq), as3d(rk), as3d(rv))
    return out.reshape(n_batch * s_len, RET_W), state


def _softmax_lanes(s):
    m = jnp.max(s, axis=-1, keepdims=True)
    e = jnp.exp(s - m)
    return e / jnp.sum(e, axis=-1, keepdims=True)


def _mematt_body(mq_ref, mk_ref, mv_ref, o_ref):
    mq = mq_ref[...].astype(F32)
    mk = mk_ref[...].astype(BF16)
    mv = mv_ref[...].astype(BF16)
    head = lax.broadcasted_iota(I32, (1, MEM_W), 1) // HEAD_DIM
    out = jnp.zeros(mq.shape, F32)
    for h in range(N_MEM_HEADS):
        qm = jnp.where(head == h, mq, 0.0).astype(BF16)
        p = _softmax_lanes(_nt(qm, mk)).astype(BF16)
        out = out + jnp.where(head == h, jnp.dot(p, mv, preferred_element_type=F32), 0.0)
    o_ref[...] = out


def _mematt_prompt(mq, mk, mv, n_batch):
    t = mq.shape[0]
    n_mem = mk.shape[0] // n_batch
    s_len = t // n_batch
    tm = min(ROW_TILE, s_len)
    nsb = s_len // tm
    row = pl.BlockSpec((tm, MEM_W), lambda i: (i, 0))
    mem = pl.BlockSpec((n_mem, MEM_W), lambda i: (i // nsb, 0))
    return pl.pallas_call(
        _mematt_body,
        grid=(t // tm,),
        in_specs=[row, mem, mem],
        out_specs=row,
        out_shape=jax.ShapeDtypeStruct((t, MEM_W), F32),
        compiler_params=_cparams(("parallel",)),
        name="mem_attend",
    )(mq, mk, mv)


def _mematt_sample_body(mq_ref, mk_ref, mv_ref, o_ref):
    rows = 8
    sel = lax.broadcasted_iota(I32, (rows, MEM_W), 1) // HEAD_DIM == lax.broadcasted_iota(I32, (rows, MEM_W), 0)
    for t in range(mq_ref.shape[0]):
        mq = jnp.broadcast_to(mq_ref[t].astype(F32), (rows, MEM_W))
        qbd = jnp.where(sel, mq, 0.0).astype(BF16)
        p = _softmax_lanes(jnp.dot(qbd, mk_ref[t].astype(BF16), preferred_element_type=F32)).astype(BF16)
        o = _nt(p, mv_ref[t].astype(BF16))
        o_ref[t] = jnp.sum(jnp.where(sel, o, 0.0), axis=0, keepdims=True)


def _mematt_sample(mq, mk_t, mv_t):
    n, n_mem = mk_t.shape[0], mk_t.shape[2]
    grp = SAMPLE_GROUP if n % SAMPLE_GROUP == 0 else 1
    one = pl.BlockSpec((grp, 1, MEM_W), lambda b: (b, 0, 0))
    mem = pl.BlockSpec((grp, MEM_W, n_mem), lambda b: (b, 0, 0))
    out = pl.pallas_call(
        _mematt_sample_body,
        grid=(n // grp,),
        in_specs=[one, mem, mem],
        out_specs=one,
        out_shape=jax.ShapeDtypeStruct((n, 1, MEM_W), F32),
        compiler_params=_cparams(("parallel",)),
        name="mem_attend_sample",
    )(mq.reshape(n, 1, MEM_W), mk_t, mv_t)
    return out.reshape(n, MEM_W)


def _mixout_ffn_body(y_ref, att_ref, ret_ref, rg_ref, mo_ref, gn_ref, bd_ref, w_ref, g2_ref, wgu_ref, wd_ref,
                     o_ref, *, d_ff, fc):
    retn = _head_rms(ret_ref[...], gn_ref[...], bd_ref[...])
    rg = rg_ref[...]
    gated = (rg * jax.nn.sigmoid(rg)) * retn
    acc = jnp.dot(att_ref[...].astype(BF16), w_ref[:ATT_W, :], preferred_element_type=F32)
    acc = acc + jnp.dot(gated.astype(BF16), w_ref[ATT_W:ATT_W + RET_W, :], preferred_element_type=F32)
    acc = acc + jnp.dot(mo_ref[...].astype(BF16), w_ref[ATT_W + RET_W:, :], preferred_element_type=F32)
    o_ref[...] = _ffn_half_step(y_ref[...] + acc, g2_ref[...], wgu_ref, wd_ref, d_ff, fc)


def _mixout_ffn(y, att, ret, rg, mo, gn, bd, w, g2, wgu, wd):
    t, d = y.shape
    d_ff = wd.shape[0]
    tm = min(ROW_TILE, t)
    row = lambda w_: pl.BlockSpec((tm, w_), lambda i: (i, 0))
    return pl.pallas_call(
        functools.partial(_mixout_ffn_body, d_ff=d_ff, fc=_ffn_chunk(d_ff)),
        grid=(t // tm,),
        in_specs=[row(d), row(ATT_W), row(RET_W), row(RET_W), row(MEM_W), _resident((1, RET_W)),
                  _resident(bd.shape), _resident(w.shape), _resident((1, d)), _resident(wgu.shape),
                  _resident(wd.shape)],
        out_specs=row(d),
        out_shape=jax.ShapeDtypeStruct((t, d), F32),
        compiler_params=_cparams(("parallel",)),
        name="mix_out_ffn",
    )(y, att, ret, rg, mo, gn, bd, w, g2, wgu, wd)


def _sample_score_body(pt_ref, qi_ref, w_ref, kin_ref, *rest, pps):
    page_refs, (past_ref, self_ref) = rest[:pps], rest[pps:]
    q8 = qi_ref[...]
    w8 = (w_ref[...] * IDX_HEAD_SCALE) * IDX_SCALE

    def head_sum(lg):
        s = jnp.sum(jnp.maximum(lg, 0.0) * w8, axis=0, keepdims=True)
        return jnp.where(s == 0.0, 0.0, s)

    for i in range(pps):
        lg = jnp.dot(q8, page_refs[i][...].astype(BF16), preferred_element_type=F32)
        past_ref[i:i + 1, :] = head_sum(lg)

    @pl.when(pl.program_id(1) == 0)
    def _():
        kn = kin_ref[...].astype(BF16).astype(F32)
        lg = jnp.sum(q8.astype(F32) * kn, axis=1, keepdims=True)
        self_ref[...] = jnp.broadcast_to(head_sum(lg), self_ref.shape)


def _sample_scores(page_table, qi, wi, ki_new, kidx_t):
    n, n_pages = page_table.shape
    pps = min(2 * PAGES_PER_STEP, n_pages)
    page_specs = [pl.BlockSpec((None, IDX_DIM, PAGE_SIZE),
                               functools.partial(lambda b, g, pt, i: (pt[b, g * pps + i], 0, 0), i=i))
                  for i in range(pps)]
    grid_spec = pltpu.PrefetchScalarGridSpec(
        num_scalar_prefetch=1,
        grid=(n, n_pages // pps),
        in_specs=[pl.BlockSpec((None, N_IDX_HEADS, IDX_DIM), lambda b, g, pt: (b, 0, 0)),
                  pl.BlockSpec((None, N_IDX_HEADS, 1), lambda b, g, pt: (b, 0, 0)),
                  pl.BlockSpec((None, 1, IDX_DIM), lambda b, g, pt: (b, 0, 0))] + page_specs,
        out_specs=[pl.BlockSpec((None, pps, PAGE_SIZE), lambda b, g, pt: (b, g, 0)),
                   pl.BlockSpec((None, 1, LANES), lambda b, g, pt: (b, 0, 0))],
    )
    return pl.pallas_call(
        functools.partial(_sample_score_body, pps=pps),
        grid_spec=grid_spec,
        out_shape=[jax.ShapeDtypeStruct((n, n_pages, PAGE_SIZE), F32),
                   jax.ShapeDtypeStruct((n, 1, LANES), F32)],
        compiler_params=_cparams(("parallel", "arbitrary")),
        name="sample_scores",
    )(page_table, qi.reshape(n, N_IDX_HEADS, IDX_DIM), wi.reshape(n, N_IDX_HEADS, 1),
      ki_new.reshape(n, 1, IDX_DIM), *([kidx_t] * pps))


def _sample_select_body(past_ref, self_ref, thr_ref, cut_ref, *, topk, past_len):
    past = past_ref[...]
    own = self_ref[...][:, 0:1]
    n = past.shape[0]
    kpos = lax.broadcasted_iota(I32, past.shape, 1)

    def count(pred):
        c = jnp.sum(jnp.where(pred(past, kpos), 1.0, 0.0), axis=1, keepdims=True)
        return c + jnp.where(pred(own, past_len), 1.0, 0.0)

    def bisect(i, thr_key):
        cand = thr_key + lax.shift_left(jnp.int32(1), 31 - i)
        cand_f = _from_key(cand)
        return jnp.where(count(lambda x, _: x >= cand_f) >= topk, cand, thr_key)

    thr = _finite_threshold(lax.fori_loop(0, 32, bisect, jnp.full((n, 1), INT_MIN, I32)))
    c_gt = count(lambda x, _: x > thr)
    c_ge = count(lambda x, _: x >= thr)
    need = topk - c_gt
    tie = (c_ge - c_gt) > need
    thr_ref[...] = jnp.broadcast_to(thr, thr_ref.shape)
    cut_ref[...] = jnp.full(cut_ref.shape, past_len + 1, I32)

    @pl.when(jnp.max(jnp.where(tie, 1.0, 0.0)) > 0.0)
    def _():
        nbits = max(1, past_len.bit_length())

        def bisect_pos(i, lo):
            cand = lo + lax.shift_left(jnp.int32(1), nbits - 1 - i)
            cnt = count(lambda x, pos: jnp.where(x == thr, pos, past_len + 1) < cand)
            return jnp.where(cnt < need, cand, lo)

        lo = lax.fori_loop(0, nbits, bisect_pos, jnp.zeros((n, 1), I32))
        cut_ref[...] = jnp.broadcast_to(jnp.where(tie, lo, past_len + 1), cut_ref.shape)


def _sample_select(sc_past, sc_self, topk):
    n, past_len = sc_past.shape
    full = lambda shape: pl.BlockSpec(shape, lambda i: (0,) * len(shape))
    return pl.pallas_call(
        functools.partial(_sample_select_body, topk=topk, past_len=past_len),
        grid=(1,),
        in_specs=[full((n, past_len)), full((n, LANES))],
        out_specs=[full((n, LANES)), full((n, LANES))],
        out_shape=[jax.ShapeDtypeStruct((n, LANES), F32), jax.ShapeDtypeStruct((n, LANES), I32)],
        compiler_params=_cparams(("arbitrary",)),
        name="sample_select",
    )(sc_past, sc_self)


def _select_bias(x, kpos, thr, cut):
    bias = jnp.where(x >= thr, 0.0, NEG_BIG)
    bias = jnp.where(x == jnp.inf, NEG_BIG, bias)
    return jnp.where(x == thr, jnp.where(kpos > cut, NEG_BIG, bias), bias)


def _heads_to_rows(x):
    h, w = x.shape
    return jnp.broadcast_to(x[:, None, :], (h, HEAD_DIM, w)).reshape(h * HEAD_DIM, w)


def _sample_att_body(pt_ref, thr_ref, cut_ref, q_ref, kn_ref, vn_ref, past_ref, self_ref, *rest, pps, past_len):
    k_refs, v_refs = rest[:pps], rest[pps:2 * pps]
    o_ref, m_scr, l_scr, acc_scr = rest[2 * pps:]
    b = pl.program_id(0)
    g = pl.program_id(1)
    thr = thr_ref[b]
    cut = cut_ref[b]

    @pl.when(g == 0)
    def _():
        m_scr[...] = jnp.full(m_scr.shape, NEG_BIG, F32)
        l_scr[...] = jnp.zeros(l_scr.shape, F32)
        acc_scr[...] = jnp.zeros(acc_scr.shape, F32)

    q = jnp.broadcast_to(q_ref[...].astype(F32), (ATT_W, PAGE_SIZE))
    lane = lax.broadcasted_iota(I32, (1, PAGE_SIZE), 1)

    def head_sums(x):
        return x.reshape(N_ATT_HEADS, HEAD_DIM, x.shape[1]).sum(axis=1)

    parts = []
    for i in range(pps):
        kpos = (g * pps + i) * PAGE_SIZE + lane
        bias = _select_bias(past_ref[i:i + 1, :], kpos, thr, cut)
        parts.append(head_sums(k_refs[i][...] * q) + bias)
    m_old = m_scr[...]
    m_new = m_old
    for s in parts:
        m_new = jnp.maximum(m_new, jnp.max(s, axis=1, keepdims=True))
    alpha = jnp.exp(m_old - m_new)
    l_new = alpha * l_scr[...]
    acc = acc_scr[...] * _heads_to_rows(alpha)
    for i in range(pps):
        e = jnp.exp(parts[i] - m_new)
        l_new = l_new + jnp.sum(e, axis=1, keepdims=True)
        acc = acc + _heads_to_rows(e) * v_refs[i][...]
    m_scr[...] = m_new
    l_scr[...] = l_new
    acc_scr[...] = acc

    @pl.when(g == pl.num_programs(1) - 1)
    def _():
        s_self = head_sums(kn_ref[...] * q_ref[...].astype(F32))
        s_self = s_self + _select_bias(self_ref[...][:, 0:1], past_len, thr, cut)
        m_fin = jnp.maximum(m_new, s_self)
        a2 = jnp.exp(m_new - m_fin)
        e_self = jnp.exp(s_self - m_fin)
        l_fin = a2 * l_new + e_self
        num = jnp.sum(acc, axis=1, keepdims=True) * _heads_to_rows(a2) + _heads_to_rows(e_self) * vn_ref[...]
        o_ref[...] = num / _heads_to_rows(l_fin)


def _sample_attend(page_table, thr, cut, q, k_new, v_new, sc_past, sc_self, k_t, v_t):
    n, n_pages = page_table.shape
    pps = min(PAGES_PER_STEP, n_pages)
    past_len = n_pages * PAGE_SIZE
    page_specs = [pl.BlockSpec((None, ATT_W, PAGE_SIZE),
                               functools.partial(lambda b, g, pt, th, cu, i: (pt[b, g * pps + i], 0, 0), i=i))
                  for i in range(pps)]
    col = pl.BlockSpec((None, ATT_W, 1), lambda b, g, pt, th, cu: (b, 0, 0))
    grid_spec = pltpu.PrefetchScalarGridSpec(
        num_scalar_prefetch=3,
        grid=(n, n_pages // pps),
        in_specs=[col, col, col,
                  pl.BlockSpec((None, pps, PAGE_SIZE), lambda b, g, pt, th, cu: (b, g, 0)),
                  pl.BlockSpec((None, 1, LANES), lambda b, g, pt, th, cu: (b, 0, 0))] + page_specs + page_specs,
        out_specs=col,
        scratch_shapes=[pltpu.VMEM((N_ATT_HEADS, 1), F32), pltpu.VMEM((N_ATT_HEADS, 1), F32),
                        pltpu.VMEM((ATT_W, PAGE_SIZE), F32)],
    )
    out = pl.pallas_call(
        functools.partial(_sample_att_body, pps=pps, past_len=past_len),
        grid_spec=grid_spec,
        out_shape=jax.ShapeDtypeStruct((n, ATT_W, 1), F32),
        compiler_params=_cparams(("parallel", "arbitrary")),
        name="sample_attend",
    )(page_table, thr, cut, q.reshape(n, ATT_W, 1), k_new.reshape(n, ATT_W, 1), v_new.reshape(n, ATT_W, 1),
      sc_past, sc_self, *([k_t] * pps), *([v_t] * pps))
    return out.reshape(n, ATT_W)


def _ret_sample_body(st_ref, q_ref, k_ref, v_ref, o_ref, ns_ref):
    state = st_ref[...]
    q = q_ref[...]
    k = k_ref[...]
    v = v_ref[...]
    hh = lax.broadcasted_iota(I32, (1, N_RET_HEADS, 1, 1), 1)
    g = jnp.zeros((1, N_RET_HEADS, 1, 1), F32)
    for h in range(N_RET_HEADS):
        g = jnp.where(hh == h, math.exp(LOG_G[h]), g)
    inner = jnp.sum(q * k, axis=2, keepdims=True) * v
    cross = jnp.sum(q * state, axis=2, keepdims=True) * g
    o_ref[...] = inner + cross
    ns_ref[...] = g * state + k * v


def _ret_sample(state, rq, rk, rv):
    n = state.shape[0]
    grp = SAMPLE_GROUP if n % SAMPLE_GROUP == 0 else 1
    col = pl.BlockSpec((grp, N_RET_HEADS, HEAD_DIM, 1), lambda b: (b, 0, 0, 0))
    rowv = pl.BlockSpec((grp, N_RET_HEADS, 1, HEAD_DIM), lambda b: (b, 0, 0, 0))
    st = pl.BlockSpec((grp, N_RET_HEADS, HEAD_DIM, HEAD_DIM), lambda b: (b, 0, 0, 0))
    out, new_state = pl.pallas_call(
        _ret_sample_body,
        grid=(n // grp,),
        in_specs=[st, col, col, rowv],
        out_specs=[rowv, st],
        out_shape=[jax.ShapeDtypeStruct((n, N_RET_HEADS, 1, HEAD_DIM), F32),
                   jax.ShapeDtypeStruct(state.shape, F32)],
        compiler_params=_cparams(("parallel",)),
        name="ret_sample",
    )(state, rq.reshape(n, N_RET_HEADS, HEAD_DIM, 1), rk.reshape(n, N_RET_HEADS, HEAD_DIM, 1),
      rv.reshape(n, N_RET_HEADS, 1, HEAD_DIM))
    return out.reshape(n, RET_W), new_state


def _rope_tables(pos):
    half = HEAD_DIM // 2
    inv = ROPE_BASE ** (-jnp.arange(half, dtype=F32) / half)
    ang = pos.astype(F32)[:, None] * inv[None, :]
    cos, sin = lax.optimization_barrier((jnp.cos(ang), jnp.sin(ang)))
    return jnp.concatenate([cos, cos, cos, cos], axis=1), jnp.concatenate([-sin, sin, -sin, sin], axis=1)


def _block_diag_ones(width):
    r = np.arange(width)[:, None] // HEAD_DIM
    c = np.arange(width)[None, :] // HEAD_DIM
    return jnp.asarray((r == c).astype(np.float32), dtype=BF16)


def _repack_w_in(w):
    o = np.cumsum([0, ATT_W, ATT_W, ATT_W, IDXQ_W, IDX_DIM, N_IDX_HEADS, RET_W, RET_W, RET_W, RET_W, MEM_W])
    pad = jnp.zeros((w.shape[0], KW_W - IDX_DIM - N_IDX_HEADS), w.dtype)
    return jnp.concatenate([w[:, o[0]:o[4]], w[:, o[4]:o[6]], pad, w[:, o[6]:o[11]]], axis=1).astype(BF16)


def kernel(x_prompt, x_sample, mem_prompt, cache_k, cache_v, cache_kidx, state_ret, cache_mem_k, cache_mem_v,
           page_table, ffn1_norm_g, ffn1_w_gu, ffn1_w_down, mix_norm_g, w_in, att_q_norm_g, att_k_norm_g,
           ret_gn_g, mem_norm_g, w_mem_kv, mem_q_norm_g, mem_k_norm_g, w_out, ffn2_norm_g, ffn2_w_gu, ffn2_w_down):
    n_b, s_len, d = x_prompt.shape
    n_s, t_s, _ = x_sample.shape
    assert t_s == 1, "the sample group decodes one token per sequence"
    depth = w_in.shape[0]
    n_mem = mem_prompt.shape[1]
    n_pool = cache_k.shape[1]
    past_len = page_table.shape[1] * PAGE_SIZE
    page_table = page_table.astype(I32)

    cos_p, sin_p = _rope_tables(jnp.arange(s_len, dtype=I32))
    cos_s, sin_s = _rope_tables(jnp.full((n_s,), past_len, I32))
    bd = _block_diag_ones(ATT_W)
    bd_ret = _block_diag_ones(RET_W)
    tile8 = lambda g_: jnp.tile(g_, N_ATT_HEADS).reshape(1, ATT_W)
    tile4 = lambda g_: jnp.tile(g_, N_MEM_HEADS).reshape(1, MEM_W)

    yp = x_prompt.reshape(n_b * s_len, d)
    ys = x_sample.reshape(n_s, d)
    mem = mem_prompt.reshape(n_b * n_mem, d)
    outs = [[] for _ in range(10)]
    for l in range(depth):
        g1 = ffn1_norm_g[l].reshape(1, d)
        g2 = ffn2_norm_g[l].reshape(1, d)
        gmix = mix_norm_g[l].reshape(1, d)
        w1gu, w1d = ffn1_w_gu[l].astype(BF16), ffn1_w_down[l].astype(BF16)
        w2gu, w2d = ffn2_w_gu[l].astype(BF16), ffn2_w_down[l].astype(BF16)
        w_in_l = _repack_w_in(w_in[l])
        w_out_l = w_out[l].astype(BF16)
        gq, gk, gm = tile8(att_q_norm_g[l]), tile8(att_k_norm_g[l]), tile4(mem_q_norm_g[l])
        gn = ret_gn_g[l].reshape(1, RET_W)

        yp = _ffn(yp, g1, w1gu, w1d)
        (q, k, kb, v, vt, qi, kw, ki, ki2, rq, rk, rv, rg, mq) = _proj(
            yp, gmix, w_in_l, gq, gk, gm, cos_p, sin_p, bd, n_b)
        mk, mv = _memkv(mem, mem_norm_g[l].reshape(1, d), w_mem_kv[l].astype(BF16), tile4(mem_k_norm_g[l]), bd_ret)
        att = _dsa_prompt_pairs(q, qi, kw, kb, vt, ki2, n_b, s_len)
        ret, s_fin = _ret_prompt(rq, rk, rv, n_b, s_len)
        mo = _mematt_prompt(mq, mk, mv, n_b)
        yp = _mixout_ffn(yp, att, ret, rg, mo, gn, bd_ret, w_out_l, g2, w2gu, w2d)
        outs[0].append(k.reshape(n_b, s_len, N_ATT_HEADS, HEAD_DIM))
        outs[1].append(v.reshape(n_b, s_len, N_ATT_HEADS, HEAD_DIM))
        outs[2].append(ki.reshape(n_b, s_len, IDX_DIM))
        outs[3].append(s_fin)
        outs[4].append(mk.reshape(n_b, n_mem, N_MEM_HEADS, HEAD_DIM))
        outs[5].append(mv.reshape(n_b, n_mem, N_MEM_HEADS, HEAD_DIM))

        ys = _ffn(ys, g1, w1gu, w1d)
        (q, k, kb, v, vt, qi, kw, ki, ki2, rq, rk, rv, rg, mq) = _proj(
            ys, gmix, w_in_l, gq, gk, gm, cos_s, sin_s, bd, 1)
        wi = kw[:, IDX_DIM:IDX_DIM + N_IDX_HEADS]
        sc_past, sc_self = _sample_scores(page_table, qi, wi, ki, jnp.transpose(cache_kidx[l], (0, 2, 1)))
        topk = min(TOPK_MAX, (past_len + t_s) // 4)
        thr, cut = _sample_select(sc_past.reshape(n_s, past_len), sc_self.reshape(n_s, LANES), topk)
        k_t = jnp.transpose(cache_k[l], (0, 2, 3, 1)).reshape(n_pool, ATT_W, PAGE_SIZE)
        v_t = jnp.transpose(cache_v[l], (0, 2, 3, 1)).reshape(n_pool, ATT_W, PAGE_SIZE)
        att = _sample_attend(page_table, thr[:, 0], cut[:, 0], q, k, v, sc_past, sc_self, k_t, v_t)
        ret, s_new = _ret_sample(state_ret[l].astype(F32), rq, rk, rv)
        slot_minor = lambda a: jnp.transpose(a, (0, 2, 3, 1)).reshape(n_s, MEM_W, n_mem)
        mo = _mematt_sample(mq, slot_minor(cache_mem_k[l]), slot_minor(cache_mem_v[l]))
        ys = _mixout_ffn(ys, att, ret, rg, mo, gn, bd_ret, w_out_l, g2, w2gu, w2d)
        outs[6].append(k.reshape(n_s, t_s, N_ATT_HEADS, HEAD_DIM))
        outs[7].append(v.reshape(n_s, t_s, N_ATT_HEADS, HEAD_DIM))
        outs[8].append(ki.reshape(n_s, t_s, IDX_DIM))
        outs[9].append(s_new)

    return (yp.reshape(n_b, s_len, d), ys.reshape(n_s, t_s, d)) + tuple(jnp.stack(o) for o in outs)
```

```python
import functools
import math

import numpy as np
import jax
import jax.numpy as jnp
from jax import lax
from jax.experimental import pallas as pl
from jax.experimental.pallas import tpu as pltpu

F32 = jnp.float32
BF16 = jnp.bfloat16
I32 = jnp.int32

HEAD_DIM = 64
N_ATT_HEADS = 8
N_IDX_HEADS = 8
IDX_DIM = 64
TOPK_MAX = 256
N_RET_HEADS = 4
N_MEM_HEADS = 4
PAGE_SIZE = 128
EPS = 1e-6
ROPE_BASE = 10000.0

ATT_W = N_ATT_HEADS * HEAD_DIM
RET_W = N_RET_HEADS * HEAD_DIM
MEM_W = N_MEM_HEADS * HEAD_DIM
IDXQ_W = N_IDX_HEADS * IDX_DIM
LANES = 128
KW_W = LANES
C_Q = 0
C_K = C_Q + ATT_W
C_V = C_K + ATT_W
C_QI = C_V + ATT_W
C_KW = C_QI + IDXQ_W
C_RQ = C_KW + KW_W
C_RK = C_RQ + RET_W
C_RV = C_RK + RET_W
C_RG = C_RV + RET_W
C_MQ = C_RG + RET_W
PROJ_W = C_MQ + MEM_W

QK_SCALE = HEAD_DIM ** -0.5
IDX_SCALE = IDX_DIM ** -0.5
IDX_HEAD_SCALE = N_IDX_HEADS ** -0.5
NEG_BIG = -1e30
M_INIT = -(2.0 ** 100)
INT_MIN = -(2 ** 31)
F32_LOWEST = float(np.finfo(np.float32).min)
LOG_G = [float(np.log1p(np.float32(-(2.0 ** (-5.0 - h))))) for h in range(N_RET_HEADS)]

VMEM_LIMIT = 56 * 1024 * 1024
ROW_TILE = 512
KEY_CHUNK = 512
Q_TILE = LANES
VT_ROWS = LANES + 16
BISECT_GROUP = 4
RET_CHUNK = 128
PAGES_PER_STEP = 16
SAMPLE_GROUP = 8


def _cparams(sem):
    return pltpu.CompilerParams(dimension_semantics=sem, vmem_limit_bytes=VMEM_LIMIT)


def _resident(shape):
    nd = len(shape)
    return pl.BlockSpec(shape, lambda *_: (0,) * nd, pipeline_mode=pl.Buffered(1))


def _rms(x, g):
    return x * lax.rsqrt(jnp.mean(x * x, axis=-1, keepdims=True) + EPS) * g


def _head_rms(z, g, bd):
    z2 = z * z
    hi = z2.astype(BF16)
    lo = (z2 - hi.astype(F32)).astype(BF16)
    ss = jnp.dot(hi, bd, preferred_element_type=F32) + jnp.dot(lo, bd, preferred_element_type=F32)
    return z * lax.rsqrt(ss * (1.0 / HEAD_DIM) + EPS) * g


def _nt(a, b):
    return lax.dot_general(a, b, (((1,), (1,)), ((), ())), preferred_element_type=F32)


def _from_key(k):
    return lax.bitcast_convert_type(jnp.where(k < 0, k ^ jnp.int32(0x7FFFFFFF), k), F32)


def _finite_threshold(thr_key):
    return jnp.where(thr_key == INT_MIN, F32_LOWEST, jnp.maximum(_from_key(thr_key), F32_LOWEST))


def _ffn_half_step(x, g, wgu_ref, wd_ref, d_ff, fc):
    h = _rms(x, g).astype(BF16)
    acc = jnp.zeros(x.shape, F32)
    for c in range(d_ff // fc):
        gate = jnp.dot(h, wgu_ref[:, c * fc:(c + 1) * fc], preferred_element_type=F32)
        up = jnp.dot(h, wgu_ref[:, d_ff + c * fc:d_ff + (c + 1) * fc], preferred_element_type=F32)
        act = (gate * jax.nn.sigmoid(gate) * up).astype(BF16)
        acc = acc + jnp.dot(act, wd_ref[c * fc:(c + 1) * fc, :], preferred_element_type=F32)
    return x + 0.5 * acc


def _ffn_chunk(d_ff):
    return d_ff // 2 if (d_ff // 2) % LANES == 0 else d_ff


def _ffn_body(x_ref, g_ref, wgu_ref, wd_ref, o_ref, *, d_ff, fc):
    o_ref[...] = _ffn_half_step(x_ref[...], g_ref[...], wgu_ref, wd_ref, d_ff, fc)


def _ffn(x, g, wgu, wd):
    t, d = x.shape
    d_ff = wd.shape[0]
    tm = min(ROW_TILE, t)
    fc = _ffn_chunk(d_ff)
    return pl.pallas_call(
        functools.partial(_ffn_body, d_ff=d_ff, fc=fc),
        grid=(t // tm,),
        in_specs=[pl.BlockSpec((tm, d), lambda i: (i, 0)), _resident((1, d)),
                  _resident(wgu.shape), _resident(wd.shape)],
        out_specs=pl.BlockSpec((tm, d), lambda i: (i, 0)),
        out_shape=jax.ShapeDtypeStruct((t, d), F32),
        compiler_params=_cparams(("parallel",)),
        name="ffn_half",
    )(x, g, wgu, wd)


def _proj_body(x_ref, g_ref, w_ref, gq_ref, gk_ref, gm_ref, cos_ref, sin_ref, bd_ref,
               q_ref, k_ref, kb_ref, v_ref, vt_ref, qi_ref, kw_ref, ki_ref, ki2_ref,
               rq_ref, rk_ref, rv_ref, rg_ref, mq_ref):
    h = _rms(x_ref[...], g_ref[...]).astype(BF16)

    def proj(a, width):
        return jnp.dot(h, w_ref[:, a:a + width], preferred_element_type=F32)

    bd = bd_ref[...]
    q = _head_rms(proj(C_Q, ATT_W), gq_ref[...], bd)
    q_ref[...] = (q * QK_SCALE).astype(BF16)
    k = _head_rms(proj(C_K, ATT_W), gk_ref[...], bd)
    k_ref[...] = k
    kb_ref[...] = k.astype(BF16)
    v = proj(C_V, ATT_W)
    v_ref[...] = v
    vt = jnp.transpose(v)
    tail_row = lax.broadcasted_iota(I32, (VT_ROWS - LANES, vt.shape[1]), 0)
    tail = jnp.where(tail_row == 0, 1.0, 0.0)
    for p in range(N_ATT_HEADS // 2):
        vt_ref[p] = jnp.concatenate([vt[p * LANES:(p + 1) * LANES, :], tail], axis=0).astype(BF16)
    qi_ref[...] = proj(C_QI, IDXQ_W).astype(BF16)
    kw = proj(C_KW, KW_W)
    kw_ref[...] = kw
    ki = kw[:, :IDX_DIM]
    ki_ref[...] = ki
    kib = ki.astype(BF16)
    ki2_ref[...] = jnp.concatenate([kib, kib], axis=1)

    lane = lax.broadcasted_iota(I32, (1, RET_W), 1)
    first_half = (lane % HEAD_DIM) < (HEAD_DIM // 2)
    cos = jnp.concatenate([cos_ref[...]] * (RET_W // LANES), axis=1)
    sin = jnp.concatenate([sin_ref[...]] * (RET_W // LANES), axis=1)

    def rot(x):
        swapped = jnp.where(first_half, pltpu.roll(x, RET_W - HEAD_DIM // 2, axis=1),
                            pltpu.roll(x, HEAD_DIM // 2, axis=1))
        return x * cos + swapped * sin

    rq_ref[...] = rot(proj(C_RQ, RET_W))
    rk_ref[...] = rot(proj(C_RK, RET_W)) * QK_SCALE
    rv_ref[...] = proj(C_RV, RET_W)
    rg_ref[...] = proj(C_RG, RET_W)
    mq = _head_rms(proj(C_MQ, MEM_W), gm_ref[...], bd[:MEM_W, :MEM_W])
    mq_ref[...] = (mq * QK_SCALE).astype(BF16)


def _proj(y, g, w, gq, gk, gm, cos, sin, bd, n_batch):
    t, d = y.shape
    s_len = t // n_batch
    tm = min(ROW_TILE, s_len)
    nsb = s_len // tm
    row = lambda w_: pl.BlockSpec((tm, w_), lambda i: (i, 0))
    tab = pl.BlockSpec((tm, LANES), lambda i: (i % nsb, 0))
    out_shapes = [
        (jax.ShapeDtypeStruct((t, ATT_W), BF16), row(ATT_W)),
        (jax.ShapeDtypeStruct((t, ATT_W), F32), row(ATT_W)),
        (jax.ShapeDtypeStruct((t, ATT_W), BF16), row(ATT_W)),
        (jax.ShapeDtypeStruct((t, ATT_W), F32), row(ATT_W)),
        (jax.ShapeDtypeStruct((n_batch, nsb, N_ATT_HEADS // 2, VT_ROWS, tm), BF16),
         pl.BlockSpec((None, None, N_ATT_HEADS // 2, VT_ROWS, tm),
                      lambda i: (i // nsb, i % nsb, 0, 0, 0))),
        (jax.ShapeDtypeStruct((t, IDXQ_W), BF16), row(IDXQ_W)),
        (jax.ShapeDtypeStruct((t, KW_W), F32), row(KW_W)),
        (jax.ShapeDtypeStruct((t, IDX_DIM), F32), row(IDX_DIM)),
        (jax.ShapeDtypeStruct((t, 2 * IDX_DIM), BF16), row(2 * IDX_DIM)),
        (jax.ShapeDtypeStruct((t, RET_W), F32), row(RET_W)),
        (jax.ShapeDtypeStruct((t, RET_W), F32), row(RET_W)),
        (jax.ShapeDtypeStruct((t, RET_W), F32), row(RET_W)),
        (jax.ShapeDtypeStruct((t, RET_W), F32), row(RET_W)),
        (jax.ShapeDtypeStruct((t, MEM_W), BF16), row(MEM_W)),
    ]
    return pl.pallas_call(
        _proj_body,
        grid=(t // tm,),
        in_specs=[row(d), _resident((1, d)), _resident(w.shape), _resident((1, ATT_W)),
                  _resident((1, ATT_W)), _resident((1, MEM_W)), tab, tab, _resident(bd.shape)],
        out_specs=[s for _, s in out_shapes],
        out_shape=[s for s, _ in out_shapes],
        compiler_params=_cparams(("parallel",)),
        name="mix_proj",
    )(y, g, w, gq, gk, gm, cos, sin, bd)


def _memkv_body(x_ref, g_ref, w_ref, gk_ref, bd_ref, mk_ref, mv_ref):
    h = _rms(x_ref[...], g_ref[...]).astype(BF16)
    z = jnp.dot(h, w_ref[...], preferred_element_type=F32)
    mk_ref[...] = _head_rms(z[:, :MEM_W], gk_ref[...], bd_ref[...])
    mv_ref[...] = z[:, MEM_W:]


def _memkv(mem, g, w, gk, bd):
    t, d = mem.shape
    tm = min(ROW_TILE, t)
    row = lambda w_: pl.BlockSpec((tm, w_), lambda i: (i, 0))
    return pl.pallas_call(
        _memkv_body,
        grid=(t // tm,),
        in_specs=[row(d), _resident((1, d)), _resident(w.shape), _resident((1, MEM_W)), _resident(bd.shape)],
        out_specs=[row(MEM_W), row(MEM_W)],
        out_shape=[jax.ShapeDtypeStruct((t, MEM_W), F32)] * 2,
        compiler_params=_cparams(("parallel",)),
        name="mem_kv",
    )(mem, g, w, gk, bd)


def _dsa_pair_body(q_ref, qi_ref, kw_ref, k_ref, vt_ref, ki2_ref, o_ref,
                   key_scr, qi2_scr, q2_scr, cut_scr, m_scr, acc_scr, sa_scr, sb_scr, *, topk, sc, s_len):
    n_st = 2
    jj = pl.program_id(1)
    n_ch = ((n_st * jj + 1) * Q_TILE + Q_TILE + sc - 1) // sc
    n_pair = N_ATT_HEADS // 2
    lane = lax.broadcasted_iota(I32, (1, LANES), 1)
    qpos = [(n_st * jj + s) * Q_TILE + lane for s in range(n_st)]
    kiota = lax.broadcasted_iota(I32, (sc, LANES), 0)
    streams = range(n_st)

    r2 = lax.broadcasted_iota(I32, (2 * Q_TILE, LANES), 0)
    c2 = lax.broadcasted_iota(I32, (2 * Q_TILE, LANES), 1)
    keep = (r2 < Q_TILE) == (c2 < HEAD_DIM)
    eye2 = jnp.where(r2 % Q_TILE == c2, 1.0, 0.0).astype(BF16)
    w_s = []
    for s in streams:
        rows = slice(s * Q_TILE, (s + 1) * Q_TILE)
        for p in range(n_pair):
            a = qi_ref[rows, p * LANES:(p + 1) * LANES].astype(F32)
            qi2_scr[s, p] = jnp.where(keep, jnp.concatenate([a, a], axis=0), 0.0).astype(BF16)
            a = q_ref[rows, p * LANES:(p + 1) * LANES].astype(F32)
            q2 = jnp.where(keep, jnp.concatenate([a, a], axis=0), 0.0).astype(BF16)
            q2_scr[s, p] = jnp.concatenate([q2, eye2], axis=1)
        w_t = jnp.transpose(kw_ref[rows, :])[IDX_DIM:IDX_DIM + N_IDX_HEADS, :]
        w_s.append((w_t * IDX_HEAD_SCALE) * IDX_SCALE)

    def score_chunk(c, carry):
        off = pl.multiple_of(c * sc, sc)
        kc = ki2_ref[pl.ds(off, sc), :]
        acc = [jnp.zeros((sc, LANES), F32) for _ in streams]
        for p in range(n_pair):
            for s in streams:
                lg = _nt(kc, qi2_scr[s, p])
                acc[s] = acc[s] + jnp.maximum(lg[:, :LANES], 0.0) * w_s[s][2 * p:2 * p + 1, :]
                acc[s] = acc[s] + jnp.maximum(lg[:, LANES:], 0.0) * w_s[s][2 * p + 1:2 * p + 2, :]
        for s in streams:
            key_scr[s, pl.ds(off, sc), :] = jnp.where(off + kiota <= qpos[s], acc[s], -jnp.inf)
        return carry

    lax.fori_loop(0, n_ch, score_chunk, 0)

    def count(preds):
        def body(c, accs):
            off = pl.multiple_of(c * sc, sc)
            accs = list(accs)
            for g in range(sc // 64):
                rows = pl.ds(off + g * 64, 64)
                for s in streams:
                    hit = preds[s](key_scr[s, rows, :], off + kiota[g * 64:(g + 1) * 64])
                    accs[s] = jnp.where(hit, accs[s] + 1.0, accs[s])
            return tuple(accs)
        accs = lax.fori_loop(0, n_ch, body, tuple(jnp.zeros((64, LANES), F32) for _ in streams))
        return [a.reshape(8, 8, LANES).sum(axis=0).sum(axis=0, keepdims=True) for a in accs]

    def bisect_group(state):
        i, thrs, actives, _ = state
        thrs, actives = list(thrs), list(actives)
        for _ in range(BISECT_GROUP):
            cands = [t + lax.shift_left(jnp.int32(1), 31 - i) for t in thrs]
            cnts = count([functools.partial(lambda x, _, cand: x >= cand, cand=_from_key(cand)) for cand in cands])
            for s in streams:
                thrs[s] = jnp.where((cnts[s] >= topk) & (actives[s] > 0.0), cands[s], thrs[s])
                actives[s] = jnp.where(cnts[s] == topk, 0.0, actives[s])
            i = i + 1
        return i, tuple(thrs), tuple(actives), jnp.max(jnp.maximum(actives[0], actives[1]))

    actives0 = tuple(jnp.where(qpos[s] + 1 <= topk, 0.0, 1.0) for s in streams)
    _, thrs, actives, n_active = lax.while_loop(
        lambda st: (st[0] < 32) & (st[3] > 0.0), bisect_group,
        (jnp.int32(0), tuple(jnp.full((1, LANES), INT_MIN, I32) for _ in streams), actives0,
         jnp.max(jnp.maximum(actives0[0], actives0[1]))))
    for s in streams:
        cut_scr[s] = jnp.full((1, LANES), s_len, I32)

    thr_f = [_finite_threshold(t) for t in thrs]

    @pl.when(n_active > 0.0)
    def _():
        c_gt = count([functools.partial(lambda x, _, t: x > t, t=t) for t in thr_f])
        c_ge = count([functools.partial(lambda x, _, t: x >= t, t=t) for t in thr_f])
        need = [topk - c for c in c_gt]
        tie = [((c_ge[s] - c_gt[s]) > need[s]) & (actives[s] > 0.0) for s in streams]
        nbits = max(1, (s_len - 1).bit_length())

        def bisect_pos(i, los):
            cands = [lo + lax.shift_left(jnp.int32(1), nbits - 1 - i) for lo in los]
            cnts = count([functools.partial(lambda x, kpos, t, cand: jnp.where(x == t, kpos, s_len) < cand,
                                            t=thr_f[s], cand=cands[s]) for s in streams])
            return tuple(jnp.where(cnts[s] < need[s], cands[s], los[s]) for s in streams)

        los = lax.fori_loop(0, nbits, bisect_pos, tuple(jnp.zeros((1, LANES), I32) for _ in streams))
        for s in streams:
            cut_scr[s] = jnp.where(tie[s], los[s], s_len)

    cuts = [cut_scr[s] for s in streams]

    m_scr[...] = jnp.full(m_scr.shape, M_INIT, F32)
    acc_scr[...] = jnp.zeros(acc_scr.shape, F32)

    def qk_chunk(c, s_scr):
        off = pl.multiple_of(c * sc, sc)
        biases = []
        for s in streams:
            bias = _select_bias(key_scr[s, pl.ds(off, sc), :], off + kiota, thr_f[s], cuts[s])
            biases.append(bias.astype(BF16))
        for p in range(n_pair):
            kp = k_ref[pl.ds(off, sc), p * LANES:(p + 1) * LANES]
            for s in streams:
                s_scr[s, p] = _nt(jnp.concatenate([kp, biases[s]], axis=1), q2_scr[s, p])

    def softmax_pv_chunk(c, s_scr):
        es, alphas = {}, {}
        for p in range(n_pair):
            for s in streams:
                x = s_scr[s, p].astype(BF16)
                m_c = x.reshape(sc // 64, 64, 2 * Q_TILE).max(axis=0)
                m_c = m_c.reshape(4, 16, 2 * Q_TILE).max(axis=0).astype(F32).max(axis=0, keepdims=True)
                m_old = m_scr[s, p]
                m_new = jnp.maximum(m_old, m_c)
                alphas[s, p] = jnp.exp(m_old - m_new)
                es[s, p] = jnp.exp(x - m_new.astype(BF16))
                m_scr[s, p] = m_new
            for s in streams:
                pv = jnp.dot(vt_ref[c, p], es[s, p], preferred_element_type=F32)
                acc_scr[s, p] = acc_scr[s, p] * alphas[s, p] + pv

    def att_two_chunks(i, carry):
        c = 2 * i
        qk_chunk(c + 1, sb_scr)
        softmax_pv_chunk(c, sa_scr)
        qk_chunk(c + 2, sa_scr)
        softmax_pv_chunk(c + 1, sb_scr)
        return carry

    qk_chunk(0, sa_scr)
    n_two = (n_ch - 1) // 2
    lax.fori_loop(0, n_two, att_two_chunks, 0)

    @pl.when(n_ch % 2 == 0)
    def _():
        qk_chunk(n_ch - 1, sb_scr)
        softmax_pv_chunk(n_ch - 2, sa_scr)
        softmax_pv_chunk(n_ch - 1, sb_scr)

    @pl.when(n_ch % 2 == 1)
    def _():
        softmax_pv_chunk(n_ch - 1, sa_scr)

    for s in streams:
        outs = []
        for p in range(n_pair):
            a = acc_scr[s, p]
            l = a[LANES:LANES + 1, :]
            outs.append(a[:HEAD_DIM, :LANES] / l[:, :LANES])
            outs.append(a[HEAD_DIM:LANES, LANES:] / l[:, LANES:])
        o_ref[s * Q_TILE:(s + 1) * Q_TILE, :] = jnp.transpose(jnp.concatenate(outs, axis=0))


def _dsa_prompt_pairs(q, qi, kw, kb, vt, ki2, n_batch, s_len):
    t = q.shape[0]
    sc = min(KEY_CHUNK, s_len)
    n_st = 2
    nq = s_len // (n_st * Q_TILE)
    n_pair = N_ATT_HEADS // 2
    topk = min(TOPK_MAX, s_len // 4)
    qrow = lambda w_: pl.BlockSpec((n_st * Q_TILE, w_), lambda b, j: (b * nq + j, 0))
    per_batch = lambda w_: pl.BlockSpec((s_len, w_), lambda b, j: (b, 0), pipeline_mode=pl.Buffered(1))
    return pl.pallas_call(
        functools.partial(_dsa_pair_body, topk=topk, sc=sc, s_len=s_len),
        grid=(n_batch, nq),
        in_specs=[qrow(ATT_W), qrow(IDXQ_W), qrow(KW_W), per_batch(ATT_W),
                  pl.BlockSpec((None, s_len // sc, n_pair, VT_ROWS, sc), lambda b, j: (b, 0, 0, 0, 0),
                               pipeline_mode=pl.Buffered(1)),
                  per_batch(2 * IDX_DIM)],
        out_specs=qrow(ATT_W),
        out_shape=jax.ShapeDtypeStruct((t, ATT_W), F32),
        scratch_shapes=[pltpu.VMEM((n_st, s_len, LANES), F32),
                        pltpu.VMEM((n_st, n_pair, 2 * Q_TILE, LANES), BF16),
                        pltpu.VMEM((n_st, n_pair, 2 * Q_TILE, 2 * LANES), BF16),
                        pltpu.VMEM((n_st, 1, LANES), I32),
                        pltpu.VMEM((n_st, n_pair, 1, 2 * Q_TILE), F32),
                        pltpu.VMEM((n_st, n_pair, VT_ROWS, 2 * Q_TILE), F32),
                        pltpu.VMEM((n_st, n_pair, sc, 2 * Q_TILE), F32),
                        pltpu.VMEM((n_st, n_pair, sc, 2 * Q_TILE), F32)],
        compiler_params=_cparams(("parallel", "arbitrary")),
        name="dsa_prompt",
    )(q, qi, kw, kb, vt, ki2)


def _ret_lane_const(vals):
    lane = lax.broadcasted_iota(I32, (1, RET_W), 1)
    out = jnp.zeros((1, RET_W), F32)
    for h, v in enumerate(vals):
        out = jnp.where(lane // HEAD_DIM == h, v, out)
    return out


def _ret_body(rq_ref, rk_ref, rv_ref, o_ref, st_ref, sbd_scr, *, ch):
    c = pl.program_id(0)
    n_seq = sbd_scr.shape[0]

    @pl.when(c == 0)
    def _():
        sbd_scr[...] = jnp.zeros(sbd_scr.shape, F32)

    head = lax.broadcasted_iota(I32, (1, RET_W), 1) // HEAD_DIM
    log_g = _ret_lane_const(LOG_G)
    i_col = lax.broadcasted_iota(I32, (ch, 1), 0).astype(F32)
    q_dec = jnp.exp(log_g * (i_col + 1.0))
    k_dec = jnp.exp(log_g * (ch - 1.0 - i_col))
    ii = lax.broadcasted_iota(I32, (ch, ch), 0)
    jj = lax.broadcasted_iota(I32, (ch, ch), 1)
    causal = ii >= jj
    diff = jnp.where(causal, ii - jj, 0).astype(F32)
    decays = [jnp.where(causal, jnp.exp(LOG_G[h] * diff), 0.0) for h in range(N_RET_HEADS)]
    rh = lax.broadcasted_iota(I32, (RET_W, RET_W), 0) // HEAD_DIM
    ch_ = lax.broadcasted_iota(I32, (RET_W, RET_W), 1) // HEAD_DIM

    for b in range(n_seq):
        q = rq_ref[b]
        k = rk_ref[b]
        vb = rv_ref[b].astype(BF16)
        kb = k.astype(BF16)
        state = sbd_scr[b]
        cross = jnp.dot(q.astype(BF16), state.astype(BF16), preferred_element_type=F32) * q_dec
        inner = jnp.zeros((ch, RET_W), F32)
        for h in range(N_RET_HEADS):
            qm = jnp.where(head == h, q, 0.0).astype(BF16)
            sc = (_nt(qm, kb) * decays[h]).astype(BF16)
            inner = inner + jnp.where(head == h, jnp.dot(sc, vb, preferred_element_type=F32), 0.0)
        o_ref[b] = inner + cross

        kd = (k * k_dec).astype(BF16)
        kv = lax.dot_general(kd, vb, (((0,), (0,)), ((), ())), preferred_element_type=F32)
        new_state = state * jnp.exp(log_g * float(ch)) + jnp.where(rh == ch_, kv, 0.0)
        sbd_scr[b] = new_state

    @pl.when(c == pl.num_programs(0) - 1)
    def _():
        for b in range(n_seq):
            for h in range(N_RET_HEADS):
                st_ref[b, h] = sbd_scr[b, h * HEAD_DIM:(h + 1) * HEAD_DIM, h * HEAD_DIM:(h + 1) * HEAD_DIM]


def _ret_prompt(rq, rk, rv, n_batch, s_len):
    ch = min(RET_CHUNK, s_len)
    blk = pl.BlockSpec((n_batch, ch, RET_W), lambda c: (0, c, 0))
    as3d = lambda a: a.reshape(n_batch, s_len, RET_W)
    out, state = pl.pallas_call(
        functools.partial(_ret_body, ch=ch),
        grid=(s_len // ch,),
        in_specs=[blk, blk, blk],
        out_specs=[blk, pl.BlockSpec((n_batch, N_RET_HEADS, HEAD_DIM, HEAD_DIM), lambda c: (0, 0, 0, 0))],
        out_shape=[jax.ShapeDtypeStruct((n_batch, s_len, RET_W), F32),
                   jax.ShapeDtypeStruct((n_batch, N_RET_HEADS, HEAD_DIM, HEAD_DIM), F32)],
        scratch_shapes=[pltpu.VMEM((n_batch, RET_W, RET_W), F32)],
        compiler_params=_cparams(("arbitrary",)),
        name="ret_prompt",
    )(as3d(rq), as3d(rk), as3d(rv))
    return out.reshape(n_batch * s_len, RET_W), state


def _softmax_lanes(s):
    m = jnp.max(s, axis=-1, keepdims=True)
    e = jnp.exp(s - m)
    return e / jnp.sum(e, axis=-1, keepdims=True)


def _mematt_body(mq_ref, mk_ref, mv_ref, o_ref):
    mq = mq_ref[...].astype(F32)
    mk = mk_ref[...].astype(BF16)
    mv = mv_ref[...].astype(BF16)
    head = lax.broadcasted_iota(I32, (1, MEM_W), 1) // HEAD_DIM
    out = jnp.zeros(mq.shape, F32)
    for h in range(N_MEM_HEADS):
        qm = jnp.where(head == h, mq, 0.0).astype(BF16)
        p = _softmax_lanes(_nt(qm, mk)).astype(BF16)
        out = out + jnp.where(head == h, jnp.dot(p, mv, preferred_element_type=F32), 0.0)
    o_ref[...] = out


def _mematt_prompt(mq, mk, mv, n_batch):
    t = mq.shape[0]
    n_mem = mk.shape[0] // n_batch
    s_len = t // n_batch
    tm = min(ROW_TILE, s_len)
    nsb = s_len // tm
    row = pl.BlockSpec((tm, MEM_W), lambda i: (i, 0))
    mem = pl.BlockSpec((n_mem, MEM_W), lambda i: (i // nsb, 0))
    return pl.pallas_call(
        _mematt_body,
        grid=(t // tm,),
        in_specs=[row, mem, mem],
        out_specs=row,
        out_shape=jax.ShapeDtypeStruct((t, MEM_W), F32),
        compiler_params=_cparams(("parallel",)),
        name="mem_attend",
    )(mq, mk, mv)


def _mematt_sample_body(mq_ref, mk_ref, mv_ref, o_ref):
    rows = 8
    sel = lax.broadcasted_iota(I32, (rows, MEM_W), 1) // HEAD_DIM == lax.broadcasted_iota(I32, (rows, MEM_W), 0)
    for t in range(mq_ref.shape[0]):
        mq = jnp.broadcast_to(mq_ref[t].astype(F32), (rows, MEM_W))
        qbd = jnp.where(sel, mq, 0.0).astype(BF16)
        p = _softmax_lanes(jnp.dot(qbd, mk_ref[t].astype(BF16), preferred_element_type=F32)).astype(BF16)
        o = _nt(p, mv_ref[t].astype(BF16))
        o_ref[t] = jnp.sum(jnp.where(sel, o, 0.0), axis=0, keepdims=True)


def _mematt_sample(mq, mk_t, mv_t):
    n, n_mem = mk_t.shape[0], mk_t.shape[2]
    grp = SAMPLE_GROUP if n % SAMPLE_GROUP == 0 else 1
    one = pl.BlockSpec((grp, 1, MEM_W), lambda b: (b, 0, 0))
    mem = pl.BlockSpec((grp, MEM_W, n_mem), lambda b: (b, 0, 0))
    out = pl.pallas_call(
        _mematt_sample_body,
        grid=(n // grp,),
        in_specs=[one, mem, mem],
        out_specs=one,
        out_shape=jax.ShapeDtypeStruct((n, 1, MEM_W), F32),
        compiler_params=_cparams(("parallel",)),
        name="mem_attend_sample",
    )(mq.reshape(n, 1, MEM_W), mk_t, mv_t)
    return out.reshape(n, MEM_W)


def _mixout_ffn_body(y_ref, att_ref, ret_ref, rg_ref, mo_ref, gn_ref, bd_ref, w_ref, g2_ref, wgu_ref, wd_ref,
                     o_ref, *, d_ff, fc):
    retn = _head_rms(ret_ref[...], gn_ref[...], bd_ref[...])
    rg = rg_ref[...]
    gated = (rg * jax.nn.sigmoid(rg)) * retn
    acc = jnp.dot(att_ref[...].astype(BF16), w_ref[:ATT_W, :], preferred_element_type=F32)
    acc = acc + jnp.dot(gated.astype(BF16), w_ref[ATT_W:ATT_W + RET_W, :], preferred_element_type=F32)
    acc = acc + jnp.dot(mo_ref[...].astype(BF16), w_ref[ATT_W + RET_W:, :], preferred_element_type=F32)
    o_ref[...] = _ffn_half_step(y_ref[...] + acc, g2_ref[...], wgu_ref, wd_ref, d_ff, fc)


def _mixout_ffn(y, att, ret, rg, mo, gn, bd, w, g2, wgu, wd):
    t, d = y.shape
    d_ff = wd.shape[0]
    tm = min(ROW_TILE, t)
    row = lambda w_: pl.BlockSpec((tm, w_), lambda i: (i, 0))
    return pl.pallas_call(
        functools.partial(_mixout_ffn_body, d_ff=d_ff, fc=_ffn_chunk(d_ff)),
        grid=(t // tm,),
        in_specs=[row(d), row(ATT_W), row(RET_W), row(RET_W), row(MEM_W), _resident((1, RET_W)),
                  _resident(bd.shape), _resident(w.shape), _resident((1, d)), _resident(wgu.shape),
                  _resident(wd.shape)],
        out_specs=row(d),
        out_shape=jax.ShapeDtypeStruct((t, d), F32),
        compiler_params=_cparams(("parallel",)),
        name="mix_out_ffn",
    )(y, att, ret, rg, mo, gn, bd, w, g2, wgu, wd)


def _sample_score_body(pt_ref, qi_ref, w_ref, kin_ref, *rest, pps):
    page_refs, (past_ref, self_ref) = rest[:pps], rest[pps:]
    q8 = qi_ref[...]
    w8 = (w_ref[...] * IDX_HEAD_SCALE) * IDX_SCALE

    def head_sum(lg):
        s = jnp.sum(jnp.maximum(lg, 0.0) * w8, axis=0, keepdims=True)
        return jnp.where(s == 0.0, 0.0, s)

    for i in range(pps):
        lg = jnp.dot(q8, page_refs[i][...].astype(BF16), preferred_element_type=F32)
        past_ref[i:i + 1, :] = head_sum(lg)

    @pl.when(pl.program_id(1) == 0)
    def _():
        kn = kin_ref[...].astype(BF16).astype(F32)
        lg = jnp.sum(q8.astype(F32) * kn, axis=1, keepdims=True)
        self_ref[...] = jnp.broadcast_to(head_sum(lg), self_ref.shape)


def _sample_scores(page_table, qi, wi, ki_new, kidx_t):
    n, n_pages = page_table.shape
    pps = min(4 * PAGES_PER_STEP, n_pages)
    page_specs = [pl.BlockSpec((None, IDX_DIM, PAGE_SIZE),
                               functools.partial(lambda b, g, pt, i: (pt[b, g * pps + i], 0, 0), i=i))
                  for i in range(pps)]
    grid_spec = pltpu.PrefetchScalarGridSpec(
        num_scalar_prefetch=1,
        grid=(n, n_pages // pps),
        in_specs=[pl.BlockSpec((None, N_IDX_HEADS, IDX_DIM), lambda b, g, pt: (b, 0, 0)),
                  pl.BlockSpec((None, N_IDX_HEADS, 1), lambda b, g, pt: (b, 0, 0)),
                  pl.BlockSpec((None, 1, IDX_DIM), lambda b, g, pt: (b, 0, 0))] + page_specs,
        out_specs=[pl.BlockSpec((None, pps, PAGE_SIZE), lambda b, g, pt: (b, g, 0)),
                   pl.BlockSpec((None, 1, LANES), lambda b, g, pt: (b, 0, 0))],
    )
    return pl.pallas_call(
        functools.partial(_sample_score_body, pps=pps),
        grid_spec=grid_spec,
        out_shape=[jax.ShapeDtypeStruct((n, n_pages, PAGE_SIZE), F32),
                   jax.ShapeDtypeStruct((n, 1, LANES), F32)],
        compiler_params=_cparams(("parallel", "arbitrary")),
        name="sample_scores",
    )(page_table, qi.reshape(n, N_IDX_HEADS, IDX_DIM), wi.reshape(n, N_IDX_HEADS, 1),
      ki_new.reshape(n, 1, IDX_DIM), *([kidx_t] * pps))


def _sample_select_body(past_ref, self_ref, thr_ref, cut_ref, *, topk, past_len):
    past = past_ref[...]
    own = self_ref[...][:, 0:1]
    n = past.shape[0]
    kpos = lax.broadcasted_iota(I32, past.shape, 1)

    def count(pred):
        c = jnp.sum(jnp.where(pred(past, kpos), 1.0, 0.0), axis=1, keepdims=True)
        return c + jnp.where(pred(own, past_len), 1.0, 0.0)

    def bisect(i, thr_key):
        cand = thr_key + lax.shift_left(jnp.int32(1), 31 - i)
        cand_f = _from_key(cand)
        return jnp.where(count(lambda x, _: x >= cand_f) >= topk, cand, thr_key)

    thr = _finite_threshold(lax.fori_loop(0, 32, bisect, jnp.full((n, 1), INT_MIN, I32)))
    c_gt = count(lambda x, _: x > thr)
    c_ge = count(lambda x, _: x >= thr)
    need = topk - c_gt
    tie = (c_ge - c_gt) > need
    thr_ref[...] = jnp.broadcast_to(thr, thr_ref.shape)
    cut_ref[...] = jnp.full(cut_ref.shape, past_len + 1, I32)

    @pl.when(jnp.max(jnp.where(tie, 1.0, 0.0)) > 0.0)
    def _():
        nbits = max(1, past_len.bit_length())

        def bisect_pos(i, lo):
            cand = lo + lax.shift_left(jnp.int32(1), nbits - 1 - i)
            cnt = count(lambda x, pos: jnp.where(x == thr, pos, past_len + 1) < cand)
            return jnp.where(cnt < need, cand, lo)

        lo = lax.fori_loop(0, nbits, bisect_pos, jnp.zeros((n, 1), I32))
        cut_ref[...] = jnp.broadcast_to(jnp.where(tie, lo, past_len + 1), cut_ref.shape)


def _sample_select(sc_past, sc_self, topk):
    n, past_len = sc_past.shape
    full = lambda shape: pl.BlockSpec(shape, lambda i: (0,) * len(shape))
    return pl.pallas_call(
        functools.partial(_sample_select_body, topk=topk, past_len=past_len),
        grid=(1,),
        in_specs=[full((n, past_len)), full((n, LANES))],
        out_specs=[full((n, LANES)), full((n, LANES))],
        out_shape=[jax.ShapeDtypeStruct((n, LANES), F32), jax.ShapeDtypeStruct((n, LANES), I32)],
        compiler_params=_cparams(("arbitrary",)),
        name="sample_select",
    )(sc_past, sc_self)


def _select_bias(x, kpos, thr, cut):
    bias = jnp.where(x >= thr, 0.0, NEG_BIG)
    bias = jnp.where(x == jnp.inf, NEG_BIG, bias)
    return jnp.where(x == thr, jnp.where(kpos > cut, NEG_BIG, bias), bias)


def _heads_to_rows(x):
    h, w = x.shape
    return jnp.broadcast_to(x[:, None, :], (h, HEAD_DIM, w)).reshape(h * HEAD_DIM, w)


def _sample_att_body(pt_ref, thr_ref, cut_ref, q_ref, kn_ref, vn_ref, past_ref, self_ref, *rest, pps, past_len):
    k_refs, v_refs = rest[:pps], rest[pps:2 * pps]
    o_ref, m_scr, l_scr, acc_scr = rest[2 * pps:]
    b = pl.program_id(0)
    g = pl.program_id(1)
    thr = thr_ref[b]
    cut = cut_ref[b]

    @pl.when(g == 0)
    def _():
        m_scr[...] = jnp.full(m_scr.shape, NEG_BIG, F32)
        l_scr[...] = jnp.zeros(l_scr.shape, F32)
        acc_scr[...] = jnp.zeros(acc_scr.shape, F32)

    q = jnp.broadcast_to(q_ref[...].astype(F32), (ATT_W, PAGE_SIZE))
    lane = lax.broadcasted_iota(I32, (1, PAGE_SIZE), 1)

    def head_sums(x):
        return x.reshape(N_ATT_HEADS, HEAD_DIM, x.shape[1]).sum(axis=1)

    parts = []
    for i in range(pps):
        kpos = (g * pps + i) * PAGE_SIZE + lane
        bias = _select_bias(past_ref[i:i + 1, :], kpos, thr, cut)
        parts.append(head_sums(k_refs[i][...] * q) + bias)
    m_old = m_scr[...]
    m_new = m_old
    for s in parts:
        m_new = jnp.maximum(m_new, jnp.max(s, axis=1, keepdims=True))
    alpha = jnp.exp(m_old - m_new)
    l_new = alpha * l_scr[...]
    acc = acc_scr[...] * _heads_to_rows(alpha)
    for i in range(pps):
        e = jnp.exp(parts[i] - m_new)
        l_new = l_new + jnp.sum(e, axis=1, keepdims=True)
        acc = acc + _heads_to_rows(e) * v_refs[i][...]
    m_scr[...] = m_new
    l_scr[...] = l_new
    acc_scr[...] = acc

    @pl.when(g == pl.num_programs(1) - 1)
    def _():
        s_self = head_sums(kn_ref[...] * q_ref[...].astype(F32))
        s_self = s_self + _select_bias(self_ref[...][:, 0:1], past_len, thr, cut)
        m_fin = jnp.maximum(m_new, s_self)
        a2 = jnp.exp(m_new - m_fin)
        e_self = jnp.exp(s_self - m_fin)
        l_fin = a2 * l_new + e_self
        num = jnp.sum(acc, axis=1, keepdims=True) * _heads_to_rows(a2) + _heads_to_rows(e_self) * vn_ref[...]
        o_ref[...] = num / _heads_to_rows(l_fin)


def _sample_attend(page_table, thr, cut, q, k_new, v_new, sc_past, sc_self, k_t, v_t):
    n, n_pages = page_table.shape
    pps = min(PAGES_PER_STEP, n_pages)
    past_len = n_pages * PAGE_SIZE
    page_specs = [pl.BlockSpec((None, ATT_W, PAGE_SIZE),
                               functools.partial(lambda b, g, pt, th, cu, i: (pt[b, g * pps + i], 0, 0), i=i))
                  for i in range(pps)]
    col = pl.BlockSpec((None, ATT_W, 1), lambda b, g, pt, th, cu: (b, 0, 0))
    grid_spec = pltpu.PrefetchScalarGridSpec(
        num_scalar_prefetch=3,
        grid=(n, n_pages // pps),
        in_specs=[col, col, col,
                  pl.BlockSpec((None, pps, PAGE_SIZE), lambda b, g, pt, th, cu: (b, g, 0)),
                  pl.BlockSpec((None, 1, LANES), lambda b, g, pt, th, cu: (b, 0, 0))] + page_specs + page_specs,
        out_specs=col,
        scratch_shapes=[pltpu.VMEM((N_ATT_HEADS, 1), F32), pltpu.VMEM((N_ATT_HEADS, 1), F32),
                        pltpu.VMEM((ATT_W, PAGE_SIZE), F32)],
    )
    out = pl.pallas_call(
        functools.partial(_sample_att_body, pps=pps, past_len=past_len),
        grid_spec=grid_spec,
        out_shape=jax.ShapeDtypeStruct((n, ATT_W, 1), F32),
        compiler_params=_cparams(("parallel", "arbitrary")),
        name="sample_attend",
    )(page_table, thr, cut, q.reshape(n, ATT_W, 1), k_new.reshape(n, ATT_W, 1), v_new.reshape(n, ATT_W, 1),
      sc_past, sc_self, *([k_t] * pps), *([v_t] * pps))
    return out.reshape(n, ATT_W)


def _ret_sample_body(st_ref, q_ref, k_ref, v_ref, o_ref, ns_ref):
    state = st_ref[...]
    q = q_ref[...]
    k = k_ref[...]
    v = v_ref[...]
    hh = lax.broadcasted_iota(I32, (1, N_RET_HEADS, 1, 1), 1)
    g = jnp.zeros((1, N_RET_HEADS, 1, 1), F32)
    for h in range(N_RET_HEADS):
        g = jnp.where(hh == h, math.exp(LOG_G[h]), g)
    inner = jnp.sum(q * k, axis=2, keepdims=True) * v
    cross = jnp.sum(q * state, axis=2, keepdims=True) * g
    o_ref[...] = inner + cross
    ns_ref[...] = g * state + k * v


def _ret_sample(state, rq, rk, rv):
    n = state.shape[0]
    grp = SAMPLE_GROUP if n % SAMPLE_GROUP == 0 else 1
    col = pl.BlockSpec((grp, N_RET_HEADS, HEAD_DIM, 1), lambda b: (b, 0, 0, 0))
    rowv = pl.BlockSpec((grp, N_RET_HEADS, 1, HEAD_DIM), lambda b: (b, 0, 0, 0))
    st = pl.BlockSpec((grp, N_RET_HEADS, HEAD_DIM, HEAD_DIM), lambda b: (b, 0, 0, 0))
    out, new_state = pl.pallas_call(
        _ret_sample_body,
        grid=(n // grp,),
        in_specs=[st, col, col, rowv],
        out_specs=[rowv, st],
        out_shape=[jax.ShapeDtypeStruct((n, N_RET_HEADS, 1, HEAD_DIM), F32),
                   jax.ShapeDtypeStruct(state.shape, F32)],
        compiler_params=_cparams(("parallel",)),
        name="ret_sample",
    )(state, rq.reshape(n, N_RET_HEADS, HEAD_DIM, 1), rk.reshape(n, N_RET_HEADS, HEAD_DIM, 1),
      rv.reshape(n, N_RET_HEADS, 1, HEAD_DIM))
    return out.reshape(n, RET_W), new_state


def _rope_tables(pos):
    half = HEAD_DIM // 2
    inv = ROPE_BASE ** (-jnp.arange(half, dtype=F32) / half)
    ang = pos.astype(F32)[:, None] * inv[None, :]
    cos, sin = lax.optimization_barrier((jnp.cos(ang), jnp.sin(ang)))
    return jnp.concatenate([cos, cos, cos, cos], axis=1), jnp.concatenate([-sin, sin, -sin, sin], axis=1)


def _block_diag_ones(width):
    r = np.arange(width)[:, None] // HEAD_DIM
    c = np.arange(width)[None, :] // HEAD_DIM
    return jnp.asarray((r == c).astype(np.float32), dtype=BF16)


def _repack_w_in(w):
    o = np.cumsum([0, ATT_W, ATT_W, ATT_W, IDXQ_W, IDX_DIM, N_IDX_HEADS, RET_W, RET_W, RET_W, RET_W, MEM_W])
    w = w.astype(BF16)
    pad = jnp.zeros((w.shape[0], KW_W - IDX_DIM - N_IDX_HEADS), BF16)
    return jnp.concatenate([w[:, o[0]:o[4]], w[:, o[4]:o[6]], pad, w[:, o[6]:o[11]]], axis=1)


def kernel(x_prompt, x_sample, mem_prompt, cache_k, cache_v, cache_kidx, state_ret, cache_mem_k, cache_mem_v,
           page_table, ffn1_norm_g, ffn1_w_gu, ffn1_w_down, mix_norm_g, w_in, att_q_norm_g, att_k_norm_g,
           ret_gn_g, mem_norm_g, w_mem_kv, mem_q_norm_g, mem_k_norm_g, w_out, ffn2_norm_g, ffn2_w_gu, ffn2_w_down):
    n_b, s_len, d = x_prompt.shape
    n_s, t_s, _ = x_sample.shape
    assert t_s == 1, "the sample group decodes one token per sequence"
    depth = w_in.shape[0]
    n_mem = mem_prompt.shape[1]
    n_pool = cache_k.shape[1]
    past_len = page_table.shape[1] * PAGE_SIZE
    page_table = page_table.astype(I32)

    cos_p, sin_p = _rope_tables(jnp.arange(s_len, dtype=I32))
    cos_s, sin_s = _rope_tables(jnp.full((n_s,), past_len, I32))
    bd = _block_diag_ones(ATT_W)
    bd_ret = _block_diag_ones(RET_W)
    tile8 = lambda g_: jnp.tile(g_, N_ATT_HEADS).reshape(1, ATT_W)
    tile4 = lambda g_: jnp.tile(g_, N_MEM_HEADS).reshape(1, MEM_W)

    yp = x_prompt.reshape(n_b * s_len, d)
    ys = x_sample.reshape(n_s, d)
    mem = mem_prompt.reshape(n_b * n_mem, d)
    outs = [[] for _ in range(10)]
    for l in range(depth):
        g1 = ffn1_norm_g[l].reshape(1, d)
        g2 = ffn2_norm_g[l].reshape(1, d)
        gmix = mix_norm_g[l].reshape(1, d)
        w1gu, w1d = ffn1_w_gu[l].astype(BF16), ffn1_w_down[l].astype(BF16)
        w2gu, w2d = ffn2_w_gu[l].astype(BF16), ffn2_w_down[l].astype(BF16)
        w_in_l = _repack_w_in(w_in[l])
        w_out_l = w_out[l].astype(BF16)
        gq, gk, gm = tile8(att_q_norm_g[l]), tile8(att_k_norm_g[l]), tile4(mem_q_norm_g[l])
        gn = ret_gn_g[l].reshape(1, RET_W)

        yp = _ffn(yp, g1, w1gu, w1d)
        (q, k, kb, v, vt, qi, kw, ki, ki2, rq, rk, rv, rg, mq) = _proj(
            yp, gmix, w_in_l, gq, gk, gm, cos_p, sin_p, bd, n_b)
        mk, mv = _memkv(mem, mem_norm_g[l].reshape(1, d), w_mem_kv[l].astype(BF16), tile4(mem_k_norm_g[l]), bd_ret)
        att = _dsa_prompt_pairs(q, qi, kw, kb, vt, ki2, n_b, s_len)
        ret, s_fin = _ret_prompt(rq, rk, rv, n_b, s_len)
        mo = _mematt_prompt(mq, mk, mv, n_b)
        yp = _mixout_ffn(yp, att, ret, rg, mo, gn, bd_ret, w_out_l, g2, w2gu, w2d)
        outs[0].append(k.reshape(n_b, s_len, N_ATT_HEADS, HEAD_DIM))
        outs[1].append(v.reshape(n_b, s_len, N_ATT_HEADS, HEAD_DIM))
        outs[2].append(ki.reshape(n_b, s_len, IDX_DIM))
        outs[3].append(s_fin)
        outs[4].append(mk.reshape(n_b, n_mem, N_MEM_HEADS, HEAD_DIM))
        outs[5].append(mv.reshape(n_b, n_mem, N_MEM_HEADS, HEAD_DIM))

        ys = _ffn(ys, g1, w1gu, w1d)
        (q, k, kb, v, vt, qi, kw, ki, ki2, rq, rk, rv, rg, mq) = _proj(
            ys, gmix, w_in_l, gq, gk, gm, cos_s, sin_s, bd, 1)
        wi = kw[:, IDX_DIM:IDX_DIM + N_IDX_HEADS]
        sc_past, sc_self = _sample_scores(page_table, qi, wi, ki, jnp.transpose(cache_kidx[l], (0, 2, 1)))
        topk = min(TOPK_MAX, (past_len + t_s) // 4)
        thr, cut = _sample_select(sc_past.reshape(n_s, past_len), sc_self.reshape(n_s, LANES), topk)
        k_t = jnp.transpose(cache_k[l], (0, 2, 3, 1)).reshape(n_pool, ATT_W, PAGE_SIZE)
        v_t = jnp.transpose(cache_v[l], (0, 2, 3, 1)).reshape(n_pool, ATT_W, PAGE_SIZE)
        att = _sample_attend(page_table, thr[:, 0], cut[:, 0], q, k, v, sc_past, sc_self, k_t, v_t)
        ret, s_new = _ret_sample(state_ret[l].astype(F32), rq, rk, rv)
        slot_minor = lambda a: jnp.transpose(a, (0, 2, 3, 1)).reshape(n_s, MEM_W, n_mem)
        mo = _mematt_sample(mq, slot_minor(cache_mem_k[l]), slot_minor(cache_mem_v[l]))
        ys = _mixout_ffn(ys, att, ret, rg, mo, gn, bd_ret, w_out_l, g2, w2gu, w2d)
        outs[6].append(k.reshape(n_s, t_s, N_ATT_HEADS, HEAD_DIM))
        outs[7].append(v.reshape(n_s, t_s, N_ATT_HEADS, HEAD_DIM))
        outs[8].append(ki.reshape(n_s, t_s, IDX_DIM))
        outs[9].append(s_new)

    return (yp.reshape(n_b, s_len, d), ys.reshape(n_s, t_s, d)) + tuple(jnp.stack(o) for o in outs)
```

```python
import functools
import math

import numpy as np
import jax
import jax.numpy as jnp
from jax import lax
from jax.experimental import pallas as pl
from jax.experimental.pallas import tpu as pltpu

F32 = jnp.float32
BF16 = jnp.bfloat16
I32 = jnp.int32

HEAD_DIM = 64
N_ATT_HEADS = 8
N_IDX_HEADS = 8
IDX_DIM = 64
TOPK_MAX = 256
N_RET_HEADS = 4
N_MEM_HEADS = 4
PAGE_SIZE = 128
EPS = 1e-6
ROPE_BASE = 10000.0

ATT_W = N_ATT_HEADS * HEAD_DIM
RET_W = N_RET_HEADS * HEAD_DIM
MEM_W = N_MEM_HEADS * HEAD_DIM
IDXQ_W = N_IDX_HEADS * IDX_DIM
LANES = 128
KW_W = LANES
C_Q = 0
C_K = C_Q + ATT_W
C_V = C_K + ATT_W
C_QI = C_V + ATT_W
C_KW = C_QI + IDXQ_W
C_RQ = C_KW + KW_W
C_RK = C_RQ + RET_W
C_RV = C_RK + RET_W
C_RG = C_RV + RET_W
C_MQ = C_RG + RET_W
PROJ_W = C_MQ + MEM_W

QK_SCALE = HEAD_DIM ** -0.5
IDX_SCALE = IDX_DIM ** -0.5
IDX_HEAD_SCALE = N_IDX_HEADS ** -0.5
NEG_BIG = -1e30
M_INIT = -(2.0 ** 100)
INT_MIN = -(2 ** 31)
F32_LOWEST = float(np.finfo(np.float32).min)
LOG_G = [float(np.log1p(np.float32(-(2.0 ** (-5.0 - h))))) for h in range(N_RET_HEADS)]

VMEM_LIMIT = 56 * 1024 * 1024
ROW_TILE = 512
KEY_CHUNK = 512
Q_TILE = LANES
VT_ROWS = LANES + 16
BISECT_GROUP = 4
RET_CHUNK = 128
PAGES_PER_STEP = 16
SAMPLE_GROUP = 8


def _cparams(sem):
    return pltpu.CompilerParams(dimension_semantics=sem, vmem_limit_bytes=VMEM_LIMIT)


def _resident(shape):
    nd = len(shape)
    return pl.BlockSpec(shape, lambda *_: (0,) * nd, pipeline_mode=pl.Buffered(1))


def _rms(x, g):
    return x * lax.rsqrt(jnp.mean(x * x, axis=-1, keepdims=True) + EPS) * g


def _head_rms(z, g, bd):
    z2 = z * z
    hi = z2.astype(BF16)
    lo = (z2 - hi.astype(F32)).astype(BF16)
    ss = jnp.dot(hi, bd, preferred_element_type=F32) + jnp.dot(lo, bd, preferred_element_type=F32)
    return z * lax.rsqrt(ss * (1.0 / HEAD_DIM) + EPS) * g


def _nt(a, b):
    return lax.dot_general(a, b, (((1,), (1,)), ((), ())), preferred_element_type=F32)


def _from_key(k):
    return lax.bitcast_convert_type(jnp.where(k < 0, k ^ jnp.int32(0x7FFFFFFF), k), F32)


def _finite_threshold(thr_key):
    return jnp.where(thr_key == INT_MIN, F32_LOWEST, jnp.maximum(_from_key(thr_key), F32_LOWEST))


def _ffn_half_step(x, g, wgu_ref, wd_ref, d_ff, fc):
    h = _rms(x, g).astype(BF16)
    acc = jnp.zeros(x.shape, F32)
    for c in range(d_ff // fc):
        gate = jnp.dot(h, wgu_ref[:, c * fc:(c + 1) * fc], preferred_element_type=F32)
        up = jnp.dot(h, wgu_ref[:, d_ff + c * fc:d_ff + (c + 1) * fc], preferred_element_type=F32)
        act = (gate * jax.nn.sigmoid(gate) * up).astype(BF16)
        acc = acc + jnp.dot(act, wd_ref[c * fc:(c + 1) * fc, :], preferred_element_type=F32)
    return x + 0.5 * acc


def _ffn_chunk(d_ff):
    return d_ff // 2 if (d_ff // 2) % LANES == 0 else d_ff


def _ffn_body(x_ref, g_ref, wgu_ref, wd_ref, o_ref, *, d_ff, fc):
    o_ref[...] = _ffn_half_step(x_ref[...], g_ref[...], wgu_ref, wd_ref, d_ff, fc)


def _ffn(x, g, wgu, wd):
    t, d = x.shape
    d_ff = wd.shape[0]
    tm = min(ROW_TILE, t)
    fc = _ffn_chunk(d_ff)
    return pl.pallas_call(
        functools.partial(_ffn_body, d_ff=d_ff, fc=fc),
        grid=(t // tm,),
        in_specs=[pl.BlockSpec((tm, d), lambda i: (i, 0)), _resident((1, d)),
                  _resident(wgu.shape), _resident(wd.shape)],
        out_specs=pl.BlockSpec((tm, d), lambda i: (i, 0)),
        out_shape=jax.ShapeDtypeStruct((t, d), F32),
        compiler_params=_cparams(("parallel",)),
        name="ffn_half",
    )(x, g, wgu, wd)


def _proj_body(x_ref, g_ref, w_ref, gq_ref, gk_ref, gm_ref, cos_ref, sin_ref, bd_ref,
               q_ref, k_ref, kb_ref, v_ref, vt_ref, qi_ref, kw_ref, ki_ref, ki2_ref,
               rq_ref, rk_ref, rv_ref, rg_ref, mq_ref):
    h = _rms(x_ref[...], g_ref[...]).astype(BF16)

    def proj(a, width):
        return jnp.dot(h, w_ref[:, a:a + width], preferred_element_type=F32)

    bd = bd_ref[...]
    q = _head_rms(proj(C_Q, ATT_W), gq_ref[...], bd)
    q_ref[...] = (q * QK_SCALE).astype(BF16)
    k = _head_rms(proj(C_K, ATT_W), gk_ref[...], bd)
    k_ref[...] = k
    kb_ref[...] = k.astype(BF16)
    v = proj(C_V, ATT_W)
    v_ref[...] = v
    vt = jnp.transpose(v)
    tail_row = lax.broadcasted_iota(I32, (VT_ROWS - LANES, vt.shape[1]), 0)
    tail = jnp.where(tail_row == 0, 1.0, 0.0)
    for p in range(N_ATT_HEADS // 2):
        vt_ref[p] = jnp.concatenate([vt[p * LANES:(p + 1) * LANES, :], tail], axis=0).astype(BF16)
    qi_ref[...] = proj(C_QI, IDXQ_W).astype(BF16)
    kw = proj(C_KW, KW_W)
    kw_ref[...] = kw
    ki = kw[:, :IDX_DIM]
    ki_ref[...] = ki
    kib = ki.astype(BF16)
    ki2_ref[...] = jnp.concatenate([kib, kib], axis=1)

    lane = lax.broadcasted_iota(I32, (1, RET_W), 1)
    first_half = (lane % HEAD_DIM) < (HEAD_DIM // 2)
    cos = jnp.concatenate([cos_ref[...]] * (RET_W // LANES), axis=1)
    sin = jnp.concatenate([sin_ref[...]] * (RET_W // LANES), axis=1)

    def rot(x):
        swapped = jnp.where(first_half, pltpu.roll(x, RET_W - HEAD_DIM // 2, axis=1),
                            pltpu.roll(x, HEAD_DIM // 2, axis=1))
        return x * cos + swapped * sin

    rq_ref[...] = rot(proj(C_RQ, RET_W))
    rk_ref[...] = rot(proj(C_RK, RET_W)) * QK_SCALE
    rv_ref[...] = proj(C_RV, RET_W)
    rg_ref[...] = proj(C_RG, RET_W)
    mq = _head_rms(proj(C_MQ, MEM_W), gm_ref[...], bd[:MEM_W, :MEM_W])
    mq_ref[...] = (mq * QK_SCALE).astype(BF16)


def _proj(y, g, w, gq, gk, gm, cos, sin, bd, n_batch):
    t, d = y.shape
    s_len = t // n_batch
    tm = min(ROW_TILE, s_len)
    nsb = s_len // tm
    row = lambda w_: pl.BlockSpec((tm, w_), lambda i: (i, 0))
    tab = pl.BlockSpec((tm, LANES), lambda i: (i % nsb, 0))
    out_shapes = [
        (jax.ShapeDtypeStruct((t, ATT_W), BF16), row(ATT_W)),
        (jax.ShapeDtypeStruct((t, ATT_W), F32), row(ATT_W)),
        (jax.ShapeDtypeStruct((t, ATT_W), BF16), row(ATT_W)),
        (jax.ShapeDtypeStruct((t, ATT_W), F32), row(ATT_W)),
        (jax.ShapeDtypeStruct((n_batch, nsb, N_ATT_HEADS // 2, VT_ROWS, tm), BF16),
         pl.BlockSpec((None, None, N_ATT_HEADS // 2, VT_ROWS, tm),
                      lambda i: (i // nsb, i % nsb, 0, 0, 0))),
        (jax.ShapeDtypeStruct((t, IDXQ_W), BF16), row(IDXQ_W)),
        (jax.ShapeDtypeStruct((t, KW_W), F32), row(KW_W)),
        (jax.ShapeDtypeStruct((t, IDX_DIM), F32), row(IDX_DIM)),
        (jax.ShapeDtypeStruct((t, 2 * IDX_DIM), BF16), row(2 * IDX_DIM)),
        (jax.ShapeDtypeStruct((t, RET_W), F32), row(RET_W)),
        (jax.ShapeDtypeStruct((t, RET_W), F32), row(RET_W)),
        (jax.ShapeDtypeStruct((t, RET_W), F32), row(RET_W)),
        (jax.ShapeDtypeStruct((t, RET_W), F32), row(RET_W)),
        (jax.ShapeDtypeStruct((t, MEM_W), BF16), row(MEM_W)),
    ]
    return pl.pallas_call(
        _proj_body,
        grid=(t // tm,),
        in_specs=[row(d), _resident((1, d)), _resident(w.shape), _resident((1, ATT_W)),
                  _resident((1, ATT_W)), _resident((1, MEM_W)), tab, tab, _resident(bd.shape)],
        out_specs=[s for _, s in out_shapes],
        out_shape=[s for s, _ in out_shapes],
        compiler_params=_cparams(("parallel",)),
        name="mix_proj",
    )(y, g, w, gq, gk, gm, cos, sin, bd)


def _memkv_body(x_ref, g_ref, w_ref, gk_ref, bd_ref, mk_ref, mv_ref):
    h = _rms(x_ref[...], g_ref[...]).astype(BF16)
    z = jnp.dot(h, w_ref[...], preferred_element_type=F32)
    mk_ref[...] = _head_rms(z[:, :MEM_W], gk_ref[...], bd_ref[...])
    mv_ref[...] = z[:, MEM_W:]


def _memkv(mem, g, w, gk, bd):
    t, d = mem.shape
    tm = min(ROW_TILE, t)
    row = lambda w_: pl.BlockSpec((tm, w_), lambda i: (i, 0))
    return pl.pallas_call(
        _memkv_body,
        grid=(t // tm,),
        in_specs=[row(d), _resident((1, d)), _resident(w.shape), _resident((1, MEM_W)), _resident(bd.shape)],
        out_specs=[row(MEM_W), row(MEM_W)],
        out_shape=[jax.ShapeDtypeStruct((t, MEM_W), F32)] * 2,
        compiler_params=_cparams(("parallel",)),
        name="mem_kv",
    )(mem, g, w, gk, bd)


def _dsa_pair_body(q_ref, qi_ref, kw_ref, k_ref, vt_ref, ki2_ref, o_ref,
                   key_scr, qi2_scr, q2_scr, cut_scr, m_scr, acc_scr, sa_scr, sb_scr, *, topk, sc, s_len):
    n_st = 2
    jj = pl.program_id(1)
    n_ch = ((n_st * jj + 1) * Q_TILE + Q_TILE + sc - 1) // sc
    n_pair = N_ATT_HEADS // 2
    lane = lax.broadcasted_iota(I32, (1, LANES), 1)
    qpos = [(n_st * jj + s) * Q_TILE + lane for s in range(n_st)]
    kiota = lax.broadcasted_iota(I32, (sc, LANES), 0)
    streams = range(n_st)

    r2 = lax.broadcasted_iota(I32, (2 * Q_TILE, LANES), 0)
    c2 = lax.broadcasted_iota(I32, (2 * Q_TILE, LANES), 1)
    keep = (r2 < Q_TILE) == (c2 < HEAD_DIM)
    eye2 = jnp.where(r2 % Q_TILE == c2, 1.0, 0.0).astype(BF16)
    w_s = []
    for s in streams:
        rows = slice(s * Q_TILE, (s + 1) * Q_TILE)
        for p in range(n_pair):
            a = qi_ref[rows, p * LANES:(p + 1) * LANES].astype(F32)
            qi2_scr[s, p] = jnp.where(keep, jnp.concatenate([a, a], axis=0), 0.0).astype(BF16)
            a = q_ref[rows, p * LANES:(p + 1) * LANES].astype(F32)
            q2 = jnp.where(keep, jnp.concatenate([a, a], axis=0), 0.0).astype(BF16)
            q2_scr[s, p] = jnp.concatenate([q2, eye2], axis=1)
        w_t = jnp.transpose(kw_ref[rows, :])[IDX_DIM:IDX_DIM + N_IDX_HEADS, :]
        w_s.append((w_t * IDX_HEAD_SCALE) * IDX_SCALE)

    def score_chunk(c, carry):
        off = pl.multiple_of(c * sc, sc)
        kc = ki2_ref[pl.ds(off, sc), :]
        acc = [jnp.zeros((sc, LANES), F32) for _ in streams]
        for p in range(n_pair):
            for s in streams:
                lg = _nt(kc, qi2_scr[s, p])
                acc[s] = acc[s] + jnp.maximum(lg[:, :LANES], 0.0) * w_s[s][2 * p:2 * p + 1, :]
                acc[s] = acc[s] + jnp.maximum(lg[:, LANES:], 0.0) * w_s[s][2 * p + 1:2 * p + 2, :]
        for s in streams:
            key_scr[s, pl.ds(off, sc), :] = jnp.where(off + kiota <= qpos[s], acc[s], -jnp.inf)
        return carry

    lax.fori_loop(0, n_ch, score_chunk, 0)

    def count(preds):
        def body(c, accs):
            off = pl.multiple_of(c * sc, sc)
            accs = list(accs)
            for g in range(sc // 64):
                rows = pl.ds(off + g * 64, 64)
                for s in streams:
                    hit = preds[s](key_scr[s, rows, :], off + kiota[g * 64:(g + 1) * 64])
                    accs[s] = jnp.where(hit, accs[s] + 1.0, accs[s])
            return tuple(accs)
        accs = lax.fori_loop(0, n_ch, body, tuple(jnp.zeros((64, LANES), F32) for _ in streams))
        return [a.reshape(8, 8, LANES).sum(axis=0).sum(axis=0, keepdims=True) for a in accs]

    def bisect_group(state):
        i, thrs, actives, _ = state
        thrs, actives = list(thrs), list(actives)
        for _ in range(BISECT_GROUP):
            cands = [t + lax.shift_left(jnp.int32(1), 31 - i) for t in thrs]
            cnts = count([functools.partial(lambda x, _, cand: x >= cand, cand=_from_key(cand)) for cand in cands])
            for s in streams:
                thrs[s] = jnp.where((cnts[s] >= topk) & (actives[s] > 0.0), cands[s], thrs[s])
                actives[s] = jnp.where(cnts[s] == topk, 0.0, actives[s])
            i = i + 1
        return i, tuple(thrs), tuple(actives), jnp.max(jnp.maximum(actives[0], actives[1]))

    actives0 = tuple(jnp.where(qpos[s] + 1 <= topk, 0.0, 1.0) for s in streams)
    _, thrs, actives, n_active = lax.while_loop(
        lambda st: (st[0] < 32) & (st[3] > 0.0), bisect_group,
        (jnp.int32(0), tuple(jnp.full((1, LANES), INT_MIN, I32) for _ in streams), actives0,
         jnp.max(jnp.maximum(actives0[0], actives0[1]))))
    for s in streams:
        cut_scr[s] = jnp.full((1, LANES), s_len, I32)

    thr_f = [_finite_threshold(t) for t in thrs]

    @pl.when(n_active > 0.0)
    def _():
        c_gt = count([functools.partial(lambda x, _, t: x > t, t=t) for t in thr_f])
        c_ge = count([functools.partial(lambda x, _, t: x >= t, t=t) for t in thr_f])
        need = [topk - c for c in c_gt]
        tie = [((c_ge[s] - c_gt[s]) > need[s]) & (actives[s] > 0.0) for s in streams]
        nbits = max(1, (s_len - 1).bit_length())

        def bisect_pos(i, los):
            cands = [lo + lax.shift_left(jnp.int32(1), nbits - 1 - i) for lo in los]
            cnts = count([functools.partial(lambda x, kpos, t, cand: jnp.where(x == t, kpos, s_len) < cand,
                                            t=thr_f[s], cand=cands[s]) for s in streams])
            return tuple(jnp.where(cnts[s] < need[s], cands[s], los[s]) for s in streams)

        los = lax.fori_loop(0, nbits, bisect_pos, tuple(jnp.zeros((1, LANES), I32) for _ in streams))
        for s in streams:
            cut_scr[s] = jnp.where(tie[s], los[s], s_len)

    cuts = [cut_scr[s] for s in streams]

    m_scr[...] = jnp.full(m_scr.shape, M_INIT, F32)
    acc_scr[...] = jnp.zeros(acc_scr.shape, F32)

    def qk_chunk(c, s_scr):
        off = pl.multiple_of(c * sc, sc)
        biases = []
        for s in streams:
            bias = _select_bias(key_scr[s, pl.ds(off, sc), :], off + kiota, thr_f[s], cuts[s])
            biases.append(bias.astype(BF16))
        for p in range(n_pair):
            kp = k_ref[pl.ds(off, sc), p * LANES:(p + 1) * LANES]
            for s in streams:
                s_scr[s, p] = _nt(jnp.concatenate([kp, biases[s]], axis=1), q2_scr[s, p])

    def softmax_pv_chunk(c, s_scr):
        es, alphas = {}, {}
        for p in range(n_pair):
            for s in streams:
                x = s_scr[s, p].astype(BF16)
                m_c = x.reshape(sc // 64, 64, 2 * Q_TILE).max(axis=0)
                m_c = m_c.reshape(4, 16, 2 * Q_TILE).max(axis=0).astype(F32).max(axis=0, keepdims=True)
                m_old = m_scr[s, p]
                m_new = jnp.maximum(m_old, m_c)
                alphas[s, p] = jnp.exp(m_old - m_new)
                es[s, p] = m_new.astype(BF16)
                m_scr[s, p] = m_new
            for s in streams:
                pv = acc_scr[s, p] * alphas[s, p]
                half = sc // 2
                for r in range(2):
                    e = jnp.exp(s_scr[s, p, r * half:(r + 1) * half, :].astype(BF16) - es[s, p])
                    pv = pv + jnp.dot(vt_ref[c, p, :, r * half:(r + 1) * half], e, preferred_element_type=F32)
                acc_scr[s, p] = pv

    def att_two_chunks(i, carry):
        c = 2 * i
        qk_chunk(c + 1, sb_scr)
        softmax_pv_chunk(c, sa_scr)
        qk_chunk(c + 2, sa_scr)
        softmax_pv_chunk(c + 1, sb_scr)
        return carry

    qk_chunk(0, sa_scr)
    n_two = (n_ch - 1) // 2
    lax.fori_loop(0, n_two, att_two_chunks, 0)

    @pl.when(n_ch % 2 == 0)
    def _():
        qk_chunk(n_ch - 1, sb_scr)
        softmax_pv_chunk(n_ch - 2, sa_scr)
        softmax_pv_chunk(n_ch - 1, sb_scr)

    @pl.when(n_ch % 2 == 1)
    def _():
        softmax_pv_chunk(n_ch - 1, sa_scr)

    for s in streams:
        outs = []
        for p in range(n_pair):
            a = acc_scr[s, p]
            l = a[LANES:LANES + 1, :]
            outs.append(a[:HEAD_DIM, :LANES] / l[:, :LANES])
            outs.append(a[HEAD_DIM:LANES, LANES:] / l[:, LANES:])
        o_ref[s * Q_TILE:(s + 1) * Q_TILE, :] = jnp.transpose(jnp.concatenate(outs, axis=0))


def _dsa_prompt_pairs(q, qi, kw, kb, vt, ki2, n_batch, s_len):
    t = q.shape[0]
    sc = min(KEY_CHUNK, s_len)
    n_st = 2
    nq = s_len // (n_st * Q_TILE)
    n_pair = N_ATT_HEADS // 2
    topk = min(TOPK_MAX, s_len // 4)
    qrow = lambda w_: pl.BlockSpec((n_st * Q_TILE, w_), lambda b, j: (b * nq + j, 0))
    per_batch = lambda w_: pl.BlockSpec((s_len, w_), lambda b, j: (b, 0), pipeline_mode=pl.Buffered(1))
    return pl.pallas_call(
        functools.partial(_dsa_pair_body, topk=topk, sc=sc, s_len=s_len),
        grid=(n_batch, nq),
        in_specs=[qrow(ATT_W), qrow(IDXQ_W), qrow(KW_W), per_batch(ATT_W),
                  pl.BlockSpec((None, s_len // sc, n_pair, VT_ROWS, sc), lambda b, j: (b, 0, 0, 0, 0),
                               pipeline_mode=pl.Buffered(1)),
                  per_batch(2 * IDX_DIM)],
        out_specs=qrow(ATT_W),
        out_shape=jax.ShapeDtypeStruct((t, ATT_W), F32),
        scratch_shapes=[pltpu.VMEM((n_st, s_len, LANES), F32),
                        pltpu.VMEM((n_st, n_pair, 2 * Q_TILE, LANES), BF16),
                        pltpu.VMEM((n_st, n_pair, 2 * Q_TILE, 2 * LANES), BF16),
                        pltpu.VMEM((n_st, 1, LANES), I32),
                        pltpu.VMEM((n_st, n_pair, 1, 2 * Q_TILE), F32),
                        pltpu.VMEM((n_st, n_pair, VT_ROWS, 2 * Q_TILE), F32),
                        pltpu.VMEM((n_st, n_pair, sc, 2 * Q_TILE), F32),
                        pltpu.VMEM((n_st, n_pair, sc, 2 * Q_TILE), F32)],
        compiler_params=_cparams(("parallel", "arbitrary")),
        name="dsa_prompt",
    )(q, qi, kw, kb, vt, ki2)


def _ret_lane_const(vals):
    lane = lax.broadcasted_iota(I32, (1, RET_W), 1)
    out = jnp.zeros((1, RET_W), F32)
    for h, v in enumerate(vals):
        out = jnp.where(lane // HEAD_DIM == h, v, out)
    return out


def _ret_body(rq_ref, rk_ref, rv_ref, o_ref, st_ref, sbd_scr, *, ch):
    c = pl.program_id(0)
    n_seq = sbd_scr.shape[0]

    @pl.when(c == 0)
    def _():
        sbd_scr[...] = jnp.zeros(sbd_scr.shape, F32)

    head = lax.broadcasted_iota(I32, (1, RET_W), 1) // HEAD_DIM
    log_g = _ret_lane_const(LOG_G)
    i_col = lax.broadcasted_iota(I32, (ch, 1), 0).astype(F32)
    q_dec = jnp.exp(log_g * (i_col + 1.0))
    k_dec = jnp.exp(log_g * (ch - 1.0 - i_col))
    ii = lax.broadcasted_iota(I32, (ch, ch), 0)
    jj = lax.broadcasted_iota(I32, (ch, ch), 1)
    causal = ii >= jj
    diff = jnp.where(causal, ii - jj, 0).astype(F32)
    decays = [jnp.where(causal, jnp.exp(LOG_G[h] * diff), 0.0) for h in range(N_RET_HEADS)]
    rh = lax.broadcasted_iota(I32, (RET_W, RET_W), 0) // HEAD_DIM
    ch_ = lax.broadcasted_iota(I32, (RET_W, RET_W), 1) // HEAD_DIM

    for b in range(n_seq):
        q = rq_ref[b]
        k = rk_ref[b]
        vb = rv_ref[b].astype(BF16)
        kb = k.astype(BF16)
        state = sbd_scr[b]
        cross = jnp.dot(q.astype(BF16), state.astype(BF16), preferred_element_type=F32) * q_dec
        inner = jnp.zeros((ch, RET_W), F32)
        for h in range(N_RET_HEADS):
            qm = jnp.where(head == h, q, 0.0).astype(BF16)
            sc = (_nt(qm, kb) * decays[h]).astype(BF16)
            inner = inner + jnp.where(head == h, jnp.dot(sc, vb, preferred_element_type=F32), 0.0)
        o_ref[b] = inner + cross

        kd = (k * k_dec).astype(BF16)
        kv = lax.dot_general(kd, vb, (((0,), (0,)), ((), ())), preferred_element_type=F32)
        new_state = state * jnp.exp(log_g * float(ch)) + jnp.where(rh == ch_, kv, 0.0)
        sbd_scr[b] = new_state

    @pl.when(c == pl.num_programs(0) - 1)
    def _():
        for b in range(n_seq):
            for h in range(N_RET_HEADS):
                st_ref[b, h] = sbd_scr[b, h * HEAD_DIM:(h + 1) * HEAD_DIM, h * HEAD_DIM:(h + 1) * HEAD_DIM]


def _ret_prompt(rq, rk, rv, n_batch, s_len):
    ch = min(RET_CHUNK, s_len)
    blk = pl.BlockSpec((n_batch, ch, RET_W), lambda c: (0, c, 0))
    as3d = lambda a: a.reshape(n_batch, s_len, RET_W)
    out, state = pl.pallas_call(
        functools.partial(_ret_body, ch=ch),
        grid=(s_len // ch,),
        in_specs=[blk, blk, blk],
        out_specs=[blk, pl.BlockSpec((n_batch, N_RET_HEADS, HEAD_DIM, HEAD_DIM), lambda c: (0, 0, 0, 0))],
        out_shape=[jax.ShapeDtypeStruct((n_batch, s_len, RET_W), F32),
                   jax.ShapeDtypeStruct((n_batch, N_RET_HEADS, HEAD_DIM, HEAD_DIM), F32)],
        scratch_shapes=[pltpu.VMEM((n_batch, RET_W, RET_W), F32)],
        compiler_params=_cparams(("arbitrary",)),
        name="ret_prompt",
    )(as3d(rq), as3d(rk), as3d(rv))
    return out.reshape(n_batch * s_len, RET_W), state


def _softmax_lanes(s):
    m = jnp.max(s, axis=-1, keepdims=True)
    e = jnp.exp(s - m)
    return e / jnp.sum(e, axis=-1, keepdims=True)


def _mematt_body(mq_ref, mk_ref, mv_ref, o_ref):
    mq = mq_ref[...].astype(F32)
    mk = mk_ref[...].astype(BF16)
    mv = mv_ref[...].astype(BF16)
    head = lax.broadcasted_iota(I32, (1, MEM_W), 1) // HEAD_DIM
    out = jnp.zeros(mq.shape, F32)
    for h in range(N_MEM_HEADS):
        qm = jnp.where(head == h, mq, 0.0).astype(BF16)
        p = _softmax_lanes(_nt(qm, mk)).astype(BF16)
        out = out + jnp.where(head == h, jnp.dot(p, mv, preferred_element_type=F32), 0.0)
    o_ref[...] = out


def _mematt_prompt(mq, mk, mv, n_batch):
    t = mq.shape[0]
    n_mem = mk.shape[0] // n_batch
    s_len = t // n_batch
    tm = min(ROW_TILE, s_len)
    nsb = s_len // tm
    row = pl.BlockSpec((tm, MEM_W), lambda i: (i, 0))
    mem = pl.BlockSpec((n_mem, MEM_W), lambda i: (i // nsb, 0))
    return pl.pallas_call(
        _mematt_body,
        grid=(t // tm,),
        in_specs=[row, mem, mem],
        out_specs=row,
        out_shape=jax.ShapeDtypeStruct((t, MEM_W), F32),
        compiler_params=_cparams(("parallel",)),
        name="mem_attend",
    )(mq, mk, mv)


def _mematt_sample_body(mq_ref, mk_ref, mv_ref, o_ref):
    rows = 8
    sel = lax.broadcasted_iota(I32, (rows, MEM_W), 1) // HEAD_DIM == lax.broadcasted_iota(I32, (rows, MEM_W), 0)
    for t in range(mq_ref.shape[0]):
        mq = jnp.broadcast_to(mq_ref[t].astype(F32), (rows, MEM_W))
        qbd = jnp.where(sel, mq, 0.0).astype(BF16)
        p = _softmax_lanes(jnp.dot(qbd, mk_ref[t].astype(BF16), preferred_element_type=F32)).astype(BF16)
        o = _nt(p, mv_ref[t].astype(BF16))
        o_ref[t] = jnp.sum(jnp.where(sel, o, 0.0), axis=0, keepdims=True)


def _mematt_sample(mq, mk_t, mv_t):
    n, n_mem = mk_t.shape[0], mk_t.shape[2]
    grp = SAMPLE_GROUP if n % SAMPLE_GROUP == 0 else 1
    one = pl.BlockSpec((grp, 1, MEM_W), lambda b: (b, 0, 0))
    mem = pl.BlockSpec((grp, MEM_W, n_mem), lambda b: (b, 0, 0))
    out = pl.pallas_call(
        _mematt_sample_body,
        grid=(n // grp,),
        in_specs=[one, mem, mem],
        out_specs=one,
        out_shape=jax.ShapeDtypeStruct((n, 1, MEM_W), F32),
        compiler_params=_cparams(("parallel",)),
        name="mem_attend_sample",
    )(mq.reshape(n, 1, MEM_W), mk_t, mv_t)
    return out.reshape(n, MEM_W)


def _mixout_ffn_body(y_ref, att_ref, ret_ref, rg_ref, mo_ref, gn_ref, bd_ref, w_ref, g2_ref, wgu_ref, wd_ref,
                     o_ref, *, d_ff, fc):
    retn = _head_rms(ret_ref[...], gn_ref[...], bd_ref[...])
    rg = rg_ref[...]
    gated = (rg * jax.nn.sigmoid(rg)) * retn
    acc = jnp.dot(att_ref[...].astype(BF16), w_ref[:ATT_W, :], preferred_element_type=F32)
    acc = acc + jnp.dot(gated.astype(BF16), w_ref[ATT_W:ATT_W + RET_W, :], preferred_element_type=F32)
    acc = acc + jnp.dot(mo_ref[...].astype(BF16), w_ref[ATT_W + RET_W:, :], preferred_element_type=F32)
    o_ref[...] = _ffn_half_step(y_ref[...] + acc, g2_ref[...], wgu_ref, wd_ref, d_ff, fc)


def _mixout_ffn(y, att, ret, rg, mo, gn, bd, w, g2, wgu, wd):
    t, d = y.shape
    d_ff = wd.shape[0]
    tm = min(ROW_TILE, t)
    row = lambda w_: pl.BlockSpec((tm, w_), lambda i: (i, 0))
    return pl.pallas_call(
        functools.partial(_mixout_ffn_body, d_ff=d_ff, fc=_ffn_chunk(d_ff)),
        grid=(t // tm,),
        in_specs=[row(d), row(ATT_W), row(RET_W), row(RET_W), row(MEM_W), _resident((1, RET_W)),
                  _resident(bd.shape), _resident(w.shape), _resident((1, d)), _resident(wgu.shape),
                  _resident(wd.shape)],
        out_specs=row(d),
        out_shape=jax.ShapeDtypeStruct((t, d), F32),
        compiler_params=_cparams(("parallel",)),
        name="mix_out_ffn",
    )(y, att, ret, rg, mo, gn, bd, w, g2, wgu, wd)


def _sample_score_body(pt_ref, qi_ref, w_ref, kin_ref, *rest, pps):
    page_refs, (past_ref, self_ref) = rest[:pps], rest[pps:]
    q8 = qi_ref[...]
    w8 = (w_ref[...] * IDX_HEAD_SCALE) * IDX_SCALE

    def head_sum(lg):
        s = jnp.sum(jnp.maximum(lg, 0.0) * w8, axis=0, keepdims=True)
        return jnp.where(s == 0.0, 0.0, s)

    for i in range(pps):
        lg = jnp.dot(q8, page_refs[i][...].astype(BF16), preferred_element_type=F32)
        past_ref[i:i + 1, :] = head_sum(lg)

    @pl.when(pl.program_id(1) == 0)
    def _():
        kn = kin_ref[...].astype(BF16).astype(F32)
        lg = jnp.sum(q8.astype(F32) * kn, axis=1, keepdims=True)
        self_ref[...] = jnp.broadcast_to(head_sum(lg), self_ref.shape)


def _sample_scores(page_table, qi, wi, ki_new, kidx_t):
    n, n_pages = page_table.shape
    pps = min(4 * PAGES_PER_STEP, n_pages)
    page_specs = [pl.BlockSpec((None, IDX_DIM, PAGE_SIZE),
                               functools.partial(lambda b, g, pt, i: (pt[b, g * pps + i], 0, 0), i=i))
                  for i in range(pps)]
    grid_spec = pltpu.PrefetchScalarGridSpec(
        num_scalar_prefetch=1,
        grid=(n, n_pages // pps),
        in_specs=[pl.BlockSpec((None, N_IDX_HEADS, IDX_DIM), lambda b, g, pt: (b, 0, 0)),
                  pl.BlockSpec((None, N_IDX_HEADS, 1), lambda b, g, pt: (b, 0, 0)),
                  pl.BlockSpec((None, 1, IDX_DIM), lambda b, g, pt: (b, 0, 0))] + page_specs,
        out_specs=[pl.BlockSpec((None, pps, PAGE_SIZE), lambda b, g, pt: (b, g, 0)),
                   pl.BlockSpec((None, 1, LANES), lambda b, g, pt: (b, 0, 0))],
    )
    return pl.pallas_call(
        functools.partial(_sample_score_body, pps=pps),
        grid_spec=grid_spec,
        out_shape=[jax.ShapeDtypeStruct((n, n_pages, PAGE_SIZE), F32),
                   jax.ShapeDtypeStruct((n, 1, LANES), F32)],
        compiler_params=_cparams(("parallel", "arbitrary")),
        name="sample_scores",
    )(page_table, qi.reshape(n, N_IDX_HEADS, IDX_DIM), wi.reshape(n, N_IDX_HEADS, 1),
      ki_new.reshape(n, 1, IDX_DIM), *([kidx_t] * pps))


def _sample_select_body(past_ref, self_ref, thr_ref, cut_ref, *, topk, past_len):
    past = past_ref[...]
    own = self_ref[...][:, 0:1]
    n = past.shape[0]
    kpos = lax.broadcasted_iota(I32, past.shape, 1)

    def count(pred):
        c = jnp.sum(jnp.where(pred(past, kpos), 1.0, 0.0), axis=1, keepdims=True)
        return c + jnp.where(pred(own, past_len), 1.0, 0.0)

    def bisect(i, thr_key):
        cand = thr_key + lax.shift_left(jnp.int32(1), 31 - i)
        cand_f = _from_key(cand)
        return jnp.where(count(lambda x, _: x >= cand_f) >= topk, cand, thr_key)

    thr = _finite_threshold(lax.fori_loop(0, 32, bisect, jnp.full((n, 1), INT_MIN, I32)))
    c_gt = count(lambda x, _: x > thr)
    c_ge = count(lambda x, _: x >= thr)
    need = topk - c_gt
    tie = (c_ge - c_gt) > need
    thr_ref[...] = jnp.broadcast_to(thr, thr_ref.shape)
    cut_ref[...] = jnp.full(cut_ref.shape, past_len + 1, I32)

    @pl.when(jnp.max(jnp.where(tie, 1.0, 0.0)) > 0.0)
    def _():
        nbits = max(1, past_len.bit_length())

        def bisect_pos(i, lo):
            cand = lo + lax.shift_left(jnp.int32(1), nbits - 1 - i)
            cnt = count(lambda x, pos: jnp.where(x == thr, pos, past_len + 1) < cand)
            return jnp.where(cnt < need, cand, lo)

        lo = lax.fori_loop(0, nbits, bisect_pos, jnp.zeros((n, 1), I32))
        cut_ref[...] = jnp.broadcast_to(jnp.where(tie, lo, past_len + 1), cut_ref.shape)


def _sample_select(sc_past, sc_self, topk):
    n, past_len = sc_past.shape
    full = lambda shape: pl.BlockSpec(shape, lambda i: (0,) * len(shape))
    return pl.pallas_call(
        functools.partial(_sample_select_body, topk=topk, past_len=past_len),
        grid=(1,),
        in_specs=[full((n, past_len)), full((n, LANES))],
        out_specs=[full((n, LANES)), full((n, LANES))],
        out_shape=[jax.ShapeDtypeStruct((n, LANES), F32), jax.ShapeDtypeStruct((n, LANES), I32)],
        compiler_params=_cparams(("arbitrary",)),
        name="sample_select",
    )(sc_past, sc_self)


def _select_bias(x, kpos, thr, cut):
    bias = jnp.where(x >= thr, 0.0, NEG_BIG)
    bias = jnp.where(x == jnp.inf, NEG_BIG, bias)
    return jnp.where(x == thr, jnp.where(kpos > cut, NEG_BIG, bias), bias)


def _heads_to_rows(x):
    h, w = x.shape
    return jnp.broadcast_to(x[:, None, :], (h, HEAD_DIM, w)).reshape(h * HEAD_DIM, w)


def _sample_att_body(pt_ref, thr_ref, cut_ref, q_ref, kn_ref, vn_ref, past_ref, self_ref, *rest, pps, past_len):
    k_refs, v_refs = rest[:pps], rest[pps:2 * pps]
    o_ref, m_scr, l_scr, acc_scr = rest[2 * pps:]
    b = pl.program_id(0)
    g = pl.program_id(1)
    thr = thr_ref[b]
    cut = cut_ref[b]

    @pl.when(g == 0)
    def _():
        m_scr[...] = jnp.full(m_scr.shape, NEG_BIG, F32)
        l_scr[...] = jnp.zeros(l_scr.shape, F32)
        acc_scr[...] = jnp.zeros(acc_scr.shape, F32)

    q = jnp.broadcast_to(q_ref[...].astype(F32), (ATT_W, PAGE_SIZE))
    lane = lax.broadcasted_iota(I32, (1, PAGE_SIZE), 1)

    def head_sums(x):
        return x.reshape(N_ATT_HEADS, HEAD_DIM, x.shape[1]).sum(axis=1)

    parts = []
    for i in range(pps):
        kpos = (g * pps + i) * PAGE_SIZE + lane
        bias = _select_bias(past_ref[i:i + 1, :], kpos, thr, cut)
        parts.append(head_sums(k_refs[i][...] * q) + bias)
    m_old = m_scr[...]
    m_new = m_old
    for s in parts:
        m_new = jnp.maximum(m_new, jnp.max(s, axis=1, keepdims=True))
    alpha = jnp.exp(m_old - m_new)
    l_new = alpha * l_scr[...]
    acc = acc_scr[...] * _heads_to_rows(alpha)
    for i in range(pps):
        e = jnp.exp(parts[i] - m_new)
        l_new = l_new + jnp.sum(e, axis=1, keepdims=True)
        acc = acc + _heads_to_rows(e) * v_refs[i][...]
    m_scr[...] = m_new
    l_scr[...] = l_new
    acc_scr[...] = acc

    @pl.when(g == pl.num_programs(1) - 1)
    def _():
        s_self = head_sums(kn_ref[...] * q_ref[...].astype(F32))
        s_self = s_self + _select_bias(self_ref[...][:, 0:1], past_len, thr, cut)
        m_fin = jnp.maximum(m_new, s_self)
        a2 = jnp.exp(m_new - m_fin)
        e_self = jnp.exp(s_self - m_fin)
        l_fin = a2 * l_new + e_self
        num = jnp.sum(acc, axis=1, keepdims=True) * _heads_to_rows(a2) + _heads_to_rows(e_self) * vn_ref[...]
        o_ref[...] = num / _heads_to_rows(l_fin)


def _sample_attend(page_table, thr, cut, q, k_new, v_new, sc_past, sc_self, k_t, v_t):
    n, n_pages = page_table.shape
    pps = min(PAGES_PER_STEP, n_pages)
    past_len = n_pages * PAGE_SIZE
    page_specs = [pl.BlockSpec((None, ATT_W, PAGE_SIZE),
                               functools.partial(lambda b, g, pt, th, cu, i: (pt[b, g * pps + i], 0, 0), i=i))
                  for i in range(pps)]
    col = pl.BlockSpec((None, ATT_W, 1), lambda b, g, pt, th, cu: (b, 0, 0))
    grid_spec = pltpu.PrefetchScalarGridSpec(
        num_scalar_prefetch=3,
        grid=(n, n_pages // pps),
        in_specs=[col, col, col,
                  pl.BlockSpec((None, pps, PAGE_SIZE), lambda b, g, pt, th, cu: (b, g, 0)),
                  pl.BlockSpec((None, 1, LANES), lambda b, g, pt, th, cu: (b, 0, 0))] + page_specs + page_specs,
        out_specs=col,
        scratch_shapes=[pltpu.VMEM((N_ATT_HEADS, 1), F32), pltpu.VMEM((N_ATT_HEADS, 1), F32),
                        pltpu.VMEM((ATT_W, PAGE_SIZE), F32)],
    )
    out = pl.pallas_call(
        functools.partial(_sample_att_body, pps=pps, past_len=past_len),
        grid_spec=grid_spec,
        out_shape=jax.ShapeDtypeStruct((n, ATT_W, 1), F32),
        compiler_params=_cparams(("parallel", "arbitrary")),
        name="sample_attend",
    )(page_table, thr, cut, q.reshape(n, ATT_W, 1), k_new.reshape(n, ATT_W, 1), v_new.reshape(n, ATT_W, 1),
      sc_past, sc_self, *([k_t] * pps), *([v_t] * pps))
    return out.reshape(n, ATT_W)


def _ret_sample_body(st_ref, q_ref, k_ref, v_ref, o_ref, ns_ref):
    state = st_ref[...]
    q = q_ref[...]
    k = k_ref[...]
    v = v_ref[...]
    hh = lax.broadcasted_iota(I32, (1, N_RET_HEADS, 1, 1), 1)
    g = jnp.zeros((1, N_RET_HEADS, 1, 1), F32)
    for h in range(N_RET_HEADS):
        g = jnp.where(hh == h, math.exp(LOG_G[h]), g)
    inner = jnp.sum(q * k, axis=2, keepdims=True) * v
    cross = jnp.sum(q * state, axis=2, keepdims=True) * g
    o_ref[...] = inner + cross
    ns_ref[...] = g * state + k * v


def _ret_sample(state, rq, rk, rv):
    n = state.shape[0]
    grp = SAMPLE_GROUP if n % SAMPLE_GROUP == 0 else 1
    col = pl.BlockSpec((grp, N_RET_HEADS, HEAD_DIM, 1), lambda b: (b, 0, 0, 0))
    rowv = pl.BlockSpec((grp, N_RET_HEADS, 1, HEAD_DIM), lambda b: (b, 0, 0, 0))
    st = pl.BlockSpec((grp, N_RET_HEADS, HEAD_DIM, HEAD_DIM), lambda b: (b, 0, 0, 0))
    out, new_state = pl.pallas_call(
        _ret_sample_body,
        grid=(n // grp,),
        in_specs=[st, col, col, rowv],
        out_specs=[rowv, st],
        out_shape=[jax.ShapeDtypeStruct((n, N_RET_HEADS, 1, HEAD_DIM), F32),
                   jax.ShapeDtypeStruct(state.shape, F32)],
        compiler_params=_cparams(("parallel",)),
        name="ret_sample",
    )(state, rq.reshape(n, N_RET_HEADS, HEAD_DIM, 1), rk.reshape(n, N_RET_HEADS, HEAD_DIM, 1),
      rv.reshape(n, N_RET_HEADS, 1, HEAD_DIM))
    return out.reshape(n, RET_W), new_state


def _rope_tables(pos):
    half = HEAD_DIM // 2
    inv = ROPE_BASE ** (-jnp.arange(half, dtype=F32) / half)
    ang = pos.astype(F32)[:, None] * inv[None, :]
    cos, sin = lax.optimization_barrier((jnp.cos(ang), jnp.sin(ang)))
    return jnp.concatenate([cos, cos, cos, cos], axis=1), jnp.concatenate([-sin, sin, -sin, sin], axis=1)


def _block_diag_ones(width):
    r = np.arange(width)[:, None] // HEAD_DIM
    c = np.arange(width)[None, :] // HEAD_DIM
    return jnp.asarray((r == c).astype(np.float32), dtype=BF16)


def _repack_w_in(w):
    o = np.cumsum([0, ATT_W, ATT_W, ATT_W, IDXQ_W, IDX_DIM, N_IDX_HEADS, RET_W, RET_W, RET_W, RET_W, MEM_W])
    w = w.astype(BF16)
    pad = jnp.zeros((w.shape[0], KW_W - IDX_DIM - N_IDX_HEADS), BF16)
    return jnp.concatenate([w[:, o[0]:o[4]], w[:, o[4]:o[6]], pad, w[:, o[6]:o[11]]], axis=1)


def kernel(x_prompt, x_sample, mem_prompt, cache_k, cache_v, cache_kidx, state_ret, cache_mem_k, cache_mem_v,
           page_table, ffn1_norm_g, ffn1_w_gu, ffn1_w_down, mix_norm_g, w_in, att_q_norm_g, att_k_norm_g,
           ret_gn_g, mem_norm_g, w_mem_kv, mem_q_norm_g, mem_k_norm_g, w_out, ffn2_norm_g, ffn2_w_gu, ffn2_w_down):
    n_b, s_len, d = x_prompt.shape
    n_s, t_s, _ = x_sample.shape
    assert t_s == 1, "the sample group decodes one token per sequence"
    depth = w_in.shape[0]
    n_mem = mem_prompt.shape[1]
    n_pool = cache_k.shape[1]
    past_len = page_table.shape[1] * PAGE_SIZE
    page_table = page_table.astype(I32)

    cos_p, sin_p = _rope_tables(jnp.arange(s_len, dtype=I32))
    cos_s, sin_s = _rope_tables(jnp.full((n_s,), past_len, I32))
    bd = _block_diag_ones(ATT_W)
    bd_ret = _block_diag_ones(RET_W)
    tile8 = lambda g_: jnp.tile(g_, N_ATT_HEADS).reshape(1, ATT_W)
    tile4 = lambda g_: jnp.tile(g_, N_MEM_HEADS).reshape(1, MEM_W)

    yp = x_prompt.reshape(n_b * s_len, d)
    ys = x_sample.reshape(n_s, d)
    mem = mem_prompt.reshape(n_b * n_mem, d)
    outs = [[] for _ in range(10)]
    for l in range(depth):
        g1 = ffn1_norm_g[l].reshape(1, d)
        g2 = ffn2_norm_g[l].reshape(1, d)
        gmix = mix_norm_g[l].reshape(1, d)
        w1gu, w1d = ffn1_w_gu[l].astype(BF16), ffn1_w_down[l].astype(BF16)
        w2gu, w2d = ffn2_w_gu[l].astype(BF16), ffn2_w_down[l].astype(BF16)
        w_in_l = _repack_w_in(w_in[l])
        w_out_l = w_out[l].astype(BF16)
        gq, gk, gm = tile8(att_q_norm_g[l]), tile8(att_k_norm_g[l]), tile4(mem_q_norm_g[l])
        gn = ret_gn_g[l].reshape(1, RET_W)

        yp = _ffn(yp, g1, w1gu, w1d)
        (q, k, kb, v, vt, qi, kw, ki, ki2, rq, rk, rv, rg, mq) = _proj(
            yp, gmix, w_in_l, gq, gk, gm, cos_p, sin_p, bd, n_b)
        mk, mv = _memkv(mem, mem_norm_g[l].reshape(1, d), w_mem_kv[l].astype(BF16), tile4(mem_k_norm_g[l]), bd_ret)
        att = _dsa_prompt_pairs(q, qi, kw, kb, vt, ki2, n_b, s_len)
        ret, s_fin = _ret_prompt(rq, rk, rv, n_b, s_len)
        mo = _mematt_prompt(mq, mk, mv, n_b)
        yp = _mixout_ffn(yp, att, ret, rg, mo, gn, bd_ret, w_out_l, g2, w2gu, w2d)
        outs[0].append(k.reshape(n_b, s_len, N_ATT_HEADS, HEAD_DIM))
        outs[1].append(v.reshape(n_b, s_len, N_ATT_HEADS, HEAD_DIM))
        outs[2].append(ki.reshape(n_b, s_len, IDX_DIM))
        outs[3].append(s_fin)
        outs[4].append(mk.reshape(n_b, n_mem, N_MEM_HEADS, HEAD_DIM))
        outs[5].append(mv.reshape(n_b, n_mem, N_MEM_HEADS, HEAD_DIM))

        ys = _ffn(ys, g1, w1gu, w1d)
        (q, k, kb, v, vt, qi, kw, ki, ki2, rq, rk, rv, rg, mq) = _proj(
            ys, gmix, w_in_l, gq, gk, gm, cos_s, sin_s, bd, 1)
        wi = kw[:, IDX_DIM:IDX_DIM + N_IDX_HEADS]
        sc_past, sc_self = _sample_scores(page_table, qi, wi, ki, jnp.transpose(cache_kidx[l], (0, 2, 1)))
        topk = min(TOPK_MAX, (past_len + t_s) // 4)
        thr, cut = _sample_select(sc_past.reshape(n_s, past_len), sc_self.reshape(n_s, LANES), topk)
        k_t = jnp.transpose(cache_k[l], (0, 2, 3, 1)).reshape(n_pool, ATT_W, PAGE_SIZE)
        v_t = jnp.transpose(cache_v[l], (0, 2, 3, 1)).reshape(n_pool, ATT_W, PAGE_SIZE)
        att = _sample_attend(page_table, thr[:, 0], cut[:, 0], q, k, v, sc_past, sc_self, k_t, v_t)
        ret, s_new = _ret_sample(state_ret[l].astype(F32), rq, rk, rv)
        slot_minor = lambda a: jnp.transpose(a, (0, 2, 3, 1)).reshape(n_s, MEM_W, n_mem)
        mo = _mematt_sample(mq, slot_minor(cache_mem_k[l]), slot_minor(cache_mem_v[l]))
        ys = _mixout_ffn(ys, att, ret, rg, mo, gn, bd_ret, w_out_l, g2, w2gu, w2d)
        outs[6].append(k.reshape(n_s, t_s, N_ATT_HEADS, HEAD_DIM))
        outs[7].append(v.reshape(n_s, t_s, N_ATT_HEADS, HEAD_DIM))
        outs[8].append(ki.reshape(n_s, t_s, IDX_DIM))
        outs[9].append(s_new)

    return (yp.reshape(n_b, s_len, d), ys.reshape(n_s, t_s, d)) + tuple(jnp.stack(o) for o in outs)
```
